```python
import jax
import jax.numpy as jnp
from jax import lax
import numpy as np

D_MODEL = 1024
BATCH = 32
SEQ = 2048
DEPTH = 2

GRID_W = 64
CTX_LEN = 256
NORM_EPS = 1e-6
ROPE_BASE = 10000.0

GLA_HEADS = 4
GLA_DK = 64
GLA_DV = 128
GLA_LOWRANK = 16
GLA_TEMP = 16.0
GLA_CHUNK = 64
GLA_QK = GLA_HEADS * GLA_DK
GLA_V = GLA_HEADS * GLA_DV
CONV_CH = 512
CONV_WIDTH = 31
NA_HEADS = 16
NA_HEAD_DIM = 64
NA_WIDTH = NA_HEADS * NA_HEAD_DIM
NA_WIN_R = 8
NA_WIN_C = 16
N_EXPERTS = 32
TOP_K = 4
D_EXPERT = 1024
SWIGLU_LIMIT = 7.0
SWIGLU_ALPHA = 1.702

OFF_Q = 0
OFF_G = OFF_Q + GLA_QK
OFF_GLU = OFF_G + GLA_V
OFF_K = OFF_GLU + 2 * CONV_CH
OFF_V = OFF_K + GLA_QK
OFF_AF = OFF_V + GLA_V
OFF_AB = OFF_AF + GLA_LOWRANK
W_IN_A = OFF_AB + GLA_LOWRANK
MIX_A_OUT = GLA_V + CONV_CH

kernel_name = 'hybrid_gla_conformer_natten_moe_dit'


def rmsnorm(x, g):
    xf = x.astype(jnp.float32)
    y = xf * lax.rsqrt(jnp.mean(xf * xf, axis=-1, keepdims=True) + NORM_EPS)
    return (y * g.astype(jnp.float32)).astype(x.dtype)


def layernorm(x, g, b):
    xf = x.astype(jnp.float32)
    mu = jnp.mean(xf, axis=-1, keepdims=True)
    var = jnp.mean(jnp.square(xf - mu), axis=-1, keepdims=True)
    y = (xf - mu) * lax.rsqrt(var + NORM_EPS)
    return (y * g.astype(jnp.float32) + b.astype(jnp.float32)).astype(x.dtype)


def split_heads(x, n_heads):
    b, t, _ = x.shape
    return x.reshape(b, t, n_heads, -1).transpose(0, 2, 1, 3)


def rope_1d(x, pos):
    half = x.shape[-1] // 2
    inv_freq = ROPE_BASE ** (-jnp.arange(half, dtype=jnp.float32) / half)
    ang = pos[:, None] * inv_freq[None, :]
    cos = jnp.cos(ang).astype(x.dtype)
    sin = jnp.sin(ang).astype(x.dtype)
    x1, x2 = x[..., :half], x[..., half:]
    return jnp.concatenate([x1 * cos - x2 * sin, x1 * sin + x2 * cos], axis=-1)


def axial_rope(x, row_pos, col_pos):
    half = x.shape[-1] // 2
    return jnp.concatenate([rope_1d(x[..., :half], row_pos), rope_1d(x[..., half:], col_pos)], axis=-1)


def flip_time(x):
    return x[:, :, ::-1]


def gla_log_decay(a_lr, w, b):
    z = (a_lr @ w + b).astype(jnp.float32)
    return split_heads(jax.nn.log_sigmoid(z) / GLA_TEMP, GLA_HEADS)


def gla_states(k, v, log_a, s0):
    b, h, t, _ = k.shape
    n = t // GLA_CHUNK
    kc = k.reshape(b, h, n, GLA_CHUNK, GLA_DK)
    vc = v.reshape(b, h, n, GLA_CHUNK, GLA_DV)
    bc = jnp.cumsum(log_a.reshape(b, h, n, GLA_CHUNK, GLA_DK), axis=3)
    b_last = bc[:, :, :, -1]
    kv = jnp.einsum('bhncd,bhnce->bhnde', kc * jnp.exp(b_last[:, :, :, None] - bc), vc)

    def step(s, inp):
        decay, upd = inp
        return decay[..., None] * s + upd, s

    s_final, s_before = lax.scan(step, s0, (jnp.moveaxis(jnp.exp(b_last), 2, 0), jnp.moveaxis(kv, 2, 0)))
    return jnp.moveaxis(s_before, 0, 2), s_final, bc


def gla_outputs(q, k, v, bc, s_before):
    b, h, t, _ = q.shape
    n = t // GLA_CHUNK
    qc = q.reshape(b, h, n, GLA_CHUNK, GLA_DK)
    kc = k.reshape(b, h, n, GLA_CHUNK, GLA_DK)
    vc = v.reshape(b, h, n, GLA_CHUNK, GLA_DV)
    q_t = qc * jnp.exp(bc)
    k_t = kc * jnp.exp(-bc)
    mask = jnp.tril(jnp.ones((GLA_CHUNK, GLA_CHUNK), dtype=bool))
    att = jnp.where(mask, jnp.einsum('bhncd,bhnsd->bhncs', q_t, k_t), 0.0)
    o = jnp.einsum('bhncs,bhnse->bhnce', att, vc) + jnp.einsum('bhncd,bhnde->bhnce', q_t, s_before)
    return o.reshape(b, h, t, GLA_DV)


def gla_scan(q, k, v, log_a, s0):
    s_before, s_final, bc = gla_states(k, v, log_a, s0)
    return gla_outputs(q, k, v, bc, s_before), s_final


def gla_bidir(q, k, v, la_f, la_b, s0_f, s0_b):
    o_f, s_f = gla_scan(q, k, v, la_f, s0_f)
    o_b, s_b = gla_scan(flip_time(q), flip_time(k), flip_time(v), flip_time(la_b), s0_b)
    return o_f + flip_time(o_b), s_f, s_b


def gla_post(o, g, norm_g):
    b, h, t, dv = o.shape
    of = o.astype(jnp.float32).transpose(0, 2, 1, 3)
    of = of * lax.rsqrt(jnp.mean(of * of, axis=-1, keepdims=True) + NORM_EPS)
    of = of * norm_g.astype(jnp.float32).reshape(h, dv)
    return of.reshape(b, t, h * dv).astype(g.dtype) * jax.nn.silu(g)


def conformer_conv(u2, conv_w, conv_b, ln_g, ln_b):
    a, gt = jnp.split(u2, 2, axis=-1)
    u = a * jax.nn.sigmoid(gt)
    y = lax.conv_general_dilated(
        u, conv_w[:, None, :].astype(u.dtype), window_strides=(1,),
        padding=[(CONV_WIDTH // 2, CONV_WIDTH // 2)],
        dimension_numbers=('NWC', 'WIO', 'NWC'), feature_group_count=CONV_CH)
    return jax.nn.silu(layernorm(y + conv_b, ln_g, ln_b))


def mixer_gla_conv(h_lat, h_ctx, w_in, wa_f, ba_f, wa_b, ba_b, norm_g, conv_w, conv_b, ln_g, ln_b, w_out, ctx_out):
    b, s, _ = h_lat.shape
    t = jnp.arange(s)
    row_pos = (t // GRID_W).astype(jnp.float32)
    col_pos = (t % GRID_W).astype(jnp.float32)
    q_scale = GLA_DK ** -0.5

    def sl(p, off, n, base):
        return p[..., off - base:off - base + n]

    def kv_decay(p, base):
        k = split_heads(sl(p, OFF_K, GLA_QK, base), GLA_HEADS)
        v = split_heads(sl(p, OFF_V, GLA_V, base), GLA_HEADS)
        la_f = gla_log_decay(sl(p, OFF_AF, GLA_LOWRANK, base), wa_f, ba_f)
        la_b = gla_log_decay(sl(p, OFF_AB, GLA_LOWRANK, base), wa_b, ba_b)
        return k, v, la_f, la_b

    def merge(o, p):
        y_gla = gla_post(o, sl(p, OFF_G, GLA_V, 0), norm_g)
        y_conv = conformer_conv(sl(p, OFF_GLU, 2 * CONV_CH, 0), conv_w, conv_b, ln_g, ln_b)
        return jnp.concatenate([y_gla, y_conv], axis=-1) @ w_out

    zero = jnp.zeros((h_ctx.shape[0], GLA_HEADS, GLA_DK, GLA_DV), jnp.float32)
    if ctx_out:
        p_c = h_ctx @ w_in
        k_c, v_c, laf_c, lab_c = kv_decay(p_c, 0)
        q_c = split_heads(sl(p_c, OFF_Q, GLA_QK, 0), GLA_HEADS) * q_scale
        o_c, s_f, s_b = gla_bidir(q_c, k_c, v_c, laf_c, lab_c, zero, zero)
        y_c = merge(o_c, p_c)
    else:
        p_c = h_ctx @ w_in[:, OFF_K:]
        k_c, v_c, laf_c, lab_c = kv_decay(p_c, OFF_K)
        s_f = gla_states(k_c, v_c, laf_c, zero)[1]
        s_b = gla_states(flip_time(k_c), flip_time(v_c), flip_time(lab_c), zero)[1]
        y_c = None
    p_l = h_lat @ w_in
    k_l, v_l, laf_l, lab_l = kv_decay(p_l, 0)
    q_l = axial_rope(split_heads(sl(p_l, OFF_Q, GLA_QK, 0), GLA_HEADS), row_pos, col_pos) * q_scale
    k_l = axial_rope(k_l, row_pos, col_pos)
    o_l, _, _ = gla_bidir(q_l, k_l, v_l, laf_l, lab_l, s_f, s_b)
    return merge(o_l, p_l), y_c


def softmax_attn(q, k, v):
    scale = q.shape[-1] ** -0.5
    sc = jnp.einsum('bqhd,bkhd->bhqk', q, k).astype(jnp.float32) * scale
    p = jax.nn.softmax(sc, axis=-1).astype(v.dtype)
    return jnp.einsum('bhqk,bkhd->bqhd', p, v)


def neighbourhood_attention(q, k, v, k_ctx, v_ctx, rpb):
    b, s, h, dh = q.shape
    rows = s // GRID_W
    win_r = min(NA_WIN_R, rows)
    scale = dh ** -0.5
    qg = q.reshape(b, rows, GRID_W, h, dh)
    kg = k.reshape(b, rows, GRID_W, h, dh)
    vg = v.reshape(b, rows, GRID_W, h, dh)
    cols = jnp.arange(GRID_W)
    col_start = jnp.clip(cols - NA_WIN_C // 2, 0, GRID_W - NA_WIN_C)
    col_mask = (cols[None, :] >= col_start[:, None]) & (cols[None, :] < col_start[:, None] + NA_WIN_C)
    dc_idx = jnp.clip(cols[None, :] - cols[:, None] + NA_WIN_C - 1, 0, 2 * NA_WIN_C - 2)
    rpb_col = rpb.astype(jnp.float32)[:, :, dc_idx]
    n_loc = win_r * GRID_W

    def row_block(r):
        r_start = jnp.clip(r - win_r // 2, 0, rows - win_r)
        q_r = lax.dynamic_index_in_dim(qg, r, axis=1, keepdims=False)
        k_band = lax.dynamic_slice_in_dim(kg, r_start, win_r, axis=1)
        v_band = lax.dynamic_slice_in_dim(vg, r_start, win_r, axis=1)
        dr_idx = r_start + jnp.arange(win_r) - r + NA_WIN_R - 1
        bias = jnp.take(rpb_col, dr_idx, axis=1).transpose(0, 2, 1, 3)
        bias = jnp.where(col_mask[:, None, :], bias, -jnp.inf)
        s_loc = jnp.einsum('bqhd,bjkhd->bhqjk', q_r, k_band).astype(jnp.float32) * scale + bias
        s_ctx = jnp.einsum('bqhd,bmhd->bhqm', q_r, k_ctx).astype(jnp.float32) * scale
        p = jax.nn.softmax(jnp.concatenate([s_loc.reshape(b, h, GRID_W, n_loc), s_ctx], axis=-1), axis=-1)
        p = p.astype(v.dtype)
        o = jnp.einsum('bhqjk,bjkhd->bqhd', p[..., :n_loc].reshape(b, h, GRID_W, win_r, GRID_W), v_band)
        return o + jnp.einsum('bhqm,bmhd->bqhd', p[..., n_loc:], v_ctx)

    out = lax.map(row_block, jnp.arange(rows))
    return out.transpose(1, 0, 2, 3, 4).reshape(b, s, h * dh)


def mixer_na(h_lat, h_ctx, w_qkv, rpb, w_out, ctx_out):
    b, s, _ = h_lat.shape
    l = h_ctx.shape[1]
    kv_c = (h_ctx @ w_qkv[:, NA_WIDTH:]).reshape(b, l, 2, NA_HEADS, NA_HEAD_DIM)
    k_c, v_c = kv_c[:, :, 0], kv_c[:, :, 1]
    qkv = (h_lat @ w_qkv).reshape(b, s, 3, NA_HEADS, NA_HEAD_DIM)
    o_l = neighbourhood_attention(qkv[:, :, 0], qkv[:, :, 1], qkv[:, :, 2], k_c, v_c, rpb)
    y_l = o_l @ w_out
    if ctx_out:
        q_c = (h_ctx @ w_qkv[:, :NA_WIDTH]).reshape(b, l, NA_HEADS, NA_HEAD_DIM)
        y_c = softmax_attn(q_c, k_c, v_c).reshape(b, l, NA_WIDTH) @ w_out
    else:
        y_c = None
    return y_l, y_c


def moe(h, w_r, b_r, w_gu, b_gu, w_down, b_down):
    logits = (h @ w_r + b_r).astype(jnp.float32)
    top_v, top_i = lax.top_k(logits, TOP_K)
    w = jax.nn.softmax(top_v, axis=-1)
    gates = jnp.einsum('nk,nke->ne', w, jax.nn.one_hot(top_i, N_EXPERTS, dtype=jnp.float32)).astype(h.dtype)
    out = jnp.zeros_like(h)
    for e in range(N_EXPERTS):
        gu = h @ w_gu[e] + b_gu[e]
        gt = jnp.minimum(gu[:, :D_EXPERT], SWIGLU_LIMIT)
        up = jnp.clip(gu[:, D_EXPERT:], -SWIGLU_LIMIT, SWIGLU_LIMIT)
        act = (up + 1.0) * gt * jax.nn.sigmoid(SWIGLU_ALPHA * gt)
        out = out + gates[:, e:e + 1] * (act @ w_down[e] + b_down[e])
    return out


def setup_inputs(seed: int = 0) -> dict:
    key = jax.random.key(seed)
    ks = iter(jax.random.split(key, 40))
    d = D_MODEL
    n_even = (DEPTH + 1) // 2
    n_odd = DEPTH // 2

    def nrm(shape, scale):
        return jax.random.normal(next(ks), shape, jnp.float32) * scale

    def gain(shape):
        return 1.0 + nrm(shape, 0.05)

    return {
        'x': nrm((BATCH, SEQ, d), 1.0),
        'c': nrm((BATCH, d), 1.0),
        'ctx': nrm((BATCH, CTX_LEN, d), 1.0),
        'c_ctx': nrm((d,), 1.0),
        'ada_w': nrm((DEPTH, d, 6 * d), 0.02),
        'ada_b': nrm((DEPTH, 6 * d), 0.01),
        'norm1_g': gain((DEPTH, d)),
        'norm2_g': gain((DEPTH, d)),
        'gla_conv_w_in': nrm((n_even, d, W_IN_A), d ** -0.5),
        'gla_wa_fwd': nrm((n_even, GLA_LOWRANK, GLA_QK), GLA_LOWRANK ** -0.5),
        'gla_ba_fwd': nrm((n_even, GLA_QK), 0.1),
        'gla_wa_bwd': nrm((n_even, GLA_LOWRANK, GLA_QK), GLA_LOWRANK ** -0.5),
        'gla_ba_bwd': nrm((n_even, GLA_QK), 0.1),
        'gla_norm_g': gain((n_even, GLA_V)),
        'conv_dw_w': nrm((n_even, CONV_WIDTH, CONV_CH), CONV_WIDTH ** -0.5),
        'conv_dw_b': nrm((n_even, CONV_CH), 0.01),
        'conv_ln_g': gain((n_even, CONV_CH)),
        'conv_ln_b': nrm((n_even, CONV_CH), 0.01),
        'gla_conv_w_out': nrm((n_even, MIX_A_OUT, d), MIX_A_OUT ** -0.5),
        'na_w_qkv': nrm((n_odd, d, 3 * NA_WIDTH), d ** -0.5),
        'na_rpb': nrm((n_odd, NA_HEADS, 2 * NA_WIN_R - 1, 2 * NA_WIN_C - 1), 0.1),
        'na_w_out': nrm((n_odd, NA_WIDTH, d), NA_WIDTH ** -0.5),
        'router_w': nrm((DEPTH, d, N_EXPERTS), d ** -0.5),
        'router_b': nrm((DEPTH, N_EXPERTS), 0.01),
        'expert_w_gu': nrm((DEPTH, N_EXPERTS, d, 2 * D_EXPERT), d ** -0.5),
        'expert_b_gu': nrm((DEPTH, N_EXPERTS, 2 * D_EXPERT), 0.01),
        'expert_w_down': nrm((DEPTH, N_EXPERTS, D_EXPERT, d), D_EXPERT ** -0.5),
        'expert_b_down': nrm((DEPTH, N_EXPERTS, d), 0.01),
        'final_norm_g': gain((d,)),
    }


def reference(x, c, ctx, c_ctx, ada_w, ada_b, norm1_g, norm2_g, gla_conv_w_in, gla_wa_fwd, gla_ba_fwd,
              gla_wa_bwd, gla_ba_bwd, gla_norm_g, conv_dw_w, conv_dw_b, conv_ln_g, conv_ln_b, gla_conv_w_out,
              na_w_qkv, na_rpb, na_w_out, router_w, router_b, expert_w_gu, expert_b_gu, expert_w_down,
              expert_b_down, final_norm_g):
    d = x.shape[-1]
    x_lat, x_ctx = x, ctx
    sc = jax.nn.silu(c)
    scc = jax.nn.silu(c_ctx)
    for i in range(DEPTH):
        last = i == DEPTH - 1
        j = i // 2
        mod_l = (sc @ ada_w[i] + ada_b[i])[:, None, :]
        sh1, s1, g1, sh2, s2, g2 = jnp.split(mod_l, 6, axis=-1)
        n_mod = 2 if last else 6
        mc = jnp.split(scc @ ada_w[i][:, :n_mod * d] + ada_b[i][:n_mod * d], n_mod)
        h_l = rmsnorm(x_lat, norm1_g[i]) * (1.0 + s1) + sh1
        h_c = rmsnorm(x_ctx, norm1_g[i]) * (1.0 + mc[1]) + mc[0]
        if i % 2 == 0:
            y_l, y_c = mixer_gla_conv(h_l, h_c, gla_conv_w_in[j], gla_wa_fwd[j], gla_ba_fwd[j], gla_wa_bwd[j],
                                      gla_ba_bwd[j], gla_norm_g[j], conv_dw_w[j], conv_dw_b[j], conv_ln_g[j],
                                      conv_ln_b[j], gla_conv_w_out[j], not last)
        else:
            y_l, y_c = mixer_na(h_l, h_c, na_w_qkv[j], na_rpb[j], na_w_out[j], not last)
        x_lat = x_lat + g1 * y_l
        h2_l = rmsnorm(x_lat, norm2_g[i]) * (1.0 + s2) + sh2
        if last:
            f = moe(h2_l.reshape(-1, d), router_w[i], router_b[i], expert_w_gu[i], expert_b_gu[i],
                    expert_w_down[i], expert_b_down[i])
            x_lat = x_lat + g2 * f.reshape(x_lat.shape)
        else:
            x_ctx = x_ctx + mc[2] * y_c
            h2_c = rmsnorm(x_ctx, norm2_g[i]) * (1.0 + mc[4]) + mc[3]
            n_lat = h2_l.shape[0] * h2_l.shape[1]
            tokens = jnp.concatenate([h2_l.reshape(-1, d), h2_c.reshape(-1, d)], axis=0)
            f = moe(tokens, router_w[i], router_b[i], expert_w_gu[i], expert_b_gu[i],
                    expert_w_down[i], expert_b_down[i])
            x_lat = x_lat + g2 * f[:n_lat].reshape(x_lat.shape)
            x_ctx = x_ctx + mc[5] * f[n_lat:].reshape(x_ctx.shape)
    return rmsnorm(x_lat, final_norm_g)
```

```python
import functools

import numpy as np
import jax
import jax.numpy as jnp
from jax import lax
from jax.experimental import pallas as pl
from jax.experimental.pallas import tpu as pltpu

F32 = jnp.float32
BF16 = jnp.bfloat16
HIGHEST = lax.Precision.HIGHEST

NORM_EPS = 1e-6
ROPE_BASE = 10000.0
GRID_W = 64

GLA_HEADS = 4
GLA_DK = 64
GLA_DV = 128
GLA_LOWRANK = 16
GLA_TEMP = 16.0
GLA_CHUNK = 64
GLA_QK = GLA_HEADS * GLA_DK
GLA_V = GLA_HEADS * GLA_DV
CONV_CH = 512
CONV_WIDTH = 31
NA_HEADS = 16
NA_HEAD_DIM = 64
NA_WIDTH = NA_HEADS * NA_HEAD_DIM
NA_WIN_R = 8
NA_WIN_C = 16
N_EXPERTS = 32
TOP_K = 4
D_EXPERT = 1024
SWIGLU_LIMIT = 7.0
SWIGLU_ALPHA = 1.702

OFF_Q = 0
OFF_G = OFF_Q + GLA_QK
OFF_GLU = OFF_G + GLA_V
OFF_K = OFF_GLU + 2 * CONV_CH
OFF_V = OFF_K + GLA_QK
OFF_AF = OFF_V + GLA_V
OFF_AB = OFF_AF + GLA_LOWRANK

LANE = 128
VMEM_LIMIT = 56 * 1024 * 1024

HEAD_PAD = LANE
QK_PAD = GLA_HEADS * HEAD_PAD
PK_Q, PK_K, PK_V, PK_G, PK_A, PK_GT, PK_LR = 0, 512, 1024, 1536, 2048, 2560, 3072
PK_WIDTH = PK_LR + LANE

MOD_ROWS = 40
NEG_BIG = -1e30
ROW_TILE = 256


def _cparams(*sem):
    return pltpu.CompilerParams(dimension_semantics=sem, vmem_limit_bytes=VMEM_LIMIT)


def _norm_mod(x, g, scale, shift):
    y = x * lax.rsqrt(jnp.mean(x * x, axis=-1, keepdims=True) + NORM_EPS) * g
    return y * (1.0 + scale) + shift


def _sigmoid(x):
    return 1.0 / (1.0 + jnp.exp(-x))


def _mods_kernel(c_ref, w_ref, b_ref, o_ref):
    c = c_ref[...]
    s = c * _sigmoid(c)
    o_ref[...] = jnp.dot(s, w_ref[...], precision=HIGHEST, preferred_element_type=F32) + b_ref[...]


def _mods(c_all, ada_w, ada_b):
    depth, d, n6 = ada_w.shape
    nb = 512
    out = pl.pallas_call(
        _mods_kernel,
        out_shape=jax.ShapeDtypeStruct((depth, MOD_ROWS, n6), F32),
        grid=(depth, n6 // nb),
        in_specs=[
            pl.BlockSpec((MOD_ROWS, d), lambda l, j: (0, 0)),
            pl.BlockSpec((None, d, nb), lambda l, j: (l, 0, j)),
            pl.BlockSpec((None, 1, nb), lambda l, j: (l, 0, j)),
        ],
        out_specs=pl.BlockSpec((None, MOD_ROWS, nb), lambda l, j: (l, 0, j)),
        compiler_params=_cparams("arbitrary", "arbitrary"),
        name="adaln_mods",
    )(c_all, ada_w, ada_b.reshape(depth, 1, n6))
    return out.reshape(depth, MOD_ROWS * 6, 1, d)


def _mod_spec(row_fn, chunk, d):
    return pl.BlockSpec((None, 1, d), lambda i, *_: (row_fn(i) * 6 + chunk, 0, 0))


def _nmm_kernel(x_ref, g_ref, sc_ref, sh_ref, w_ref, *rest, gla_layout, rope):
    if rope:
        cos_ref, sin_ref, o_ref = rest
    else:
        (o_ref,) = rest
    h = _norm_mod(x_ref[...], g_ref[...], sc_ref[...], sh_ref[...]).astype(BF16)
    nout = o_ref.shape[1]
    chunk = 512
    for j0 in range(0, nout, chunk):
        j1 = min(j0 + chunk, nout)
        acc = jnp.dot(h, w_ref[:, j0:j1], preferred_element_type=F32)
        if gla_layout and j0 in (PK_Q, PK_K):
            if rope:
                lane = lax.broadcasted_iota(jnp.int32, acc.shape, 1)
                first = (lane % 32) < 16
                rot = jnp.where(first, pltpu.roll(acc, QK_PAD - 16, 1), pltpu.roll(acc, 16, 1))
                acc = acc * cos_ref[...] + rot * sin_ref[...]
            if j0 == PK_Q:
                acc = acc * (GLA_DK ** -0.5)
        o_ref[:, j0:j1] = acc.astype(o_ref.dtype)


def _norm_mod_matmul(x, g, mods, row_fn, chunks, w, *, tm, gla_layout=False, rope=None, name):
    n, d = x.shape
    nout = w.shape[1]
    in_specs = [
        pl.BlockSpec((tm, d), lambda i: (i, 0)),
        pl.BlockSpec((1, d), lambda i: (0, 0)),
        _mod_spec(row_fn, chunks[0], d),
        _mod_spec(row_fn, chunks[1], d),
        pl.BlockSpec((d, nout), lambda i: (0, 0)),
    ]
    args = [x, g.reshape(1, d), mods, mods, w]
    if rope is not None:
        cos_t, sin_t = rope
        t_tiles = cos_t.shape[0] // tm
        in_specs += [pl.BlockSpec((tm, QK_PAD), lambda i: (i % t_tiles, 0))] * 2
        args += [cos_t, sin_t]
    return pl.pallas_call(
        functools.partial(_nmm_kernel, gla_layout=gla_layout, rope=rope is not None),
        out_shape=jax.ShapeDtypeStruct((n, nout), BF16),
        grid=(n // tm,),
        in_specs=in_specs,
        out_specs=pl.BlockSpec((tm, nout), lambda i: (i, 0)),
        compiler_params=_cparams("arbitrary"),
        name=name,
    )(*args)


def _mmres_kernel(*refs, n_parts):
    a_refs = refs[:n_parts]
    w_ref, x_ref, gate_ref, o_ref = refs[n_parts:]
    acc = None
    k0 = 0
    for a_ref in a_refs:
        kk = a_ref.shape[1]
        t = jnp.dot(a_ref[...], w_ref[k0:k0 + kk, :], preferred_element_type=F32)
        acc = t if acc is None else acc + t
        k0 += kk
    o_ref[...] = x_ref[...] + gate_ref[...] * acc


def _matmul_residual(parts, w, x, mods, row_fn, gate_chunk, *, tm, name):
    n, d = x.shape
    in_specs = [pl.BlockSpec((tm, a.shape[1]), lambda i: (i, 0)) for a in parts]
    in_specs += [
        pl.BlockSpec(w.shape, lambda i: (0, 0)),
        pl.BlockSpec((tm, d), lambda i: (i, 0)),
        _mod_spec(row_fn, gate_chunk, d),
    ]
    return pl.pallas_call(
        functools.partial(_mmres_kernel, n_parts=len(parts)),
        out_shape=jax.ShapeDtypeStruct((n, d), F32),
        grid=(n // tm,),
        in_specs=in_specs,
        out_specs=pl.BlockSpec((tm, d), lambda i: (i, 0)),
        compiler_params=_cparams("arbitrary"),
        name=name,
    )(*parts, w, x, mods)


def _gla_kernel(q_ref, k_ref, v_ref, g_ref, a_ref, wa_ref, ba_ref, ng_ref, s0f_ref, s0b_ref,
                y_ref, sf_ref, sb_ref, of_ref, ob_ref):
    t_len = q_ref.shape[0]
    n_chunks = t_len // GLA_CHUNK
    c = GLA_CHUNK
    sf_ref[...] = s0f_ref[...]
    sb_ref[...] = s0b_ref[...]

    row = lax.broadcasted_iota(jnp.int32, (c, c), 0)
    col = lax.broadcasted_iota(jnp.int32, (c, c), 1)
    lane = lax.broadcasted_iota(jnp.int32, (1, QK_PAD), 1)
    real_lane = ((lane % HEAD_PAD) < GLA_DK).astype(F32)

    def dir_step(ci, forward):
        r0 = pl.multiple_of(ci * c, c)
        rows = pl.ds(r0, c)
        off = 0 if forward else QK_PAD
        z = jnp.dot(a_ref[rows, :], wa_ref[:, off:off + QK_PAD], preferred_element_type=F32)
        z = z + ba_ref[:, off:off + QK_PAD]
        log_sig = jnp.minimum(z, 0.0) - jnp.log1p(jnp.exp(-jnp.abs(z)))
        la = log_sig * (1.0 / GLA_TEMP) * real_lane
        keep = (row >= col) if forward else (row <= col)
        bc = jnp.dot(keep.astype(F32), la, precision=HIGHEST, preferred_element_type=F32)
        b_last = bc[c - 1:c, :] if forward else bc[0:1, :]
        q = q_ref[rows, :].astype(F32)
        k = k_ref[rows, :].astype(F32)
        v = v_ref[rows, :]
        qt = (q * jnp.exp(bc)).astype(BF16)
        kt = (k * jnp.exp(-bc)).astype(BF16)
        kd = (k * jnp.exp(b_last - bc)).astype(BF16)
        decay = jnp.exp(b_last)
        s_ref = sf_ref if forward else sb_ref
        o_ref = of_ref if forward else ob_ref
        for h in range(GLA_HEADS):
            sl = slice(h * HEAD_PAD, (h + 1) * HEAD_PAD)
            att = lax.dot_general(qt[:, sl], kt[:, sl], (((1,), (1,)), ((), ())),
                                  preferred_element_type=F32)
            att = jnp.where(keep, att, 0.0).astype(BF16)
            st = s_ref[0, h]
            o = jnp.dot(att, v[:, sl], preferred_element_type=F32)
            o = o + lax.dot_general(qt[:, sl], st.astype(BF16), (((1,), (1,)), ((), ())),
                                    preferred_element_type=F32)
            kv_t = lax.dot_general(v[:, sl], kd[:, sl], (((0,), (0,)), ((), ())),
                                   preferred_element_type=F32)
            s_ref[0, h] = st * decay[:, sl] + kv_t
            o_ref[rows, sl] = o

    def step(i, carry):
        dir_step(i, True)
        dir_step(n_chunks - 1 - i, False)
        return carry

    lax.fori_loop(0, n_chunks, step, 0)

    blk = min(256, t_len)

    def post(i, carry):
        rows = pl.ds(pl.multiple_of(i * blk, blk), blk)
        o = of_ref[rows, :] + ob_ref[rows, :]
        g = g_ref[rows, :].astype(F32)
        gate = g * _sigmoid(g)
        for h in range(GLA_HEADS):
            sl = slice(h * GLA_DV, (h + 1) * GLA_DV)
            oh = o[:, sl]
            oh = oh * lax.rsqrt(jnp.mean(oh * oh, axis=-1, keepdims=True) + NORM_EPS)
            y_ref[rows, sl] = (oh * ng_ref[:, sl] * gate[:, sl]).astype(y_ref.dtype)
        return carry

    lax.fori_loop(0, t_len // blk, post, 0)


def _gla(p, t_len, wa, ba, norm_g, s0f, s0b):
    n = p.shape[0]
    b = n // t_len

    def col(width, start):
        return pl.BlockSpec((t_len, width), lambda i: (i, start // width))

    st_spec = pl.BlockSpec((1, GLA_HEADS, GLA_DV, HEAD_PAD), lambda i: (i, 0, 0, 0))
    st_shape = jax.ShapeDtypeStruct((b, GLA_HEADS, GLA_DV, HEAD_PAD), F32)
    return pl.pallas_call(
        _gla_kernel,
        out_shape=(jax.ShapeDtypeStruct((n, GLA_V), BF16), st_shape, st_shape),
        grid=(b,),
        in_specs=[
            col(QK_PAD, PK_Q), col(QK_PAD, PK_K), col(GLA_V, PK_V), col(GLA_V, PK_G), col(LANE, PK_LR),
            pl.BlockSpec(wa.shape, lambda i: (0, 0)),
            pl.BlockSpec(ba.shape, lambda i: (0, 0)),
            pl.BlockSpec((1, GLA_V), lambda i: (0, 0)),
            st_spec, st_spec,
        ],
        out_specs=(pl.BlockSpec((t_len, GLA_V), lambda i: (i, 0)), st_spec, st_spec),
        scratch_shapes=[pltpu.VMEM((t_len, GLA_V), F32), pltpu.VMEM((t_len, GLA_V), F32)],
        compiler_params=_cparams("arbitrary"),
        name="gla_scan",
    )(p, p, p, p, p, wa, ba, norm_g.reshape(1, GLA_V), s0f, s0b)


CONV_HALO = 16
CONV_ROWS = 64


def _conv_kernel(a_ref, gt_ref, w_ref, cb_ref, lg_ref, lb_ref, y_ref, u_ref):
    t_len = a_ref.shape[0]
    zeros = jnp.zeros((CONV_HALO, CONV_CH), F32)
    u_ref[0:CONV_HALO, :] = zeros
    u_ref[CONV_HALO + t_len:2 * CONV_HALO + t_len, :] = zeros
    blk = min(256, t_len)

    def glu(i, carry):
        r0 = pl.multiple_of(i * blk, blk)
        a = a_ref[pl.ds(r0, blk), :].astype(F32)
        gt = gt_ref[pl.ds(r0, blk), :].astype(F32)
        u_ref[pl.ds(r0 + CONV_HALO, blk), :] = a * _sigmoid(gt)
        return carry

    lax.fori_loop(0, t_len // blk, glu, 0)
    shift = CONV_HALO - CONV_WIDTH // 2

    def tile(i, carry):
        r0 = pl.multiple_of(i * CONV_ROWS, CONV_ROWS)
        parts = []
        for lb in range(CONV_CH // LANE):
            ls = slice(lb * LANE, (lb + 1) * LANE)
            win = u_ref[pl.ds(r0, CONV_ROWS + 2 * CONV_HALO), ls]
            acc = jnp.zeros((CONV_ROWS, LANE), F32)
            for j in range(CONV_WIDTH):
                acc = acc + win[j + shift:j + shift + CONV_ROWS, :] * w_ref[j:j + 1, ls]
            parts.append(acc)
        y = jnp.concatenate(parts, axis=1) + cb_ref[...]
        mu = jnp.mean(y, axis=-1, keepdims=True)
        yc = y - mu
        var = jnp.mean(yc * yc, axis=-1, keepdims=True)
        yn = yc * lax.rsqrt(var + NORM_EPS) * lg_ref[...] + lb_ref[...]
        y_ref[pl.ds(r0, CONV_ROWS), :] = (yn * _sigmoid(yn)).astype(y_ref.dtype)
        return carry

    lax.fori_loop(0, t_len // CONV_ROWS, tile, 0)


def _conv(p, t_len, conv_w, conv_b, ln_g, ln_b):
    n = p.shape[0]
    w_pad = jnp.pad(conv_w, ((0, 32 - CONV_WIDTH), (0, 0)))
    vec = pl.BlockSpec((1, CONV_CH), lambda i: (0, 0))
    return pl.pallas_call(
        _conv_kernel,
        out_shape=jax.ShapeDtypeStruct((n, CONV_CH), BF16),
        grid=(n // t_len,),
        in_specs=[
            pl.BlockSpec((t_len, CONV_CH), lambda i: (i, PK_A // CONV_CH)),
            pl.BlockSpec((t_len, CONV_CH), lambda i: (i, PK_GT // CONV_CH)),
            pl.BlockSpec((32, CONV_CH), lambda i: (0, 0)),
            vec, vec, vec,
        ],
        out_specs=pl.BlockSpec((t_len, CONV_CH), lambda i: (i, 0)),
        scratch_shapes=[pltpu.VMEM((t_len + 2 * CONV_HALO, CONV_CH), F32)],
        compiler_params=_cparams("arbitrary"),
        name="conformer_conv",
    )(p, p, w_pad, conv_b.reshape(1, -1), ln_g.reshape(1, -1), ln_b.reshape(1, -1))


NA_QROWS = 4
NA_BAND = 12


def _na_geometry(rows):
    assert rows >= NA_BAND and rows % NA_QROWS == 0
    win_r = min(NA_WIN_R, rows)
    starts, classes, sigs = [], [], []
    for rb in range(rows // NA_QROWS):
        bs = int(np.clip(rb * NA_QROWS - NA_WIN_R // 2, 0, rows - NA_BAND))
        sig = tuple((rb * NA_QROWS + i - bs,
                     int(np.clip(rb * NA_QROWS + i - win_r // 2, 0, rows - win_r)) - bs)
                    for i in range(NA_QROWS))
        if sig not in sigs:
            sigs.append(sig)
        starts.append(bs)
        classes.append(sigs.index(sig))
    return starts, classes, sigs, win_r


def _na_bias(rpb, sigs, win_r):
    w = GRID_W
    qi, qc = np.divmod(np.arange(NA_QROWS * w), w)
    kj, kc = np.divmod(np.arange(NA_BAND * w), w)
    c_start = np.clip(qc - NA_WIN_C // 2, 0, w - NA_WIN_C)
    col_ok = (kc[None, :] >= c_start[:, None]) & (kc[None, :] < c_start[:, None] + NA_WIN_C)
    dc = np.clip(kc[None, :] - qc[:, None] + NA_WIN_C - 1, 0, 2 * NA_WIN_C - 2)
    out = []
    for sig in sigs:
        q_rel = np.array([s[0] for s in sig])[qi]
        r_rel = np.array([s[1] for s in sig])[qi]
        row_ok = (kj[None, :] >= r_rel[:, None]) & (kj[None, :] < r_rel[:, None] + win_r)
        dr = np.clip(kj[None, :] - q_rel[:, None] + NA_WIN_R - 1, 0, 2 * NA_WIN_R - 2)
        bias = rpb.astype(F32)[:, dr, dc]
        out.append(jnp.where(jnp.asarray(row_ok & col_ok)[None], bias, NEG_BIG))
    return jnp.stack(out, axis=0)


def _na_kernel(q_ref, k_ref, v_ref, kc_ref, vc_ref, bias_ref, o_ref, *, starts, classes):
    w = GRID_W
    scale = NA_HEAD_DIM ** -0.5
    nq = NA_QROWS * w
    lane = lax.broadcasted_iota(jnp.int32, (nq, LANE), 1)
    kc = kc_ref[...]
    vc = vc_ref[...]
    nt = (((1,), (1,)), ((), ()))
    for rb, (bs, cls) in enumerate(zip(starts, classes)):
        q2 = q_ref[rb * nq:(rb + 1) * nq, :]
        kb = k_ref[bs * w:(bs + NA_BAND) * w, :]
        vb = v_ref[bs * w:(bs + NA_BAND) * w, :]
        acc = None
        for hh in range(2):
            mine = (lane >= hh * NA_HEAD_DIM) & (lane < (hh + 1) * NA_HEAD_DIM)
            qh = jnp.where(mine, q2, jnp.zeros_like(q2))
            s_loc = lax.dot_general(qh, kb, nt, preferred_element_type=F32) * scale + bias_ref[cls, hh]
            s_ctx = lax.dot_general(qh, kc, nt, preferred_element_type=F32) * scale
            m = jnp.maximum(jnp.max(s_loc, axis=-1, keepdims=True), jnp.max(s_ctx, axis=-1, keepdims=True))
            p_loc = jnp.exp(s_loc - m)
            p_ctx = jnp.exp(s_ctx - m)
            den = jnp.sum(p_loc, axis=-1, keepdims=True) + jnp.sum(p_ctx, axis=-1, keepdims=True)
            o = jnp.dot(p_loc.astype(BF16), vb, preferred_element_type=F32)
            o = o + jnp.dot(p_ctx.astype(BF16), vc, preferred_element_type=F32)
            o = o / den
            acc = o if acc is None else jnp.where(mine, o, acc)
        o_ref[rb * nq:(rb + 1) * nq, :] = acc.astype(o_ref.dtype)


def _neighbourhood_attention(qkv, kv_ctx, rpb, t_len, l_ctx):
    n = qkv.shape[0]
    b = n // t_len
    rows = t_len // GRID_W
    starts, classes, sigs, win_r = _na_geometry(rows)
    bias = _na_bias(rpb, sigs, win_r)
    n_pairs = NA_HEADS // 2
    nq, nk = NA_QROWS * GRID_W, NA_BAND * GRID_W
    return pl.pallas_call(
        functools.partial(_na_kernel, starts=starts, classes=classes),
        out_shape=jax.ShapeDtypeStruct((n, NA_WIDTH), BF16),
        grid=(n_pairs, b),
        in_specs=[
            pl.BlockSpec((t_len, LANE), lambda j, i: (i, j)),
            pl.BlockSpec((t_len, LANE), lambda j, i: (i, n_pairs + j)),
            pl.BlockSpec((t_len, LANE), lambda j, i: (i, 2 * n_pairs + j)),
            pl.BlockSpec((l_ctx, LANE), lambda j, i: (i, j)),
            pl.BlockSpec((l_ctx, LANE), lambda j, i: (i, n_pairs + j)),
            pl.BlockSpec((len(sigs), 2, nq, nk), lambda j, i: (0, j, 0, 0)),
        ],
        out_specs=pl.BlockSpec((t_len, LANE), lambda j, i: (i, j)),
        compiler_params=_cparams("arbitrary", "arbitrary"),
        name="neighbourhood_attention",
    )(qkv, qkv, qkv, kv_ctx, kv_ctx, bias)


INFO_ID, INFO_W, INFO_RANK = 0, TOP_K, 2 * TOP_K


def _router_kernel(x_ref, g_ref, sc_ref, sh_ref, wr_ref, br_ref, cnt0_ref, h_ref, info_ref, cnt_ref):
    tm = x_ref.shape[0]

    @pl.when(pl.program_id(0) == 0)
    def _():
        cnt_ref[...] = cnt0_ref[...]

    h = _norm_mod(x_ref[...], g_ref[...], sc_ref[...], sh_ref[...])
    h_ref[...] = h
    logits = jnp.dot(h, wr_ref[...], precision=HIGHEST, preferred_element_type=F32) + br_ref[...]
    lane = lax.broadcasted_iota(jnp.int32, (tm, LANE), 1)
    cur = logits
    vals, ids = [], []
    for _ in range(TOP_K):
        m = jnp.max(cur, axis=-1, keepdims=True)
        idx = jnp.min(jnp.where(cur == m, lane, LANE), axis=-1, keepdims=True)
        vals.append(m)
        ids.append(idx)
        cur = jnp.where(lane == idx, NEG_BIG, cur)
    ex = [jnp.exp(v - vals[0]) for v in vals]
    den = ex[0] + ex[1] + ex[2] + ex[3]
    onehot = jnp.zeros((tm, LANE), F32)
    for idx in ids:
        onehot = onehot + (lane == idx).astype(F32)
    row = lax.broadcasted_iota(jnp.int32, (tm, tm), 0)
    col = lax.broadcasted_iota(jnp.int32, (tm, tm), 1)
    before = jnp.dot((row > col).astype(BF16), onehot.astype(BF16), preferred_element_type=F32)
    running = cnt_ref[0:1, :]
    base = running + before
    info = jnp.zeros((tm, LANE), F32)
    for k in range(TOP_K):
        rank = jnp.sum(jnp.where(lane == ids[k], base, 0.0), axis=-1, keepdims=True)
        info = info + jnp.where(lane == INFO_ID + k, ids[k].astype(F32), 0.0)
        info = info + jnp.where(lane == INFO_W + k, ex[k] / den, 0.0)
        info = info + jnp.where(lane == INFO_RANK + k, rank, 0.0)
    info_ref[...] = info
    cnt_ref[...] = jnp.broadcast_to(running + jnp.sum(onehot, axis=0, keepdims=True), cnt_ref.shape)


def _router(x, g, mods, row_fn, w_r, b_r, cnt0, *, name):
    n, d = x.shape
    tm = ROW_TILE
    return pl.pallas_call(
        _router_kernel,
        out_shape=(jax.ShapeDtypeStruct((n, d), F32),
                   jax.ShapeDtypeStruct((n, LANE), F32),
                   jax.ShapeDtypeStruct((8, LANE), F32)),
        grid=(n // tm,),
        in_specs=[
            pl.BlockSpec((tm, d), lambda i: (i, 0)),
            pl.BlockSpec((1, d), lambda i: (0, 0)),
            _mod_spec(row_fn, 4, d),
            _mod_spec(row_fn, 3, d),
            pl.BlockSpec((d, LANE), lambda i: (0, 0)),
            pl.BlockSpec((1, LANE), lambda i: (0, 0)),
            pl.BlockSpec((8, LANE), lambda i: (0, 0)),
        ],
        out_specs=(pl.BlockSpec((tm, d), lambda i: (i, 0)),
                   pl.BlockSpec((tm, LANE), lambda i: (i, 0)),
                   pl.BlockSpec((8, LANE), lambda i: (0, 0))),
        compiler_params=_cparams("arbitrary"),
        name=name,
    )(x, g.reshape(1, d), mods, mods, w_r, b_r, cnt0)


def _row_copy(src_ref, src_row, dst_ref, dst_row, sem):
    return pltpu.make_async_copy(src_ref.at[pl.ds(src_row, 1)], dst_ref.at[pl.ds(dst_row, 1)], sem)


def _dispatch_kernel(pos_ref, h_ref, *rest):
    xs_ref, sem = rest[-2], rest[-1]
    tm = h_ref.shape[0]

    def issue(r, carry):
        for k in range(TOP_K):
            _row_copy(h_ref, r, xs_ref, pos_ref[0, 0, r * TOP_K + k], sem).start()
        return carry

    lax.fori_loop(0, tm, issue, 0)

    def drain(r, carry):
        for k in range(TOP_K):
            _row_copy(h_ref, r, xs_ref, pos_ref[0, 0, r * TOP_K + k], sem).wait()
        return carry

    lax.fori_loop(0, tm, drain, 0)


def _dispatch(pos, h, xs, n_slots, *, name):
    n, d = h.shape
    tm = ROW_TILE
    in_specs = [
        pl.BlockSpec((1, 1, tm * TOP_K), lambda i: (i, 0, 0), memory_space=pltpu.SMEM),
        pl.BlockSpec((tm, d), lambda i: (i, 0)),
    ]
    args = [pos.reshape(n // tm, 1, tm * TOP_K), h]
    aliases = {}
    if xs is not None:
        in_specs.append(pl.BlockSpec(memory_space=pl.ANY))
        args.append(xs)
        aliases = {2: 0}
    return pl.pallas_call(
        _dispatch_kernel,
        out_shape=jax.ShapeDtypeStruct((n_slots, d), F32),
        grid=(n // tm,),
        in_specs=in_specs,
        out_specs=pl.BlockSpec(memory_space=pl.ANY),
        scratch_shapes=[pltpu.SemaphoreType.DMA],
        input_output_aliases=aliases,
        compiler_params=_cparams("arbitrary"),
        name=name,
    )(*args)


def _padfill_kernel(lo_ref, hi_ref, xs_in, xs_ref, zero_ref, sem):
    del xs_in
    zero_ref[...] = jnp.zeros(zero_ref.shape, F32)
    for phase in range(2):
        for e in range(N_EXPERTS):
            def body(r, carry):
                cp = _row_copy(zero_ref, 0, xs_ref, r, sem)
                if phase == 0:
                    cp.start()
                else:
                    cp.wait()
                return carry

            lax.fori_loop(lo_ref[e], hi_ref[e], body, 0)


def _padfill(xs, lo, hi):
    n_slots, d = xs.shape
    return pl.pallas_call(
        _padfill_kernel,
        out_shape=jax.ShapeDtypeStruct((n_slots, d), F32),
        grid_spec=pltpu.PrefetchScalarGridSpec(
            num_scalar_prefetch=2,
            grid=(1,),
            in_specs=[pl.BlockSpec(memory_space=pl.ANY)],
            out_specs=pl.BlockSpec(memory_space=pl.ANY),
            scratch_shapes=[pltpu.VMEM((8, d), F32), pltpu.SemaphoreType.DMA],
        ),
        input_output_aliases={2: 0},
        compiler_params=_cparams("arbitrary"),
        name="moe_padfill",
    )(lo, hi, xs)


def _expert_kernel(te_ref, na_ref, xs_ref, wgu_ref, bgu_ref, wd_ref, bd_ref, y_ref):
    i = pl.program_id(0)

    @pl.when(i < na_ref[0])
    def _():
        x = xs_ref[...].astype(BF16)
        gu = jnp.dot(x, wgu_ref[...], preferred_element_type=F32) + bgu_ref[...]
        gt = jnp.minimum(gu[:, :D_EXPERT], SWIGLU_LIMIT)
        up = jnp.clip(gu[:, D_EXPERT:], -SWIGLU_LIMIT, SWIGLU_LIMIT)
        act = (up + 1.0) * gt * _sigmoid(SWIGLU_ALPHA * gt)
        y_ref[...] = jnp.dot(act.astype(BF16), wd_ref[...], preferred_element_type=F32) + bd_ref[...]

    @pl.when(i >= na_ref[0])
    def _():
        y_ref[...] = jnp.zeros(y_ref.shape, F32)


def _experts(xs, tile_expert, n_active, w_gu, b_gu, w_down, b_down):
    n_slots, d = xs.shape
    tm = ROW_TILE
    n_e, _, two_f = w_gu.shape
    return pl.pallas_call(
        _expert_kernel,
        out_shape=jax.ShapeDtypeStruct((n_slots, d), F32),
        grid_spec=pltpu.PrefetchScalarGridSpec(
            num_scalar_prefetch=2,
            grid=(n_slots // tm,),
            in_specs=[
                pl.BlockSpec((tm, d), lambda i, te, na: (jnp.minimum(i, na[0] - 1), 0)),
                pl.BlockSpec((None, d, two_f), lambda i, te, na: (te[i], 0, 0)),
                pl.BlockSpec((None, 1, two_f), lambda i, te, na: (te[i], 0, 0)),
                pl.BlockSpec((None, two_f // 2, d), lambda i, te, na: (te[i], 0, 0)),
                pl.BlockSpec((None, 1, d), lambda i, te, na: (te[i], 0, 0)),
            ],
            out_specs=pl.BlockSpec((tm, d), lambda i, te, na: (i, 0)),
        ),
        compiler_params=_cparams("arbitrary"),
        name="moe_experts",
    )(tile_expert, n_active, xs, w_gu, b_gu.reshape(n_e, 1, two_f), w_down, b_down.reshape(n_e, 1, d))


def _combine_kernel(pos_ref, x_ref, info_ref, gate_ref, *rest, final):
    if final:
        fg_ref, y_hbm, o_ref, ybuf, sem = rest
    else:
        y_hbm, o_ref, ybuf, sem = rest
    tm = x_ref.shape[0]

    def issue(r, carry):
        for k in range(TOP_K):
            _row_copy(y_hbm, pos_ref[0, 0, r * TOP_K + k], ybuf.at[k], r, sem).start()
        return carry

    lax.fori_loop(0, tm, issue, 0)

    def drain(r, carry):
        for k in range(TOP_K):
            _row_copy(y_hbm, pos_ref[0, 0, r * TOP_K + k], ybuf.at[k], r, sem).wait()
        return carry

    lax.fori_loop(0, tm, drain, 0)
    info = info_ref[...]
    f = info[:, INFO_W:INFO_W + 1] * ybuf[0]
    for k in range(1, TOP_K):
        f = f + info[:, INFO_W + k:INFO_W + k + 1] * ybuf[k]
    out = x_ref[...] + gate_ref[...] * f
    if final:
        out = out * lax.rsqrt(jnp.mean(out * out, axis=-1, keepdims=True) + NORM_EPS) * fg_ref[...]
    o_ref[...] = out


def _combine(pos, x, info, mods, row_fn, y, final_g, *, name):
    n, d = x.shape
    tm = ROW_TILE
    in_specs = [
        pl.BlockSpec((1, 1, tm * TOP_K), lambda i: (i, 0, 0), memory_space=pltpu.SMEM),
        pl.BlockSpec((tm, d), lambda i: (i, 0)),
        pl.BlockSpec((tm, LANE), lambda i: (i, 0)),
        _mod_spec(row_fn, 5, d),
    ]
    args = [pos.reshape(n // tm, 1, tm * TOP_K), x, info, mods]
    if final_g is not None:
        in_specs.append(pl.BlockSpec((1, d), lambda i: (0, 0)))
        args.append(final_g.reshape(1, d))
    in_specs.append(pl.BlockSpec(memory_space=pl.ANY))
    args.append(y)
    return pl.pallas_call(
        functools.partial(_combine_kernel, final=final_g is not None),
        out_shape=jax.ShapeDtypeStruct((n, d), F32),
        grid=(n // tm,),
        in_specs=in_specs,
        out_specs=pl.BlockSpec((tm, d), lambda i: (i, 0)),
        scratch_shapes=[pltpu.VMEM((TOP_K, tm, d), F32), pltpu.SemaphoreType.DMA],
        compiler_params=_cparams("arbitrary"),
        name=name,
    )(*args)


def _moe(streams, mods, norm_g, w_r, b_r, w_gu, b_gu, w_down, b_down, final_g):
    d = streams[0][0].shape[1]
    tm = ROW_TILE
    w_r_pad = jnp.pad(w_r, ((0, 0), (0, LANE - N_EXPERTS)))
    b_r_pad = jnp.pad(b_r, (0, LANE - N_EXPERTS), constant_values=NEG_BIG).reshape(1, LANE)
    cnt = jnp.zeros((8, LANE), F32)
    routed = []
    for s, (x, row_fn) in enumerate(streams):
        h, info, cnt = _router(x, norm_g, mods, row_fn, w_r_pad, b_r_pad, cnt, name=f"moe_router_{s}")
        routed.append((h, info))
    counts = cnt[0, :N_EXPERTS].astype(jnp.int32)
    padded = ((counts + tm - 1) // tm) * tm
    ends = jnp.cumsum(padded)
    offsets = ends - padded
    n_total = sum(x.shape[0] for x, _ in streams)
    n_slots = n_total * TOP_K + N_EXPERTS * tm
    n_tiles = n_slots // tm
    n_active = (ends[-1] // tm).astype(jnp.int32).reshape(1)
    tile_start = jnp.minimum(jnp.arange(n_tiles, dtype=jnp.int32), n_active[0] - 1) * tm
    tile_expert = jnp.searchsorted(ends, tile_start, side="right").astype(jnp.int32)
    tile_expert = jnp.minimum(tile_expert, N_EXPERTS - 1)

    xs = None
    positions = []
    for s, (h, info) in enumerate(routed):
        eid = info[:, INFO_ID:INFO_ID + TOP_K].astype(jnp.int32)
        rank = info[:, INFO_RANK:INFO_RANK + TOP_K].astype(jnp.int32)
        pos = (jnp.take(offsets, eid) + rank).astype(jnp.int32)
        positions.append(pos)
        xs = _dispatch(pos, h, xs, n_slots, name=f"moe_dispatch_{s}")
    xs = _padfill(xs, (offsets + counts).astype(jnp.int32), ends.astype(jnp.int32))
    y = _experts(xs, tile_expert, n_active, w_gu.astype(BF16), b_gu, w_down.astype(BF16), b_down)
    outs = []
    for s, ((x, row_fn), (h, info), pos) in enumerate(zip(streams, routed, positions)):
        outs.append(_combine(pos, x, info, mods, row_fn, y, final_g, name=f"moe_combine_{s}"))
    return outs


def _pad_heads(w):
    lead = w.shape[:-1]
    w4 = w.reshape(*lead, GLA_HEADS, GLA_DK)
    w4 = jnp.pad(w4, [(0, 0)] * len(lead) + [(0, 0), (0, HEAD_PAD - GLA_DK)])
    return w4.reshape(*lead, QK_PAD)


def _pack_gla_in(w_in):
    d = w_in.shape[0]
    lr = jnp.pad(w_in[:, OFF_AF:OFF_AB + GLA_LOWRANK], ((0, 0), (0, LANE - 2 * GLA_LOWRANK)))
    return jnp.concatenate([
        _pad_heads(w_in[:, OFF_Q:OFF_Q + GLA_QK]),
        _pad_heads(w_in[:, OFF_K:OFF_K + GLA_QK]),
        w_in[:, OFF_V:OFF_V + GLA_V],
        w_in[:, OFF_G:OFF_G + GLA_V],
        w_in[:, OFF_GLU:OFF_GLU + 2 * CONV_CH],
        lr,
    ], axis=1).astype(BF16)


def _pack_decay(wa_f, ba_f, wa_b, ba_b):
    wa = jnp.zeros((LANE, 2 * QK_PAD), F32)
    wa = wa.at[0:GLA_LOWRANK, 0:QK_PAD].set(_pad_heads(wa_f))
    wa = wa.at[GLA_LOWRANK:2 * GLA_LOWRANK, QK_PAD:].set(_pad_heads(wa_b))
    ba = jnp.concatenate([_pad_heads(ba_f), _pad_heads(ba_b)]).reshape(1, 2 * QK_PAD)
    return wa.astype(BF16), ba


def _rope_tables(t_len):
    t = jnp.arange(t_len)
    row_pos = (t // GRID_W).astype(F32)
    col_pos = (t % GRID_W).astype(F32)
    half = GLA_DK // 4
    inv_freq = ROPE_BASE ** (-jnp.arange(half, dtype=F32) / half)
    dim = np.arange(HEAD_PAD)
    real = dim < GLA_DK
    use_col = (dim % GLA_DK) >= GLA_DK // 2
    first = (dim % (GLA_DK // 2)) < half
    pos = jnp.where(jnp.asarray(use_col)[None, :], col_pos[:, None], row_pos[:, None])
    ang = pos * inv_freq[dim % half][None, :]
    cos = jnp.where(jnp.asarray(real)[None, :], jnp.cos(ang), 0.0)
    sin = jnp.where(jnp.asarray(real)[None, :], jnp.sin(ang), 0.0)
    sin = jnp.where(jnp.asarray(first)[None, :], -sin, sin)
    return jnp.tile(cos, (1, GLA_HEADS)), jnp.tile(sin, (1, GLA_HEADS))


def kernel(x, c, ctx, c_ctx, ada_w, ada_b, norm1_g, norm2_g, gla_conv_w_in, gla_wa_fwd, gla_ba_fwd,
           gla_wa_bwd, gla_ba_bwd, gla_norm_g, conv_dw_w, conv_dw_b, conv_ln_g, conv_ln_b,
           gla_conv_w_out, na_w_qkv, na_rpb, na_w_out, router_w, router_b, expert_w_gu, expert_b_gu,
           expert_w_down, expert_b_down, final_norm_g):
    b, t_len, d = x.shape
    l_ctx = ctx.shape[1]
    depth = ada_w.shape[0]
    assert b + 1 <= MOD_ROWS and t_len % 512 == 0 and l_ctx % ROW_TILE == 0

    c_all = jnp.concatenate([c, c_ctx[None, :], jnp.zeros((MOD_ROWS - b - 1, d), F32)], axis=0)
    mods_all = _mods(c_all, ada_w, ada_b)

    x_lat = x.reshape(b * t_len, d)
    x_ctx = ctx.reshape(b * l_ctx, d)
    tm_lat = 512
    tm_ctx = min(512, l_ctx)

    def lat_row(i, tm=tm_lat):
        return (i * tm) // t_len

    def ctx_row(i):
        return b

    def lat_row_moe(i):
        return (i * ROW_TILE) // t_len

    for layer in range(depth):
        last = layer == depth - 1
        j = layer // 2
        mods = mods_all[layer]
        if layer % 2 == 0:
            w_pack = _pack_gla_in(gla_conv_w_in[j])
            wa, ba = _pack_decay(gla_wa_fwd[j], gla_ba_fwd[j], gla_wa_bwd[j], gla_ba_bwd[j])
            w_out = gla_conv_w_out[j].astype(BF16)
            zero_state = jnp.zeros((b, GLA_HEADS, GLA_DV, HEAD_PAD), F32)
            p_c = _norm_mod_matmul(x_ctx, norm1_g[layer], mods, ctx_row, (1, 0), w_pack, tm=tm_ctx,
                                   gla_layout=True, name="gla_in_ctx")
            y_gla_c, s_f, s_b = _gla(p_c, l_ctx, wa, ba, gla_norm_g[j], zero_state, zero_state)
            p_l = _norm_mod_matmul(x_lat, norm1_g[layer], mods, lat_row, (1, 0), w_pack, tm=tm_lat,
                                   gla_layout=True, rope=_rope_tables(t_len), name="gla_in_lat")
            y_gla_l, _, _ = _gla(p_l, t_len, wa, ba, gla_norm_g[j], s_f, s_b)
            y_conv_l = _conv(p_l, t_len, conv_dw_w[j], conv_dw_b[j], conv_ln_g[j], conv_ln_b[j])
            x_lat = _matmul_residual([y_gla_l, y_conv_l], w_out, x_lat, mods, lat_row, 2, tm=tm_lat,
                                     name="mix_out_lat")
            if not last:
                y_conv_c = _conv(p_c, l_ctx, conv_dw_w[j], conv_dw_b[j], conv_ln_g[j], conv_ln_b[j])
                x_ctx = _matmul_residual([y_gla_c, y_conv_c], w_out, x_ctx, mods, ctx_row, 2, tm=tm_ctx,
                                         name="mix_out_ctx")
        else:
            w_qkv = na_w_qkv[j].astype(BF16)
            kv_c = _norm_mod_matmul(x_ctx, norm1_g[layer], mods, ctx_row, (1, 0), w_qkv[:, NA_WIDTH:],
                                    tm=tm_ctx, name="na_kv_ctx")
            qkv = _norm_mod_matmul(x_lat, norm1_g[layer], mods, lat_row, (1, 0), w_qkv, tm=tm_lat,
                                   name="na_qkv_lat")
            o_l = _neighbourhood_attention(qkv, kv_c, na_rpb[j], t_len, l_ctx)
            x_lat = _matmul_residual([o_l], na_w_out[j].astype(BF16), x_lat, mods, lat_row, 2, tm=tm_lat,
                                     name="na_out_lat")
            if not last:
                raise NotImplementedError("context output of an attention layer is only needed mid-stack")
        streams = [(x_lat, lat_row_moe)]
        if not last:
            streams.append((x_ctx, ctx_row))
        outs = _moe(streams, mods, norm2_g[layer], router_w[layer], router_b[layer], expert_w_gu[layer],
                    expert_b_gu[layer], expert_w_down[layer], expert_b_down[layer],
                    final_norm_g if last else None)
        x_lat = outs[0]
        if not last:
            x_ctx = outs[1]
    return x_lat.reshape(b, t_len, d)
```

```python
import functools

import numpy as np
import jax
import jax.numpy as jnp
from jax import lax
from jax.experimental import pallas as pl
from jax.experimental.pallas import tpu as pltpu

F32 = jnp.float32
BF16 = jnp.bfloat16
HIGHEST = lax.Precision.HIGHEST

NORM_EPS = 1e-6
ROPE_BASE = 10000.0
GRID_W = 64

GLA_HEADS = 4
GLA_DK = 64
GLA_DV = 128
GLA_LOWRANK = 16
GLA_TEMP = 16.0
GLA_CHUNK = 64
GLA_QK = GLA_HEADS * GLA_DK
GLA_V = GLA_HEADS * GLA_DV
CONV_CH = 512
CONV_WIDTH = 31
NA_HEADS = 16
NA_HEAD_DIM = 64
NA_WIDTH = NA_HEADS * NA_HEAD_DIM
NA_WIN_R = 8
NA_WIN_C = 16
N_EXPERTS = 32
TOP_K = 4
D_EXPERT = 1024
SWIGLU_LIMIT = 7.0
SWIGLU_ALPHA = 1.702

OFF_Q = 0
OFF_G = OFF_Q + GLA_QK
OFF_GLU = OFF_G + GLA_V
OFF_K = OFF_GLU + 2 * CONV_CH
OFF_V = OFF_K + GLA_QK
OFF_AF = OFF_V + GLA_V
OFF_AB = OFF_AF + GLA_LOWRANK

LANE = 128
VMEM_LIMIT = 56 * 1024 * 1024

HEAD_PAD = LANE
QK_PAD = GLA_HEADS * HEAD_PAD
PK_Q, PK_K, PK_V, PK_G, PK_A, PK_GT, PK_LR = 0, 512, 1024, 1536, 2048, 2560, 3072
PK_WIDTH = PK_LR + LANE

MOD_ROWS = 40
NEG_BIG = -1e30
ROW_TILE = 256
EXPERT_TILE = 512
ISSUE_UNROLL = 8


def _cparams(*sem):
    return pltpu.CompilerParams(dimension_semantics=sem, vmem_limit_bytes=VMEM_LIMIT)


def _norm_mod(x, g, scale, shift):
    y = x * lax.rsqrt(jnp.mean(x * x, axis=-1, keepdims=True) + NORM_EPS) * g
    return y * (1.0 + scale) + shift


def _sigmoid(x):
    return 1.0 / (1.0 + jnp.exp(-x))


def _mods_kernel(c_ref, w_ref, b_ref, o_ref):
    c = c_ref[...]
    s = c * _sigmoid(c)
    o_ref[...] = jnp.dot(s, w_ref[...], precision=HIGHEST, preferred_element_type=F32) + b_ref[...]


def _mods(c_all, ada_w, ada_b):
    depth, d, n6 = ada_w.shape
    nb = 512
    out = pl.pallas_call(
        _mods_kernel,
        out_shape=jax.ShapeDtypeStruct((depth, MOD_ROWS, n6), F32),
        grid=(depth, n6 // nb),
        in_specs=[
            pl.BlockSpec((MOD_ROWS, d), lambda l, j: (0, 0)),
            pl.BlockSpec((None, d, nb), lambda l, j: (l, 0, j)),
            pl.BlockSpec((None, 1, nb), lambda l, j: (l, 0, j)),
        ],
        out_specs=pl.BlockSpec((None, MOD_ROWS, nb), lambda l, j: (l, 0, j)),
        compiler_params=_cparams("arbitrary", "arbitrary"),
        name="adaln_mods",
    )(c_all, ada_w, ada_b.reshape(depth, 1, n6))
    return out.reshape(depth, MOD_ROWS * 6, 1, d)


def _mod_spec(row_fn, chunk, d):
    return pl.BlockSpec((None, 1, d), lambda i, *_: (row_fn(i) * 6 + chunk, 0, 0))


def _nmm_kernel(x_ref, g_ref, sc_ref, sh_ref, w_ref, *rest, gla_layout, rope):
    if rope:
        cos_ref, sin_ref, o_ref = rest
    else:
        (o_ref,) = rest
    h = _norm_mod(x_ref[...], g_ref[...], sc_ref[...], sh_ref[...]).astype(BF16)
    nout = o_ref.shape[1]
    chunk = 512
    for j0 in range(0, nout, chunk):
        j1 = min(j0 + chunk, nout)
        acc = jnp.dot(h, w_ref[:, j0:j1], preferred_element_type=F32)
        if gla_layout and j0 in (PK_Q, PK_K):
            if rope:
                lane = lax.broadcasted_iota(jnp.int32, acc.shape, 1)
                first = (lane % 32) < 16
                rot = jnp.where(first, pltpu.roll(acc, QK_PAD - 16, 1), pltpu.roll(acc, 16, 1))
                acc = acc * cos_ref[...] + rot * sin_ref[...]
            if j0 == PK_Q:
                acc = acc * (GLA_DK ** -0.5)
        o_ref[:, j0:j1] = acc.astype(o_ref.dtype)


def _norm_mod_matmul(x, g, mods, row_fn, chunks, w, *, tm, gla_layout=False, rope=None, name):
    n, d = x.shape
    nout = w.shape[1]
    in_specs = [
        pl.BlockSpec((tm, d), lambda i: (i, 0)),
        pl.BlockSpec((1, d), lambda i: (0, 0)),
        _mod_spec(row_fn, chunks[0], d),
        _mod_spec(row_fn, chunks[1], d),
        pl.BlockSpec((d, nout), lambda i: (0, 0)),
    ]
    args = [x, g.reshape(1, d), mods, mods, w]
    if rope is not None:
        cos_t, sin_t = rope
        t_tiles = cos_t.shape[0] // tm
        in_specs += [pl.BlockSpec((tm, QK_PAD), lambda i: (i % t_tiles, 0))] * 2
        args += [cos_t, sin_t]
    return pl.pallas_call(
        functools.partial(_nmm_kernel, gla_layout=gla_layout, rope=rope is not None),
        out_shape=jax.ShapeDtypeStruct((n, nout), BF16),
        grid=(n // tm,),
        in_specs=in_specs,
        out_specs=pl.BlockSpec((tm, nout), lambda i: (i, 0)),
        compiler_params=_cparams("arbitrary"),
        name=name,
    )(*args)


def _mmres_kernel(*refs, n_parts):
    a_refs = refs[:n_parts]
    w_ref, x_ref, gate_ref, o_ref = refs[n_parts:]
    acc = None
    k0 = 0
    for a_ref in a_refs:
        kk = a_ref.shape[1]
        t = jnp.dot(a_ref[...], w_ref[k0:k0 + kk, :], preferred_element_type=F32)
        acc = t if acc is None else acc + t
        k0 += kk
    o_ref[...] = x_ref[...] + gate_ref[...] * acc


def _matmul_residual(parts, w, x, mods, row_fn, gate_chunk, *, tm, name):
    n, d = x.shape
    in_specs = [pl.BlockSpec((tm, a.shape[1]), lambda i: (i, 0)) for a in parts]
    in_specs += [
        pl.BlockSpec(w.shape, lambda i: (0, 0)),
        pl.BlockSpec((tm, d), lambda i: (i, 0)),
        _mod_spec(row_fn, gate_chunk, d),
    ]
    return pl.pallas_call(
        functools.partial(_mmres_kernel, n_parts=len(parts)),
        out_shape=jax.ShapeDtypeStruct((n, d), F32),
        grid=(n // tm,),
        in_specs=in_specs,
        out_specs=pl.BlockSpec((tm, d), lambda i: (i, 0)),
        compiler_params=_cparams("arbitrary"),
        name=name,
    )(*parts, w, x, mods)


def _gla_kernel(q_ref, k_ref, v_ref, g_ref, a_ref, wa_ref, ba_ref, ng_ref, s0f_ref, s0b_ref,
                y_ref, sf_ref, sb_ref, of_ref, ob_ref):
    t_len = q_ref.shape[0]
    n_chunks = t_len // GLA_CHUNK
    c = GLA_CHUNK
    sf_ref[...] = s0f_ref[...]
    sb_ref[...] = s0b_ref[...]

    row = lax.broadcasted_iota(jnp.int32, (c, c), 0)
    col = lax.broadcasted_iota(jnp.int32, (c, c), 1)
    lane = lax.broadcasted_iota(jnp.int32, (1, QK_PAD), 1)
    real_lane = ((lane % HEAD_PAD) < GLA_DK).astype(F32)

    def dir_step(ci, forward):
        r0 = pl.multiple_of(ci * c, c)
        rows = pl.ds(r0, c)
        off = 0 if forward else QK_PAD
        z = jnp.dot(a_ref[rows, :], wa_ref[:, off:off + QK_PAD], preferred_element_type=F32)
        z = z + ba_ref[:, off:off + QK_PAD]
        log_sig = jnp.minimum(z, 0.0) - jnp.log1p(jnp.exp(-jnp.abs(z)))
        la = log_sig * (1.0 / GLA_TEMP) * real_lane
        keep = (row >= col) if forward else (row <= col)
        bc = jnp.dot(keep.astype(F32), la, precision=HIGHEST, preferred_element_type=F32)
        b_last = bc[c - 1:c, :] if forward else bc[0:1, :]
        q = q_ref[rows, :].astype(F32)
        k = k_ref[rows, :].astype(F32)
        v = v_ref[rows, :]
        qt = (q * jnp.exp(bc)).astype(BF16)
        kt = (k * jnp.exp(-bc)).astype(BF16)
        kd = (k * jnp.exp(b_last - bc)).astype(BF16)
        decay = jnp.exp(b_last)
        s_ref = sf_ref if forward else sb_ref
        o_ref = of_ref if forward else ob_ref
        for h in range(GLA_HEADS):
            sl = slice(h * HEAD_PAD, (h + 1) * HEAD_PAD)
            att = lax.dot_general(qt[:, sl], kt[:, sl], (((1,), (1,)), ((), ())),
                                  preferred_element_type=F32)
            att = jnp.where(keep, att, 0.0).astype(BF16)
            st = s_ref[0, h]
            o = jnp.dot(att, v[:, sl], preferred_element_type=F32)
            o = o + lax.dot_general(qt[:, sl], st.astype(BF16), (((1,), (1,)), ((), ())),
                                    preferred_element_type=F32)
            kv_t = lax.dot_general(v[:, sl], kd[:, sl], (((0,), (0,)), ((), ())),
                                   preferred_element_type=F32)
            s_ref[0, h] = st * decay[:, sl] + kv_t
            o_ref[rows, sl] = o

    def step(i, carry):
        dir_step(i, True)
        dir_step(n_chunks - 1 - i, False)
        return carry

    lax.fori_loop(0, n_chunks, step, 0)

    blk = min(256, t_len)

    def post(i, carry):
        rows = pl.ds(pl.multiple_of(i * blk, blk), blk)
        o = of_ref[rows, :] + ob_ref[rows, :]
        g = g_ref[rows, :].astype(F32)
        gate = g * _sigmoid(g)
        for h in range(GLA_HEADS):
            sl = slice(h * GLA_DV, (h + 1) * GLA_DV)
            oh = o[:, sl]
            oh = oh * lax.rsqrt(jnp.mean(oh * oh, axis=-1, keepdims=True) + NORM_EPS)
            y_ref[rows, sl] = (oh * ng_ref[:, sl] * gate[:, sl]).astype(y_ref.dtype)
        return carry

    lax.fori_loop(0, t_len // blk, post, 0)


def _gla(p, t_len, wa, ba, norm_g, s0f, s0b):
    n = p.shape[0]
    b = n // t_len

    def col(width, start):
        return pl.BlockSpec((t_len, width), lambda i: (i, start // width))

    st_spec = pl.BlockSpec((1, GLA_HEADS, GLA_DV, HEAD_PAD), lambda i: (i, 0, 0, 0))
    st_shape = jax.ShapeDtypeStruct((b, GLA_HEADS, GLA_DV, HEAD_PAD), F32)
    return pl.pallas_call(
        _gla_kernel,
        out_shape=(jax.ShapeDtypeStruct((n, GLA_V), BF16), st_shape, st_shape),
        grid=(b,),
        in_specs=[
            col(QK_PAD, PK_Q), col(QK_PAD, PK_K), col(GLA_V, PK_V), col(GLA_V, PK_G), col(LANE, PK_LR),
            pl.BlockSpec(wa.shape, lambda i: (0, 0)),
            pl.BlockSpec(ba.shape, lambda i: (0, 0)),
            pl.BlockSpec((1, GLA_V), lambda i: (0, 0)),
            st_spec, st_spec,
        ],
        out_specs=(pl.BlockSpec((t_len, GLA_V), lambda i: (i, 0)), st_spec, st_spec),
        scratch_shapes=[pltpu.VMEM((t_len, GLA_V), F32), pltpu.VMEM((t_len, GLA_V), F32)],
        compiler_params=_cparams("arbitrary"),
        name="gla_scan",
    )(p, p, p, p, p, wa, ba, norm_g.reshape(1, GLA_V), s0f, s0b)


CONV_HALO = 16
CONV_ROWS = 64


def _conv_kernel(a_ref, gt_ref, w_ref, cb_ref, lg_ref, lb_ref, y_ref, u_ref):
    t_len = a_ref.shape[0]
    zeros = jnp.zeros((CONV_HALO, CONV_CH), F32)
    u_ref[0:CONV_HALO, :] = zeros
    u_ref[CONV_HALO + t_len:2 * CONV_HALO + t_len, :] = zeros
    blk = min(256, t_len)

    def glu(i, carry):
        r0 = pl.multiple_of(i * blk, blk)
        a = a_ref[pl.ds(r0, blk), :].astype(F32)
        gt = gt_ref[pl.ds(r0, blk), :].astype(F32)
        u_ref[pl.ds(r0 + CONV_HALO, blk), :] = a * _sigmoid(gt)
        return carry

    lax.fori_loop(0, t_len // blk, glu, 0)
    shift = CONV_HALO - CONV_WIDTH // 2
    win_rows = CONV_ROWS + 2 * CONV_HALO

    def tile(i, carry):
        r0 = pl.multiple_of(i * CONV_ROWS, CONV_ROWS)
        parts = []
        for lb in range(CONV_CH // LANE):
            ls = slice(lb * LANE, (lb + 1) * LANE)
            win = u_ref[pl.ds(r0, win_rows), ls]
            acc = jnp.zeros((CONV_ROWS, LANE), F32)
            for b in range(8):
                wb = win if b == 0 else pltpu.roll(win, win_rows - b, 0)
                for a in range(win_rows // 8):
                    j = 8 * a + b - shift
                    if 0 <= j < CONV_WIDTH:
                        acc = acc + wb[8 * a:8 * a + CONV_ROWS, :] * w_ref[j:j + 1, ls]
            parts.append(acc)
        y = jnp.concatenate(parts, axis=1) + cb_ref[...]
        mu = jnp.mean(y, axis=-1, keepdims=True)
        yc = y - mu
        var = jnp.mean(yc * yc, axis=-1, keepdims=True)
        yn = yc * lax.rsqrt(var + NORM_EPS) * lg_ref[...] + lb_ref[...]
        y_ref[pl.ds(r0, CONV_ROWS), :] = (yn * _sigmoid(yn)).astype(y_ref.dtype)
        return carry

    lax.fori_loop(0, t_len // CONV_ROWS, tile, 0)


def _conv(p, t_len, conv_w, conv_b, ln_g, ln_b):
    n = p.shape[0]
    w_pad = jnp.pad(conv_w, ((0, 32 - CONV_WIDTH), (0, 0)))
    vec = pl.BlockSpec((1, CONV_CH), lambda i: (0, 0))
    return pl.pallas_call(
        _conv_kernel,
        out_shape=jax.ShapeDtypeStruct((n, CONV_CH), BF16),
        grid=(n // t_len,),
        in_specs=[
            pl.BlockSpec((t_len, CONV_CH), lambda i: (i, PK_A // CONV_CH)),
            pl.BlockSpec((t_len, CONV_CH), lambda i: (i, PK_GT // CONV_CH)),
            pl.BlockSpec((32, CONV_CH), lambda i: (0, 0)),
            vec, vec, vec,
        ],
        out_specs=pl.BlockSpec((t_len, CONV_CH), lambda i: (i, 0)),
        scratch_shapes=[pltpu.VMEM((t_len + 2 * CONV_HALO, CONV_CH), F32)],
        compiler_params=_cparams("arbitrary"),
        name="conformer_conv",
    )(p, p, w_pad, conv_b.reshape(1, -1), ln_g.reshape(1, -1), ln_b.reshape(1, -1))


NA_QROWS = 4
NA_BAND = 12


def _na_geometry(rows):
    assert rows >= NA_BAND and rows % NA_QROWS == 0
    win_r = min(NA_WIN_R, rows)
    starts, classes, sigs = [], [], []
    for rb in range(rows // NA_QROWS):
        bs = int(np.clip(rb * NA_QROWS - NA_WIN_R // 2, 0, rows - NA_BAND))
        sig = tuple((rb * NA_QROWS + i - bs,
                     int(np.clip(rb * NA_QROWS + i - win_r // 2, 0, rows - win_r)) - bs)
                    for i in range(NA_QROWS))
        if sig not in sigs:
            sigs.append(sig)
        starts.append(bs)
        classes.append(sigs.index(sig))
    return starts, classes, sigs, win_r


def _na_bias(rpb, sigs, win_r):
    w = GRID_W
    cols = np.arange(w)
    c_start = np.clip(cols - NA_WIN_C // 2, 0, w - NA_WIN_C)
    col_ok = (cols[None, :] >= c_start[:, None]) & (cols[None, :] < c_start[:, None] + NA_WIN_C)
    dc = np.clip(cols[None, :] - cols[:, None] + NA_WIN_C - 1, 0, 2 * NA_WIN_C - 2)
    rpb_col = jnp.take(rpb.astype(F32), jnp.asarray(dc.reshape(-1)), axis=2)
    rpb_col = rpb_col.reshape(NA_HEADS, 2 * NA_WIN_R - 1, w, w)
    rpb_col = jnp.where(jnp.asarray(col_ok)[None, None], rpb_col, NEG_BIG)
    kj = np.arange(NA_BAND)
    out = []
    for sig in sigs:
        q_rel = np.array([s[0] for s in sig])
        r_rel = np.array([s[1] for s in sig])
        row_ok = (kj[None, :] >= r_rel[:, None]) & (kj[None, :] < r_rel[:, None] + win_r)
        dr = np.clip(kj[None, :] - q_rel[:, None] + NA_WIN_R - 1, 0, 2 * NA_WIN_R - 2)
        bias = jnp.take(rpb_col, jnp.asarray(dr.reshape(-1)), axis=1)
        bias = bias.reshape(NA_HEADS, NA_QROWS, NA_BAND, w, w)
        bias = jnp.where(jnp.asarray(row_ok)[None, :, :, None, None], bias, NEG_BIG)
        out.append(bias.transpose(0, 1, 3, 2, 4).reshape(NA_HEADS, NA_QROWS * w, NA_BAND * w))
    return jnp.stack(out, axis=0)


def _na_kernel(q_ref, k_ref, v_ref, kc_ref, vc_ref, bias_ref, o_ref, *, starts, classes):
    w = GRID_W
    scale = NA_HEAD_DIM ** -0.5
    nq = NA_QROWS * w
    lane = lax.broadcasted_iota(jnp.int32, (nq, LANE), 1)
    kc = kc_ref[...]
    vc = vc_ref[...]
    nt = (((1,), (1,)), ((), ()))
    for rb, (bs, cls) in enumerate(zip(starts, classes)):
        q2 = q_ref[rb * nq:(rb + 1) * nq, :]
        kb = k_ref[bs * w:(bs + NA_BAND) * w, :]
        vb = v_ref[bs * w:(bs + NA_BAND) * w, :]
        acc = None
        for hh in range(2):
            mine = (lane >= hh * NA_HEAD_DIM) & (lane < (hh + 1) * NA_HEAD_DIM)
            qh = jnp.where(mine, q2, jnp.zeros_like(q2)) * scale
            s_loc = lax.dot_general(qh, kb, nt, preferred_element_type=F32) + bias_ref[cls, hh]
            s_ctx = lax.dot_general(qh, kc, nt, preferred_element_type=F32)
            m = jnp.maximum(jnp.max(s_loc, axis=-1, keepdims=True), jnp.max(s_ctx, axis=-1, keepdims=True))
            p_loc = jnp.exp(s_loc - m)
            p_ctx = jnp.exp(s_ctx - m)
            den = jnp.sum(p_loc, axis=-1, keepdims=True) + jnp.sum(p_ctx, axis=-1, keepdims=True)
            o = jnp.dot(p_loc.astype(BF16), vb, preferred_element_type=F32)
            o = o + jnp.dot(p_ctx.astype(BF16), vc, preferred_element_type=F32)
            o = o / den
            acc = o if acc is None else jnp.where(mine, o, acc)
        o_ref[rb * nq:(rb + 1) * nq, :] = acc.astype(o_ref.dtype)


def _neighbourhood_attention(qkv, kv_ctx, rpb, t_len, l_ctx):
    n = qkv.shape[0]
    b = n // t_len
    rows = t_len // GRID_W
    starts, classes, sigs, win_r = _na_geometry(rows)
    bias = _na_bias(rpb, sigs, win_r)
    n_pairs = NA_HEADS // 2
    nq, nk = NA_QROWS * GRID_W, NA_BAND * GRID_W
    return pl.pallas_call(
        functools.partial(_na_kernel, starts=starts, classes=classes),
        out_shape=jax.ShapeDtypeStruct((n, NA_WIDTH), BF16),
        grid=(n_pairs, b),
        in_specs=[
            pl.BlockSpec((t_len, LANE), lambda j, i: (i, j)),
            pl.BlockSpec((t_len, LANE), lambda j, i: (i, n_pairs + j)),
            pl.BlockSpec((t_len, LANE), lambda j, i: (i, 2 * n_pairs + j)),
            pl.BlockSpec((l_ctx, LANE), lambda j, i: (i, j)),
            pl.BlockSpec((l_ctx, LANE), lambda j, i: (i, n_pairs + j)),
            pl.BlockSpec((len(sigs), 2, nq, nk), lambda j, i: (0, j, 0, 0)),
        ],
        out_specs=pl.BlockSpec((t_len, LANE), lambda j, i: (i, j)),
        compiler_params=_cparams("arbitrary", "arbitrary"),
        name="neighbourhood_attention",
    )(qkv, qkv, qkv, kv_ctx, kv_ctx, bias)


INFO_ID, INFO_W, INFO_RANK = 0, TOP_K, 2 * TOP_K
INFO_ROWS = 16


def _router_kernel(x_ref, g_ref, sc_ref, sh_ref, wrt_ref, br_ref, tri_ref, cnt0_ref, h_ref, info_ref, cnt_ref):
    tm = x_ref.shape[0]

    @pl.when(pl.program_id(0) == 0)
    def _():
        cnt_ref[...] = cnt0_ref[...]

    h = _norm_mod(x_ref[...], g_ref[...], sc_ref[...], sh_ref[...])
    h_ref[...] = h
    logits = lax.dot_general(wrt_ref[...], h, (((1,), (1,)), ((), ())), precision=HIGHEST,
                             preferred_element_type=F32) + br_ref[...]
    sub = lax.broadcasted_iota(jnp.int32, (N_EXPERTS, tm), 0)
    cur = logits
    vals, ids = [], []
    for _ in range(TOP_K):
        m = jnp.max(cur, axis=0, keepdims=True)
        idx = jnp.min(jnp.where(cur == m, sub, N_EXPERTS), axis=0, keepdims=True)
        vals.append(m)
        ids.append(idx)
        cur = jnp.where(sub == idx, NEG_BIG, cur)
    ex = [jnp.exp(v - vals[0]) for v in vals]
    den = ex[0] + ex[1] + ex[2] + ex[3]
    onehot = jnp.zeros((N_EXPERTS, tm), F32)
    for idx in ids:
        onehot = onehot + (sub == idx).astype(F32)
    before = jnp.dot(onehot.astype(BF16), tri_ref[...], preferred_element_type=F32)
    running = cnt_ref[:, 0:1]
    base = running + before
    row = lax.broadcasted_iota(jnp.int32, (INFO_ROWS, tm), 0)
    info = jnp.zeros((INFO_ROWS, tm), F32)
    for k in range(TOP_K):
        rank = jnp.sum(jnp.where(sub == ids[k], base, 0.0), axis=0, keepdims=True)
        info = info + jnp.where(row == INFO_ID + k, ids[k].astype(F32), 0.0)
        info = info + jnp.where(row == INFO_W + k, ex[k] / den, 0.0)
        info = info + jnp.where(row == INFO_RANK + k, rank, 0.0)
    info_ref[...] = info
    cnt_ref[...] = jnp.broadcast_to(running + jnp.sum(onehot, axis=1, keepdims=True), cnt_ref.shape)


def _router(x, g, mods, row_fn, w_rt, b_r, tri, cnt0, *, name):
    n, d = x.shape
    tm = ROW_TILE
    return pl.pallas_call(
        _router_kernel,
        out_shape=(jax.ShapeDtypeStruct((n, d), F32),
                   jax.ShapeDtypeStruct((INFO_ROWS, n), F32),
                   jax.ShapeDtypeStruct((N_EXPERTS, LANE), F32)),
        grid=(n // tm,),
        in_specs=[
            pl.BlockSpec((tm, d), lambda i: (i, 0)),
            pl.BlockSpec((1, d), lambda i: (0, 0)),
            _mod_spec(row_fn, 4, d),
            _mod_spec(row_fn, 3, d),
            pl.BlockSpec((N_EXPERTS, d), lambda i: (0, 0)),
            pl.BlockSpec((N_EXPERTS, 1), lambda i: (0, 0)),
            pl.BlockSpec((tm, tm), lambda i: (0, 0)),
            pl.BlockSpec((N_EXPERTS, LANE), lambda i: (0, 0)),
        ],
        out_specs=(pl.BlockSpec((tm, d), lambda i: (i, 0)),
                   pl.BlockSpec((INFO_ROWS, tm), lambda i: (0, i)),
                   pl.BlockSpec((N_EXPERTS, LANE), lambda i: (0, 0))),
        compiler_params=_cparams("arbitrary"),
        name=name,
    )(x, g.reshape(1, d), mods, mods, w_rt, b_r, tri, cnt0)


def _row_copy(src_ref, src_row, dst_ref, dst_row, sem):
    return pltpu.make_async_copy(src_ref.at[pl.ds(src_row, 1)], dst_ref.at[pl.ds(dst_row, 1)], sem)


def _dispatch_kernel(pos_ref, h_ref, *rest):
    xs_ref, sem = rest[-2], rest[-1]
    tm = h_ref.shape[0]

    def issue(rb, carry):
        for u in range(ISSUE_UNROLL):
            r = rb * ISSUE_UNROLL + u
            for k in range(TOP_K):
                _row_copy(h_ref, r, xs_ref, pos_ref[0, 0, r * TOP_K + k], sem).start()
        return carry

    lax.fori_loop(0, tm // ISSUE_UNROLL, issue, 0)
    for _ in range(TOP_K):
        pltpu.make_async_copy(h_ref, xs_ref.at[pl.ds(0, tm)], sem).wait()


def _dispatch(pos, h, xs, n_slots, *, name):
    n, d = h.shape
    tm = ROW_TILE
    in_specs = [
        pl.BlockSpec((1, 1, tm * TOP_K), lambda i: (i, 0, 0), memory_space=pltpu.SMEM),
        pl.BlockSpec((tm, d), lambda i: (i, 0)),
    ]
    args = [pos.reshape(n // tm, 1, tm * TOP_K), h]
    aliases = {}
    if xs is not None:
        in_specs.append(pl.BlockSpec(memory_space=pl.ANY))
        args.append(xs)
        aliases = {2: 0}
    return pl.pallas_call(
        _dispatch_kernel,
        out_shape=jax.ShapeDtypeStruct((n_slots, d), F32),
        grid=(n // tm,),
        in_specs=in_specs,
        out_specs=pl.BlockSpec(memory_space=pl.ANY),
        scratch_shapes=[pltpu.SemaphoreType.DMA],
        input_output_aliases=aliases,
        compiler_params=_cparams("arbitrary"),
        name=name,
    )(*args)


def _padfill_kernel(lo_ref, hi_ref, xs_in, xs_ref, zero_ref, sem):
    del xs_in
    zero_ref[...] = jnp.zeros(zero_ref.shape, F32)
    for phase in range(2):
        for e in range(N_EXPERTS):
            def body(r, carry):
                cp = _row_copy(zero_ref, 0, xs_ref, r, sem)
                if phase == 0:
                    cp.start()
                else:
                    cp.wait()
                return carry

            lax.fori_loop(lo_ref[e], hi_ref[e], body, 0)


def _padfill(xs, lo, hi):
    n_slots, d = xs.shape
    return pl.pallas_call(
        _padfill_kernel,
        out_shape=jax.ShapeDtypeStruct((n_slots, d), F32),
        grid_spec=pltpu.PrefetchScalarGridSpec(
            num_scalar_prefetch=2,
            grid=(1,),
            in_specs=[pl.BlockSpec(memory_space=pl.ANY)],
            out_specs=pl.BlockSpec(memory_space=pl.ANY),
            scratch_shapes=[pltpu.VMEM((8, d), F32), pltpu.SemaphoreType.DMA],
        ),
        input_output_aliases={2: 0},
        compiler_params=_cparams("arbitrary"),
        name="moe_padfill",
    )(lo, hi, xs)


def _expert_kernel(te_ref, na_ref, xs_ref, wgu_ref, bgu_ref, wd_ref, bd_ref, y_ref):
    i = pl.program_id(0)

    @pl.when(i < na_ref[0])
    def _():
        x = xs_ref[...].astype(BF16)
        f = wd_ref.shape[0]
        half = f // 2
        y = None
        for lo in range(0, f, half):
            gt = jnp.dot(x, wgu_ref[:, lo:lo + half], preferred_element_type=F32) + bgu_ref[:, lo:lo + half]
            up = jnp.dot(x, wgu_ref[:, f + lo:f + lo + half], preferred_element_type=F32)
            up = up + bgu_ref[:, f + lo:f + lo + half]
            gt = jnp.minimum(gt, SWIGLU_LIMIT)
            up = jnp.clip(up, -SWIGLU_LIMIT, SWIGLU_LIMIT)
            act = (up + 1.0) * gt * _sigmoid(SWIGLU_ALPHA * gt)
            t = jnp.dot(act.astype(BF16), wd_ref[lo:lo + half, :], preferred_element_type=F32)
            y = t if y is None else y + t
        y_ref[...] = y + bd_ref[...]

    @pl.when(i >= na_ref[0])
    def _():
        y_ref[...] = jnp.zeros(y_ref.shape, F32)


def _experts(xs, tile_expert, n_active, w_gu, b_gu, w_down, b_down):
    n_slots, d = xs.shape
    tm = EXPERT_TILE
    n_e, _, two_f = w_gu.shape
    return pl.pallas_call(
        _expert_kernel,
        out_shape=jax.ShapeDtypeStruct((n_slots, d), F32),
        grid_spec=pltpu.PrefetchScalarGridSpec(
            num_scalar_prefetch=2,
            grid=(n_slots // tm,),
            in_specs=[
                pl.BlockSpec((tm, d), lambda i, te, na: (jnp.minimum(i, na[0] - 1), 0)),
                pl.BlockSpec((None, d, two_f), lambda i, te, na: (te[i], 0, 0)),
                pl.BlockSpec((None, 1, two_f), lambda i, te, na: (te[i], 0, 0)),
                pl.BlockSpec((None, two_f // 2, d), lambda i, te, na: (te[i], 0, 0)),
                pl.BlockSpec((None, 1, d), lambda i, te, na: (te[i], 0, 0)),
            ],
            out_specs=pl.BlockSpec((tm, d), lambda i, te, na: (i, 0)),
        ),
        compiler_params=_cparams("arbitrary"),
        name="moe_experts",
    )(tile_expert, n_active, xs, w_gu, b_gu.reshape(n_e, 1, two_f), w_down, b_down.reshape(n_e, 1, d))


def _combine_kernel(pos_ref, x_ref, wts_ref, gate_ref, *rest, final):
    if final:
        fg_ref, y_hbm, o_ref, ybuf, sem = rest
    else:
        y_hbm, o_ref, ybuf, sem = rest
    tm = x_ref.shape[0]

    def issue(rb, carry):
        for u in range(ISSUE_UNROLL):
            r = rb * ISSUE_UNROLL + u
            for k in range(TOP_K):
                _row_copy(y_hbm, pos_ref[0, 0, r * TOP_K + k], ybuf.at[k], r, sem).start()
        return carry

    lax.fori_loop(0, tm // ISSUE_UNROLL, issue, 0)
    for k in range(TOP_K):
        pltpu.make_async_copy(y_hbm.at[pl.ds(0, tm)], ybuf.at[k], sem).wait()
    wts = wts_ref[...]
    f = wts[:, 0:1] * ybuf[0]
    for k in range(1, TOP_K):
        f = f + wts[:, k:k + 1] * ybuf[k]
    out = x_ref[...] + gate_ref[...] * f
    if final:
        out = out * lax.rsqrt(jnp.mean(out * out, axis=-1, keepdims=True) + NORM_EPS) * fg_ref[...]
    o_ref[...] = out


def _combine(pos, x, wts, mods, row_fn, y, final_g, *, name):
    n, d = x.shape
    tm = ROW_TILE
    in_specs = [
        pl.BlockSpec((1, 1, tm * TOP_K), lambda i: (i, 0, 0), memory_space=pltpu.SMEM),
        pl.BlockSpec((tm, d), lambda i: (i, 0)),
        pl.BlockSpec((tm, TOP_K), lambda i: (i, 0)),
        _mod_spec(row_fn, 5, d),
    ]
    args = [pos.reshape(n // tm, 1, tm * TOP_K), x, wts, mods]
    if final_g is not None:
        in_specs.append(pl.BlockSpec((1, d), lambda i: (0, 0)))
        args.append(final_g.reshape(1, d))
    in_specs.append(pl.BlockSpec(memory_space=pl.ANY))
    args.append(y)
    return pl.pallas_call(
        functools.partial(_combine_kernel, final=final_g is not None),
        out_shape=jax.ShapeDtypeStruct((n, d), F32),
        grid=(n // tm,),
        in_specs=in_specs,
        out_specs=pl.BlockSpec((tm, d), lambda i: (i, 0)),
        scratch_shapes=[pltpu.VMEM((TOP_K, tm, d), F32), pltpu.SemaphoreType.DMA],
        compiler_params=_cparams("arbitrary"),
        name=name,
    )(*args)


def _moe(streams, mods, norm_g, w_r, b_r, w_gu, b_gu, w_down, b_down, final_g):
    d = streams[0][0].shape[1]
    tm = EXPERT_TILE
    w_rt = w_r.T
    b_col = b_r.reshape(N_EXPERTS, 1)
    tok = np.arange(ROW_TILE)
    tri = jnp.asarray(tok[:, None] < tok[None, :], BF16)
    cnt = jnp.zeros((N_EXPERTS, LANE), F32)
    routed = []
    for s, (x, row_fn) in enumerate(streams):
        h, info, cnt = _router(x, norm_g, mods, row_fn, w_rt, b_col, tri, cnt, name=f"moe_router_{s}")
        routed.append((h, info))
    counts = cnt[:, 0].astype(jnp.int32)
    padded = ((counts + tm - 1) // tm) * tm
    ends = jnp.cumsum(padded)
    offsets = ends - padded
    n_total = sum(x.shape[0] for x, _ in streams)
    n_slots = n_total * TOP_K + N_EXPERTS * tm
    n_tiles = n_slots // tm
    n_active = (ends[-1] // tm).astype(jnp.int32).reshape(1)
    tile_start = jnp.minimum(jnp.arange(n_tiles, dtype=jnp.int32), n_active[0] - 1) * tm
    tile_expert = jnp.sum((ends[None, :] <= tile_start[:, None]).astype(jnp.int32), axis=1)
    tile_expert = jnp.minimum(tile_expert, N_EXPERTS - 1)

    xs = None
    positions = []
    for s, (h, info) in enumerate(routed):
        eid = info[INFO_ID:INFO_ID + TOP_K].astype(jnp.int32)
        rank = info[INFO_RANK:INFO_RANK + TOP_K].astype(jnp.int32)
        base = jnp.zeros_like(eid)
        for e in range(N_EXPERTS):
            base = jnp.where(eid == e, offsets[e], base)
        pos = (base + rank).T
        positions.append(pos)
        xs = _dispatch(pos, h, xs, n_slots, name=f"moe_dispatch_{s}")
    xs = _padfill(xs, (offsets + counts).astype(jnp.int32), ends.astype(jnp.int32))
    y = _experts(xs, tile_expert, n_active, w_gu.astype(BF16), b_gu, w_down.astype(BF16), b_down)
    outs = []
    for s, ((x, row_fn), (h, info), pos) in enumerate(zip(streams, routed, positions)):
        wts = info[INFO_W:INFO_W + TOP_K].T
        outs.append(_combine(pos, x, wts, mods, row_fn, y, final_g, name=f"moe_combine_{s}"))
    return outs


def _pad_heads(w):
    lead = w.shape[:-1]
    w4 = w.reshape(*lead, GLA_HEADS, GLA_DK)
    w4 = jnp.pad(w4, [(0, 0)] * len(lead) + [(0, 0), (0, HEAD_PAD - GLA_DK)])
    return w4.reshape(*lead, QK_PAD)


def _pack_gla_in(w_in):
    d = w_in.shape[0]
    lr = jnp.pad(w_in[:, OFF_AF:OFF_AB + GLA_LOWRANK], ((0, 0), (0, LANE - 2 * GLA_LOWRANK)))
    return jnp.concatenate([
        _pad_heads(w_in[:, OFF_Q:OFF_Q + GLA_QK]),
        _pad_heads(w_in[:, OFF_K:OFF_K + GLA_QK]),
        w_in[:, OFF_V:OFF_V + GLA_V],
        w_in[:, OFF_G:OFF_G + GLA_V],
        w_in[:, OFF_GLU:OFF_GLU + 2 * CONV_CH],
        lr,
    ], axis=1).astype(BF16)


def _pack_decay(wa_f, ba_f, wa_b, ba_b):
    wa = jnp.zeros((LANE, 2 * QK_PAD), F32)
    wa = wa.at[0:GLA_LOWRANK, 0:QK_PAD].set(_pad_heads(wa_f))
    wa = wa.at[GLA_LOWRANK:2 * GLA_LOWRANK, QK_PAD:].set(_pad_heads(wa_b))
    ba = jnp.concatenate([_pad_heads(ba_f), _pad_heads(ba_b)]).reshape(1, 2 * QK_PAD)
    return wa.astype(BF16), ba


def _rope_tables(t_len):
    t = jnp.arange(t_len)
    row_pos = (t // GRID_W).astype(F32)
    col_pos = (t % GRID_W).astype(F32)
    half = GLA_DK // 4
    inv_freq = ROPE_BASE ** (-jnp.arange(half, dtype=F32) / half)
    dim = np.arange(HEAD_PAD)
    real = dim < GLA_DK
    use_col = (dim % GLA_DK) >= GLA_DK // 2
    first = (dim % (GLA_DK // 2)) < half
    pos = jnp.where(jnp.asarray(use_col)[None, :], col_pos[:, None], row_pos[:, None])
    ang = pos * inv_freq[dim % half][None, :]
    cos = jnp.where(jnp.asarray(real)[None, :], jnp.cos(ang), 0.0)
    sin = jnp.where(jnp.asarray(real)[None, :], jnp.sin(ang), 0.0)
    sin = jnp.where(jnp.asarray(first)[None, :], -sin, sin)
    return jnp.tile(cos, (1, GLA_HEADS)), jnp.tile(sin, (1, GLA_HEADS))


def kernel(x, c, ctx, c_ctx, ada_w, ada_b, norm1_g, norm2_g, gla_conv_w_in, gla_wa_fwd, gla_ba_fwd,
           gla_wa_bwd, gla_ba_bwd, gla_norm_g, conv_dw_w, conv_dw_b, conv_ln_g, conv_ln_b,
           gla_conv_w_out, na_w_qkv, na_rpb, na_w_out, router_w, router_b, expert_w_gu, expert_b_gu,
           expert_w_down, expert_b_down, final_norm_g):
    b, t_len, d = x.shape
    l_ctx = ctx.shape[1]
    depth = ada_w.shape[0]
    assert b + 1 <= MOD_ROWS and t_len % 512 == 0 and l_ctx % ROW_TILE == 0

    c_all = jnp.concatenate([c, c_ctx[None, :], jnp.zeros((MOD_ROWS - b - 1, d), F32)], axis=0)
    mods_all = _mods(c_all, ada_w, ada_b)

    x_lat = x.reshape(b * t_len, d)
    x_ctx = ctx.reshape(b * l_ctx, d)
    tm_lat = 512
    tm_ctx = min(512, l_ctx)

    def lat_row(i, tm=tm_lat):
        return (i * tm) // t_len

    def ctx_row(i):
        return b

    def lat_row_moe(i):
        return (i * ROW_TILE) // t_len

    for layer in range(depth):
        last = layer == depth - 1
        j = layer // 2
        mods = mods_all[layer]
        if layer % 2 == 0:
            w_pack = _pack_gla_in(gla_conv_w_in[j])
            wa, ba = _pack_decay(gla_wa_fwd[j], gla_ba_fwd[j], gla_wa_bwd[j], gla_ba_bwd[j])
            w_out = gla_conv_w_out[j].astype(BF16)
            zero_state = jnp.zeros((b, GLA_HEADS, GLA_DV, HEAD_PAD), F32)
            p_c = _norm_mod_matmul(x_ctx, norm1_g[layer], mods, ctx_row, (1, 0), w_pack, tm=tm_ctx,
                                   gla_layout=True, name="gla_in_ctx")
            y_gla_c, s_f, s_b = _gla(p_c, l_ctx, wa, ba, gla_norm_g[j], zero_state, zero_state)
            p_l = _norm_mod_matmul(x_lat, norm1_g[layer], mods, lat_row, (1, 0), w_pack, tm=tm_lat,
                                   gla_layout=True, rope=_rope_tables(t_len), name="gla_in_lat")
            y_gla_l, _, _ = _gla(p_l, t_len, wa, ba, gla_norm_g[j], s_f, s_b)
            y_conv_l = _conv(p_l, t_len, conv_dw_w[j], conv_dw_b[j], conv_ln_g[j], conv_ln_b[j])
            x_lat = _matmul_residual([y_gla_l, y_conv_l], w_out, x_lat, mods, lat_row, 2, tm=tm_lat,
                                     name="mix_out_lat")
            if not last:
                y_conv_c = _conv(p_c, l_ctx, conv_dw_w[j], conv_dw_b[j], conv_ln_g[j], conv_ln_b[j])
                x_ctx = _matmul_residual([y_gla_c, y_conv_c], w_out, x_ctx, mods, ctx_row, 2, tm=tm_ctx,
                                         name="mix_out_ctx")
        else:
            w_qkv = na_w_qkv[j].astype(BF16)
            kv_c = _norm_mod_matmul(x_ctx, norm1_g[layer], mods, ctx_row, (1, 0), w_qkv[:, NA_WIDTH:],
                                    tm=tm_ctx, name="na_kv_ctx")
            qkv = _norm_mod_matmul(x_lat, norm1_g[layer], mods, lat_row, (1, 0), w_qkv, tm=tm_lat,
                                   name="na_qkv_lat")
            o_l = _neighbourhood_attention(qkv, kv_c, na_rpb[j], t_len, l_ctx)
            x_lat = _matmul_residual([o_l], na_w_out[j].astype(BF16), x_lat, mods, lat_row, 2, tm=tm_lat,
                                     name="na_out_lat")
            if not last:
                raise NotImplementedError("context output of an attention layer is only needed mid-stack")
        streams = [(x_lat, lat_row_moe)]
        if not last:
            streams.append((x_ctx, ctx_row))
        outs = _moe(streams, mods, norm2_g[layer], router_w[layer], router_b[layer], expert_w_gu[layer],
                    expert_b_gu[layer], expert_w_down[layer], expert_b_down[layer],
                    final_norm_g if last else None)
        x_lat = outs[0]
        if not last:
            x_ctx = outs[1]
    return x_lat.reshape(b, t_len, d)
```

```python
import functools

import numpy as np
import jax
import jax.numpy as jnp
from jax import lax
from jax.experimental import pallas as pl
from jax.experimental.pallas import tpu as pltpu

F32 = jnp.float32
BF16 = jnp.bfloat16
HIGHEST = lax.Precision.HIGHEST

NORM_EPS = 1e-6
ROPE_BASE = 10000.0
GRID_W = 64

GLA_HEADS = 4
GLA_DK = 64
GLA_DV = 128
GLA_LOWRANK = 16
GLA_TEMP = 16.0
GLA_CHUNK = 64
GLA_QK = GLA_HEADS * GLA_DK
GLA_V = GLA_HEADS * GLA_DV
CONV_CH = 512
CONV_WIDTH = 31
NA_HEADS = 16
NA_HEAD_DIM = 64
NA_WIDTH = NA_HEADS * NA_HEAD_DIM
NA_WIN_R = 8
NA_WIN_C = 16
N_EXPERTS = 32
TOP_K = 4
D_EXPERT = 1024
SWIGLU_LIMIT = 7.0
SWIGLU_ALPHA = 1.702

OFF_Q = 0
OFF_G = OFF_Q + GLA_QK
OFF_GLU = OFF_G + GLA_V
OFF_K = OFF_GLU + 2 * CONV_CH
OFF_V = OFF_K + GLA_QK
OFF_AF = OFF_V + GLA_V
OFF_AB = OFF_AF + GLA_LOWRANK

LANE = 128
VMEM_LIMIT = 56 * 1024 * 1024

HEAD_PAD = LANE
QK_PAD = GLA_HEADS * HEAD_PAD
PK_Q, PK_K, PK_V, PK_G, PK_A, PK_GT, PK_LR = 0, 512, 1024, 1536, 2048, 2560, 3072
PK_WIDTH = PK_LR + LANE

MOD_ROWS = 40
NEG_BIG = -1e30
ROW_TILE = 256
EXPERT_TILE = 512
ISSUE_UNROLL = 8
EXPERT_SPLIT = 4


def _cparams(*sem):
    return pltpu.CompilerParams(dimension_semantics=sem, vmem_limit_bytes=VMEM_LIMIT)


def _norm_mod(x, g, scale, shift):
    y = x * lax.rsqrt(jnp.mean(x * x, axis=-1, keepdims=True) + NORM_EPS) * g
    return y * (1.0 + scale) + shift


def _sigmoid(x):
    return 1.0 / (1.0 + jnp.exp(-x))


def _mods_kernel(c_ref, w_ref, b_ref, o_ref):
    c = c_ref[...]
    s = c * _sigmoid(c)
    o_ref[...] = jnp.dot(s, w_ref[...], precision=HIGHEST, preferred_element_type=F32) + b_ref[...]


def _mods(c_all, ada_w, ada_b):
    depth, d, n6 = ada_w.shape
    nb = 512
    out = pl.pallas_call(
        _mods_kernel,
        out_shape=jax.ShapeDtypeStruct((depth, MOD_ROWS, n6), F32),
        grid=(depth, n6 // nb),
        in_specs=[
            pl.BlockSpec((MOD_ROWS, d), lambda l, j: (0, 0)),
            pl.BlockSpec((None, d, nb), lambda l, j: (l, 0, j)),
            pl.BlockSpec((None, 1, nb), lambda l, j: (l, 0, j)),
        ],
        out_specs=pl.BlockSpec((None, MOD_ROWS, nb), lambda l, j: (l, 0, j)),
        compiler_params=_cparams("arbitrary", "arbitrary"),
        name="adaln_mods",
    )(c_all, ada_w, ada_b.reshape(depth, 1, n6))
    return out.reshape(depth, MOD_ROWS * 6, 1, d)


def _mod_spec(row_fn, chunk, d):
    return pl.BlockSpec((None, 1, d), lambda i, *_: (row_fn(i) * 6 + chunk, 0, 0))


def _nmm_kernel(x_ref, g_ref, sc_ref, sh_ref, w_ref, *rest, gla_layout, rope):
    if rope:
        cos_ref, sin_ref, o_ref = rest
    else:
        (o_ref,) = rest
    h = _norm_mod(x_ref[...], g_ref[...], sc_ref[...], sh_ref[...]).astype(BF16)
    nout = o_ref.shape[1]
    chunk = 512
    for j0 in range(0, nout, chunk):
        j1 = min(j0 + chunk, nout)
        acc = jnp.dot(h, w_ref[:, j0:j1], preferred_element_type=F32)
        if gla_layout and j0 in (PK_Q, PK_K):
            if rope:
                lane = lax.broadcasted_iota(jnp.int32, acc.shape, 1)
                first = (lane % 32) < 16
                rot = jnp.where(first, pltpu.roll(acc, QK_PAD - 16, 1), pltpu.roll(acc, 16, 1))
                acc = acc * cos_ref[...] + rot * sin_ref[...]
            if j0 == PK_Q:
                acc = acc * (GLA_DK ** -0.5)
        o_ref[:, j0:j1] = acc.astype(o_ref.dtype)


def _norm_mod_matmul(x, g, mods, row_fn, chunks, w, *, tm, gla_layout=False, rope=None, name):
    n, d = x.shape
    nout = w.shape[1]
    in_specs = [
        pl.BlockSpec((tm, d), lambda i: (i, 0)),
        pl.BlockSpec((1, d), lambda i: (0, 0)),
        _mod_spec(row_fn, chunks[0], d),
        _mod_spec(row_fn, chunks[1], d),
        pl.BlockSpec((d, nout), lambda i: (0, 0)),
    ]
    args = [x, g.reshape(1, d), mods, mods, w]
    if rope is not None:
        cos_t, sin_t = rope
        t_tiles = cos_t.shape[0] // tm
        in_specs += [pl.BlockSpec((tm, QK_PAD), lambda i: (i % t_tiles, 0))] * 2
        args += [cos_t, sin_t]
    return pl.pallas_call(
        functools.partial(_nmm_kernel, gla_layout=gla_layout, rope=rope is not None),
        out_shape=jax.ShapeDtypeStruct((n, nout), BF16),
        grid=(n // tm,),
        in_specs=in_specs,
        out_specs=pl.BlockSpec((tm, nout), lambda i: (i, 0)),
        compiler_params=_cparams("arbitrary"),
        name=name,
    )(*args)


def _mmres_kernel(*refs, n_parts):
    a_refs = refs[:n_parts]
    w_ref, x_ref, gate_ref, o_ref = refs[n_parts:]
    acc = None
    k0 = 0
    for a_ref in a_refs:
        kk = a_ref.shape[1]
        t = jnp.dot(a_ref[...], w_ref[k0:k0 + kk, :], preferred_element_type=F32)
        acc = t if acc is None else acc + t
        k0 += kk
    o_ref[...] = x_ref[...] + gate_ref[...] * acc


def _matmul_residual(parts, w, x, mods, row_fn, gate_chunk, *, tm, name):
    n, d = x.shape
    in_specs = [pl.BlockSpec((tm, a.shape[1]), lambda i: (i, 0)) for a in parts]
    in_specs += [
        pl.BlockSpec(w.shape, lambda i: (0, 0)),
        pl.BlockSpec((tm, d), lambda i: (i, 0)),
        _mod_spec(row_fn, gate_chunk, d),
    ]
    return pl.pallas_call(
        functools.partial(_mmres_kernel, n_parts=len(parts)),
        out_shape=jax.ShapeDtypeStruct((n, d), F32),
        grid=(n // tm,),
        in_specs=in_specs,
        out_specs=pl.BlockSpec((tm, d), lambda i: (i, 0)),
        compiler_params=_cparams("arbitrary"),
        name=name,
    )(*parts, w, x, mods)


def _gla_kernel(q_ref, k_ref, v_ref, g_ref, a_ref, wa_ref, ba_ref, ng_ref, s0f_ref, s0b_ref,
                y_ref, sf_ref, sb_ref, of_ref, ob_ref):
    t_len = q_ref.shape[0]
    n_chunks = t_len // GLA_CHUNK
    c = GLA_CHUNK
    sf_ref[...] = s0f_ref[...]
    sb_ref[...] = s0b_ref[...]

    row = lax.broadcasted_iota(jnp.int32, (c, c), 0)
    col = lax.broadcasted_iota(jnp.int32, (c, c), 1)
    lane = lax.broadcasted_iota(jnp.int32, (1, QK_PAD), 1)
    real_lane = ((lane % HEAD_PAD) < GLA_DK).astype(F32)

    def dir_step(ci, forward):
        r0 = pl.multiple_of(ci * c, c)
        rows = pl.ds(r0, c)
        off = 0 if forward else QK_PAD
        z = jnp.dot(a_ref[rows, :], wa_ref[:, off:off + QK_PAD], preferred_element_type=F32)
        z = z + ba_ref[:, off:off + QK_PAD]
        log_sig = jnp.minimum(z, 0.0) - jnp.log1p(jnp.exp(-jnp.abs(z)))
        la = log_sig * (1.0 / GLA_TEMP) * real_lane
        keep = (row >= col) if forward else (row <= col)
        bc = jnp.dot(keep.astype(F32), la, precision=HIGHEST, preferred_element_type=F32)
        b_last = bc[c - 1:c, :] if forward else bc[0:1, :]
        q = q_ref[rows, :].astype(F32)
        k = k_ref[rows, :].astype(F32)
        v = v_ref[rows, :]
        qt = (q * jnp.exp(bc)).astype(BF16)
        kt = (k * jnp.exp(-bc)).astype(BF16)
        kd = (k * jnp.exp(b_last - bc)).astype(BF16)
        decay = jnp.exp(b_last)
        s_ref = sf_ref if forward else sb_ref
        o_ref = of_ref if forward else ob_ref
        for h in range(GLA_HEADS):
            sl = slice(h * HEAD_PAD, (h + 1) * HEAD_PAD)
            att = lax.dot_general(qt[:, sl], kt[:, sl], (((1,), (1,)), ((), ())),
                                  preferred_element_type=F32)
            att = jnp.where(keep, att, 0.0).astype(BF16)
            st = s_ref[0, h]
            o = jnp.dot(att, v[:, sl], preferred_element_type=F32)
            o = o + lax.dot_general(qt[:, sl], st.astype(BF16), (((1,), (1,)), ((), ())),
                                    preferred_element_type=F32)
            kv_t = lax.dot_general(v[:, sl], kd[:, sl], (((0,), (0,)), ((), ())),
                                   preferred_element_type=F32)
            s_ref[0, h] = st * decay[:, sl] + kv_t
            o_ref[rows, sl] = o

    def step(i, carry):
        dir_step(i, True)
        dir_step(n_chunks - 1 - i, False)
        return carry

    lax.fori_loop(0, n_chunks, step, 0)

    blk = min(256, t_len)

    def post(i, carry):
        rows = pl.ds(pl.multiple_of(i * blk, blk), blk)
        o = of_ref[rows, :] + ob_ref[rows, :]
        g = g_ref[rows, :].astype(F32)
        gate = g * _sigmoid(g)
        for h in range(GLA_HEADS):
            sl = slice(h * GLA_DV, (h + 1) * GLA_DV)
            oh = o[:, sl]
            oh = oh * lax.rsqrt(jnp.mean(oh * oh, axis=-1, keepdims=True) + NORM_EPS)
            y_ref[rows, sl] = (oh * ng_ref[:, sl] * gate[:, sl]).astype(y_ref.dtype)
        return carry

    lax.fori_loop(0, t_len // blk, post, 0)


def _gla(p, t_len, wa, ba, norm_g, s0f, s0b):
    n = p.shape[0]
    b = n // t_len

    def col(width, start):
        return pl.BlockSpec((t_len, width), lambda i: (i, start // width))

    st_spec = pl.BlockSpec((1, GLA_HEADS, GLA_DV, HEAD_PAD), lambda i: (i, 0, 0, 0))
    st_shape = jax.ShapeDtypeStruct((b, GLA_HEADS, GLA_DV, HEAD_PAD), F32)
    return pl.pallas_call(
        _gla_kernel,
        out_shape=(jax.ShapeDtypeStruct((n, GLA_V), BF16), st_shape, st_shape),
        grid=(b,),
        in_specs=[
            col(QK_PAD, PK_Q), col(QK_PAD, PK_K), col(GLA_V, PK_V), col(GLA_V, PK_G), col(LANE, PK_LR),
            pl.BlockSpec(wa.shape, lambda i: (0, 0)),
            pl.BlockSpec(ba.shape, lambda i: (0, 0)),
            pl.BlockSpec((1, GLA_V), lambda i: (0, 0)),
            st_spec, st_spec,
        ],
        out_specs=(pl.BlockSpec((t_len, GLA_V), lambda i: (i, 0)), st_spec, st_spec),
        scratch_shapes=[pltpu.VMEM((t_len, GLA_V), F32), pltpu.VMEM((t_len, GLA_V), F32)],
        compiler_params=_cparams("arbitrary"),
        name="gla_scan",
    )(p, p, p, p, p, wa, ba, norm_g.reshape(1, GLA_V), s0f, s0b)


CONV_HALO = 16
CONV_ROWS = 64


def _conv_kernel(a_ref, gt_ref, w_ref, cb_ref, lg_ref, lb_ref, y_ref, u_ref):
    t_len = a_ref.shape[0]
    zeros = jnp.zeros((CONV_HALO, CONV_CH), F32)
    u_ref[0:CONV_HALO, :] = zeros
    u_ref[CONV_HALO + t_len:2 * CONV_HALO + t_len, :] = zeros
    blk = min(256, t_len)

    def glu(i, carry):
        r0 = pl.multiple_of(i * blk, blk)
        a = a_ref[pl.ds(r0, blk), :].astype(F32)
        gt = gt_ref[pl.ds(r0, blk), :].astype(F32)
        u_ref[pl.ds(r0 + CONV_HALO, blk), :] = a * _sigmoid(gt)
        return carry

    lax.fori_loop(0, t_len // blk, glu, 0)
    shift = CONV_HALO - CONV_WIDTH // 2
    win_rows = CONV_ROWS + 2 * CONV_HALO

    def tile(i, carry):
        r0 = pl.multiple_of(i * CONV_ROWS, CONV_ROWS)
        parts = []
        for lb in range(CONV_CH // LANE):
            ls = slice(lb * LANE, (lb + 1) * LANE)
            win = u_ref[pl.ds(r0, win_rows), ls]
            acc = jnp.zeros((CONV_ROWS, LANE), F32)
            for b in range(8):
                wb = win if b == 0 else pltpu.roll(win, win_rows - b, 0)
                for a in range(win_rows // 8):
                    j = 8 * a + b - shift
                    if 0 <= j < CONV_WIDTH:
                        acc = acc + wb[8 * a:8 * a + CONV_ROWS, :] * w_ref[j:j + 1, ls]
            parts.append(acc)
        y = jnp.concatenate(parts, axis=1) + cb_ref[...]
        mu = jnp.mean(y, axis=-1, keepdims=True)
        yc = y - mu
        var = jnp.mean(yc * yc, axis=-1, keepdims=True)
        yn = yc * lax.rsqrt(var + NORM_EPS) * lg_ref[...] + lb_ref[...]
        y_ref[pl.ds(r0, CONV_ROWS), :] = (yn * _sigmoid(yn)).astype(y_ref.dtype)
        return carry

    lax.fori_loop(0, t_len // CONV_ROWS, tile, 0)


def _conv(p, t_len, conv_w, conv_b, ln_g, ln_b):
    n = p.shape[0]
    w_pad = jnp.pad(conv_w, ((0, 32 - CONV_WIDTH), (0, 0)))
    vec = pl.BlockSpec((1, CONV_CH), lambda i: (0, 0))
    return pl.pallas_call(
        _conv_kernel,
        out_shape=jax.ShapeDtypeStruct((n, CONV_CH), BF16),
        grid=(n // t_len,),
        in_specs=[
            pl.BlockSpec((t_len, CONV_CH), lambda i: (i, PK_A // CONV_CH)),
            pl.BlockSpec((t_len, CONV_CH), lambda i: (i, PK_GT // CONV_CH)),
            pl.BlockSpec((32, CONV_CH), lambda i: (0, 0)),
            vec, vec, vec,
        ],
        out_specs=pl.BlockSpec((t_len, CONV_CH), lambda i: (i, 0)),
        scratch_shapes=[pltpu.VMEM((t_len + 2 * CONV_HALO, CONV_CH), F32)],
        compiler_params=_cparams("arbitrary"),
        name="conformer_conv",
    )(p, p, w_pad, conv_b.reshape(1, -1), ln_g.reshape(1, -1), ln_b.reshape(1, -1))


NA_QROWS = 4
NA_BAND = 12


def _na_geometry(rows):
    assert rows >= NA_BAND and rows % NA_QROWS == 0
    win_r = min(NA_WIN_R, rows)
    starts, classes, sigs = [], [], []
    for rb in range(rows // NA_QROWS):
        bs = int(np.clip(rb * NA_QROWS - NA_WIN_R // 2, 0, rows - NA_BAND))
        sig = tuple((rb * NA_QROWS + i - bs,
                     int(np.clip(rb * NA_QROWS + i - win_r // 2, 0, rows - win_r)) - bs)
                    for i in range(NA_QROWS))
        if sig not in sigs:
            sigs.append(sig)
        starts.append(bs)
        classes.append(sigs.index(sig))
    return starts, classes, sigs, win_r


def _na_bias(rpb, sigs, win_r):
    w = GRID_W
    cols = np.arange(w)
    c_start = np.clip(cols - NA_WIN_C // 2, 0, w - NA_WIN_C)
    col_ok = (cols[None, :] >= c_start[:, None]) & (cols[None, :] < c_start[:, None] + NA_WIN_C)
    dc = np.clip(cols[None, :] - cols[:, None] + NA_WIN_C - 1, 0, 2 * NA_WIN_C - 2)
    rpb_col = jnp.take(rpb.astype(F32), jnp.asarray(dc.reshape(-1)), axis=2)
    rpb_col = rpb_col.reshape(NA_HEADS, 2 * NA_WIN_R - 1, w, w)
    rpb_col = jnp.where(jnp.asarray(col_ok)[None, None], rpb_col, NEG_BIG)
    kj = np.arange(NA_BAND)
    out = []
    for sig in sigs:
        q_rel = np.array([s[0] for s in sig])
        r_rel = np.array([s[1] for s in sig])
        row_ok = (kj[None, :] >= r_rel[:, None]) & (kj[None, :] < r_rel[:, None] + win_r)
        dr = np.clip(kj[None, :] - q_rel[:, None] + NA_WIN_R - 1, 0, 2 * NA_WIN_R - 2)
        bias = jnp.take(rpb_col, jnp.asarray(dr.reshape(-1)), axis=1)
        bias = bias.reshape(NA_HEADS, NA_QROWS, NA_BAND, w, w)
        bias = jnp.where(jnp.asarray(row_ok)[None, :, :, None, None], bias, NEG_BIG)
        out.append(bias.transpose(0, 1, 3, 2, 4).reshape(NA_HEADS, NA_QROWS * w, NA_BAND * w))
    return jnp.stack(out, axis=0)


def _na_kernel(q_ref, k_ref, v_ref, kc_ref, vc_ref, bias_ref, o_ref, *, starts, classes):
    w = GRID_W
    scale = NA_HEAD_DIM ** -0.5
    nq = NA_QROWS * w
    lane = lax.broadcasted_iota(jnp.int32, (nq, LANE), 1)
    kc = kc_ref[...]
    vc = vc_ref[...]
    nt = (((1,), (1,)), ((), ()))
    for rb, (bs, cls) in enumerate(zip(starts, classes)):
        q2 = q_ref[rb * nq:(rb + 1) * nq, :]
        kb = k_ref[bs * w:(bs + NA_BAND) * w, :]
        vb = v_ref[bs * w:(bs + NA_BAND) * w, :]
        acc = None
        for hh in range(2):
            mine = (lane >= hh * NA_HEAD_DIM) & (lane < (hh + 1) * NA_HEAD_DIM)
            qh = jnp.where(mine, q2, jnp.zeros_like(q2)) * scale
            s_loc = lax.dot_general(qh, kb, nt, preferred_element_type=F32) + bias_ref[cls, hh]
            s_ctx = lax.dot_general(qh, kc, nt, preferred_element_type=F32)
            m = jnp.maximum(jnp.max(s_loc, axis=-1, keepdims=True), jnp.max(s_ctx, axis=-1, keepdims=True))
            p_loc = jnp.exp(s_loc - m)
            p_ctx = jnp.exp(s_ctx - m)
            den = jnp.sum(p_loc, axis=-1, keepdims=True) + jnp.sum(p_ctx, axis=-1, keepdims=True)
            o = jnp.dot(p_loc.astype(BF16), vb, preferred_element_type=F32)
            o = o + jnp.dot(p_ctx.astype(BF16), vc, preferred_element_type=F32)
            o = o / den
            acc = o if acc is None else jnp.where(mine, o, acc)
        o_ref[rb * nq:(rb + 1) * nq, :] = acc.astype(o_ref.dtype)


def _neighbourhood_attention(qkv, kv_ctx, rpb, t_len, l_ctx):
    n = qkv.shape[0]
    b = n // t_len
    rows = t_len // GRID_W
    starts, classes, sigs, win_r = _na_geometry(rows)
    bias = _na_bias(rpb, sigs, win_r)
    n_pairs = NA_HEADS // 2
    nq, nk = NA_QROWS * GRID_W, NA_BAND * GRID_W
    return pl.pallas_call(
        functools.partial(_na_kernel, starts=starts, classes=classes),
        out_shape=jax.ShapeDtypeStruct((n, NA_WIDTH), BF16),
        grid=(n_pairs, b),
        in_specs=[
            pl.BlockSpec((t_len, LANE), lambda j, i: (i, j)),
            pl.BlockSpec((t_len, LANE), lambda j, i: (i, n_pairs + j)),
            pl.BlockSpec((t_len, LANE), lambda j, i: (i, 2 * n_pairs + j)),
            pl.BlockSpec((l_ctx, LANE), lambda j, i: (i, j)),
            pl.BlockSpec((l_ctx, LANE), lambda j, i: (i, n_pairs + j)),
            pl.BlockSpec((len(sigs), 2, nq, nk), lambda j, i: (0, j, 0, 0)),
        ],
        out_specs=pl.BlockSpec((t_len, LANE), lambda j, i: (i, j)),
        compiler_params=_cparams("arbitrary", "arbitrary"),
        name="neighbourhood_attention",
    )(qkv, qkv, qkv, kv_ctx, kv_ctx, bias)


INFO_ID, INFO_W, INFO_RANK = 0, TOP_K, 2 * TOP_K
INFO_ROWS = 16


def _router_kernel(x_ref, g_ref, sc_ref, sh_ref, wrt_ref, br_ref, tri_ref, cnt0_ref, *rest):
    h_ref, info_ref, cnt_ref = rest[-3:]
    tm = x_ref.shape[0]

    @pl.when(pl.program_id(0) == 0)
    def _():
        cnt_ref[...] = cnt0_ref[...]

    h = _norm_mod(x_ref[...], g_ref[...], sc_ref[...], sh_ref[...])
    h_ref[...] = h
    logits = lax.dot_general(wrt_ref[...], h, (((1,), (1,)), ((), ())), precision=HIGHEST,
                             preferred_element_type=F32) + br_ref[...]
    sub = lax.broadcasted_iota(jnp.int32, (N_EXPERTS, tm), 0)
    cur = logits
    vals, ids = [], []
    for _ in range(TOP_K):
        m = jnp.max(cur, axis=0, keepdims=True)
        idx = jnp.min(jnp.where(cur == m, sub, N_EXPERTS), axis=0, keepdims=True)
        vals.append(m)
        ids.append(idx)
        cur = jnp.where(sub == idx, NEG_BIG, cur)
    ex = [jnp.exp(v - vals[0]) for v in vals]
    den = ex[0] + ex[1] + ex[2] + ex[3]
    onehot = jnp.zeros((N_EXPERTS, tm), F32)
    for idx in ids:
        onehot = onehot + (sub == idx).astype(F32)
    before = jnp.dot(onehot.astype(BF16), tri_ref[...], preferred_element_type=F32)
    running = cnt_ref[:, 0:1]
    base = running + before
    row = lax.broadcasted_iota(jnp.int32, (INFO_ROWS, tm), 0)
    info = jnp.zeros((INFO_ROWS, tm), F32)
    for k in range(TOP_K):
        rank = jnp.sum(jnp.where(sub == ids[k], base, 0.0), axis=0, keepdims=True)
        info = info + jnp.where(row == INFO_ID + k, ids[k].astype(F32), 0.0)
        info = info + jnp.where(row == INFO_W + k, ex[k] / den, 0.0)
        info = info + jnp.where(row == INFO_RANK + k, rank, 0.0)
    info_ref[...] = info
    cnt_ref[...] = jnp.broadcast_to(running + jnp.sum(onehot, axis=1, keepdims=True), cnt_ref.shape)


def _router(x, start, n, g, mods, row_fn, w_rt, b_r, tri, cnt0, h_prev, h_rows, h_start, *, name):
    d = x.shape[1]
    tm = ROW_TILE
    t0, ht0 = start // tm, h_start // tm
    in_specs = [
        pl.BlockSpec((tm, d), lambda i: (i + t0, 0)),
        pl.BlockSpec((1, d), lambda i: (0, 0)),
        _mod_spec(lambda i: row_fn(i + t0), 4, d),
        _mod_spec(lambda i: row_fn(i + t0), 3, d),
        pl.BlockSpec((N_EXPERTS, d), lambda i: (0, 0)),
        pl.BlockSpec((N_EXPERTS, 1), lambda i: (0, 0)),
        pl.BlockSpec((tm, tm), lambda i: (0, 0)),
        pl.BlockSpec((N_EXPERTS, LANE), lambda i: (0, 0)),
    ]
    args = [x, g.reshape(1, d), mods, mods, w_rt, b_r, tri, cnt0]
    aliases = {}
    if h_prev is not None:
        in_specs.append(pl.BlockSpec(memory_space=pl.ANY))
        args.append(h_prev)
        aliases = {len(args) - 1: 0}
    return pl.pallas_call(
        _router_kernel,
        out_shape=(jax.ShapeDtypeStruct((h_rows, d), F32),
                   jax.ShapeDtypeStruct((INFO_ROWS, n), F32),
                   jax.ShapeDtypeStruct((N_EXPERTS, LANE), F32)),
        grid=(n // tm,),
        in_specs=in_specs,
        out_specs=(pl.BlockSpec((tm, d), lambda i: (i + ht0, 0)),
                   pl.BlockSpec((INFO_ROWS, tm), lambda i: (0, i)),
                   pl.BlockSpec((N_EXPERTS, LANE), lambda i: (0, 0))),
        input_output_aliases=aliases,
        compiler_params=_cparams("arbitrary"),
        name=name,
    )(*args)


def _row_copy(src_ref, src_row, dst_ref, dst_row, sem):
    return pltpu.make_async_copy(src_ref.at[pl.ds(src_row, 1)], dst_ref.at[pl.ds(dst_row, 1)], sem)


def _dispatch_kernel(pos_ref, h_ref, *rest):
    xs_ref, sem = rest[-2], rest[-1]
    tm = h_ref.shape[0]

    def issue(rb, carry):
        for u in range(ISSUE_UNROLL):
            r = rb * ISSUE_UNROLL + u
            for k in range(TOP_K):
                _row_copy(h_ref, r, xs_ref, pos_ref[0, 0, r * TOP_K + k], sem).start()
        return carry

    lax.fori_loop(0, tm // ISSUE_UNROLL, issue, 0)
    for _ in range(TOP_K):
        pltpu.make_async_copy(h_ref, xs_ref.at[pl.ds(0, tm)], sem).wait()


def _dispatch(pos, h, xs, n_slots, *, name):
    n, d = h.shape
    tm = ROW_TILE
    in_specs = [
        pl.BlockSpec((1, 1, tm * TOP_K), lambda i: (i, 0, 0), memory_space=pltpu.SMEM),
        pl.BlockSpec((tm, d), lambda i: (i, 0)),
    ]
    args = [pos.reshape(n // tm, 1, tm * TOP_K), h]
    aliases = {}
    if xs is not None:
        in_specs.append(pl.BlockSpec(memory_space=pl.ANY))
        args.append(xs)
        aliases = {2: 0}
    return pl.pallas_call(
        _dispatch_kernel,
        out_shape=jax.ShapeDtypeStruct((n_slots, d), F32),
        grid=(n // tm,),
        in_specs=in_specs,
        out_specs=pl.BlockSpec(memory_space=pl.ANY),
        scratch_shapes=[pltpu.SemaphoreType.DMA],
        input_output_aliases=aliases,
        compiler_params=_cparams("arbitrary"),
        name=name,
    )(*args)


PAD_ROWS = 8


def _padfill_kernel(lo_ref, mid_ref, hi_ref, xs_in, xs_ref, zero_ref, sem):
    del xs_in
    zero_ref[...] = jnp.zeros(zero_ref.shape, F32)
    for phase in range(2):
        for e in range(N_EXPERTS):
            def row(r, carry):
                cp = _row_copy(zero_ref, 0, xs_ref, r, sem)
                cp.start() if phase == 0 else cp.wait()
                return carry

            def block(b, carry):
                r = pl.multiple_of(mid_ref[e] + b * PAD_ROWS, PAD_ROWS)
                cp = pltpu.make_async_copy(zero_ref, xs_ref.at[pl.ds(r, PAD_ROWS)], sem)
                cp.start() if phase == 0 else cp.wait()
                return carry

            lax.fori_loop(lo_ref[e], mid_ref[e], row, 0)
            lax.fori_loop(0, (hi_ref[e] - mid_ref[e]) // PAD_ROWS, block, 0)


def _padfill(xs, lo, hi):
    n_rows, d = xs.shape
    mid = jnp.minimum(((lo + PAD_ROWS - 1) // PAD_ROWS) * PAD_ROWS, hi)
    return pl.pallas_call(
        _padfill_kernel,
        out_shape=jax.ShapeDtypeStruct((n_rows, d), F32),
        grid_spec=pltpu.PrefetchScalarGridSpec(
            num_scalar_prefetch=3,
            grid=(1,),
            in_specs=[pl.BlockSpec(memory_space=pl.ANY)],
            out_specs=pl.BlockSpec(memory_space=pl.ANY),
            scratch_shapes=[pltpu.VMEM((PAD_ROWS, d), F32), pltpu.SemaphoreType.DMA],
        ),
        input_output_aliases={3: 0},
        compiler_params=_cparams("arbitrary"),
        name="moe_padfill",
    )(lo, mid, hi, xs)


def _expert_kernel(te_ref, na_ref, xs_ref, wgu_ref, bgu_ref, wd_ref, bd_ref, *rest, side, q, n_side):
    if side is None:
        y_ref, wgu_bf, wd_bf = rest
    else:
        spos_ref, src_hbm, y_ref, dst_hbm, wgu_bf, wd_bf, sem = rest
    i = pl.program_id(0)
    active = i < na_ref[0]

    def side_issue(part=0, parts=1):
        for j in range(part * q // parts, (part + 1) * q // parts):
            t = i * (q // TOP_K) + j // TOP_K
            if side == "scatter":
                _row_copy(src_hbm, jnp.minimum(t, n_side - 1), dst_hbm, spos_ref[0, 0, j], sem).start()
            else:
                _row_copy(src_hbm, spos_ref[0, 0, j], dst_hbm.at[j % TOP_K], jnp.minimum(t, n_side), sem).start()

    def side_wait():
        dst = dst_hbm if side == "scatter" else dst_hbm.at[0]
        pltpu.make_async_copy(src_hbm.at[pl.ds(0, q)], dst.at[pl.ds(0, q)], sem).wait()

    if side is not None:
        @pl.when(i > 0)
        def _():
            side_wait()

    first_of_expert = (i == 0) | (te_ref[i] != te_ref[jnp.maximum(i - 1, 0)])

    @pl.when(active & first_of_expert)
    def _():
        rows = 256
        for r in range(0, wgu_ref.shape[0], rows):
            wgu_bf[r:r + rows, :] = wgu_ref[r:r + rows, :].astype(BF16)
        for r in range(0, wd_ref.shape[0], rows):
            wd_bf[r:r + rows, :] = wd_ref[r:r + rows, :].astype(BF16)

    @pl.when(active)
    def _():
        x = xs_ref[...].astype(BF16)
        f = wd_ref.shape[0]
        half = f // EXPERT_SPLIT
        y = None
        part, parts = 0, 3 * EXPERT_SPLIT

        def issue_next():
            nonlocal part
            if side is not None:
                side_issue(part, parts)
            part += 1

        for lo in range(0, f, half):
            issue_next()
            gt = jnp.dot(x, wgu_bf[:, lo:lo + half], preferred_element_type=F32) + bgu_ref[:, lo:lo + half]
            issue_next()
            up = jnp.dot(x, wgu_bf[:, f + lo:f + lo + half], preferred_element_type=F32)
            up = up + bgu_ref[:, f + lo:f + lo + half]
            gt = jnp.minimum(gt, SWIGLU_LIMIT)
            up = jnp.clip(up, -SWIGLU_LIMIT, SWIGLU_LIMIT)
            act = (up + 1.0) * gt * _sigmoid(SWIGLU_ALPHA * gt)
            issue_next()
            t = jnp.dot(act.astype(BF16), wd_bf[lo:lo + half, :], preferred_element_type=F32)
            y = t if y is None else y + t
        y_ref[...] = y + bd_ref[...]

    @pl.when(jnp.logical_not(active))
    def _():
        if side is not None:
            side_issue()
        y_ref[...] = jnp.zeros(y_ref.shape, F32)

    if side is not None:
        @pl.when(i == pl.num_programs(0) - 1)
        def _():
            side_wait()


def _experts(xs, n_slots, tile_expert, n_active, layer, w_gu, b_gu, w_down, b_down, side=None):
    d = xs.shape[1]
    tm = EXPERT_TILE
    depth, n_e, _, two_f = w_gu.shape
    n_steps = n_slots // tm
    in_specs = [
        pl.BlockSpec((tm, d), lambda i, te, na: (jnp.minimum(i, na[0] - 1), 0)),
        pl.BlockSpec((None, None, d, two_f), lambda i, te, na: (layer, te[i], 0, 0)),
        pl.BlockSpec((None, None, 1, two_f), lambda i, te, na: (layer, te[i], 0, 0)),
        pl.BlockSpec((None, None, two_f // 2, d), lambda i, te, na: (layer, te[i], 0, 0)),
        pl.BlockSpec((None, None, 1, d), lambda i, te, na: (layer, te[i], 0, 0)),
    ]
    args = [xs, w_gu, b_gu.reshape(depth, n_e, 1, two_f), w_down, b_down.reshape(depth, n_e, 1, d)]
    out_shape = [jax.ShapeDtypeStruct((n_slots, d), F32)]
    out_specs = [pl.BlockSpec((tm, d), lambda i, te, na: (i, 0))]
    scratch = [pltpu.VMEM((d, two_f), BF16), pltpu.VMEM((two_f // 2, d), BF16)]
    kind, q, n_side = None, 0, 0
    if side is not None:
        kind, pos, src, n_side, dst_slots = side
        pairs = n_side * TOP_K
        q = -(-pairs // n_steps)
        q = -(-q // PAD_ROWS) * PAD_ROWS
        if kind == "scatter":
            spare = jnp.full((n_steps * q - pairs,), dst_slots, jnp.int32)
            out_shape.append(jax.ShapeDtypeStruct((dst_slots + PAD_ROWS, d), F32))
        else:
            spare = jnp.zeros((n_steps * q - pairs,), jnp.int32)
            out_shape.append(jax.ShapeDtypeStruct((TOP_K, n_side + PAD_ROWS, d), F32))
        spos = jnp.concatenate([pos.reshape(-1), spare]).reshape(n_steps, 1, q)
        in_specs += [pl.BlockSpec((1, 1, q), lambda i, te, na: (i, 0, 0), memory_space=pltpu.SMEM),
                     pl.BlockSpec(memory_space=pl.ANY)]
        args += [spos, src]
        out_specs.append(pl.BlockSpec(memory_space=pl.ANY))
        scratch.append(pltpu.SemaphoreType.DMA)
    outs = pl.pallas_call(
        functools.partial(_expert_kernel, side=kind, q=q, n_side=n_side),
        out_shape=tuple(out_shape),
        grid_spec=pltpu.PrefetchScalarGridSpec(
            num_scalar_prefetch=2,
            grid=(n_steps,),
            in_specs=in_specs,
            out_specs=tuple(out_specs),
            scratch_shapes=scratch,
        ),
        compiler_params=_cparams("arbitrary"),
        name="moe_experts" if kind is None else f"moe_experts_{kind}",
    )(tile_expert, n_active, *args)
    return outs if side is not None else (outs[0], None)


def _combine_kernel(*refs, final, gathered, aliased):
    refs = list(refs)
    pos_ref = None if gathered else refs.pop(0)
    x_ref, wts_ref, gate_ref = refs[:3]
    refs = refs[3:]
    fg_ref = refs.pop(0) if final else None
    y_src = refs.pop(0)
    if aliased:
        refs.pop(0)
    o_ref = refs.pop(0)
    tm = x_ref.shape[0]
    if gathered:
        ybuf = y_src
    else:
        ybuf, sem = refs

        def issue(rb, carry):
            for u in range(ISSUE_UNROLL):
                r = rb * ISSUE_UNROLL + u
                for k in range(TOP_K):
                    _row_copy(y_src, pos_ref[0, 0, r * TOP_K + k], ybuf.at[k], r, sem).start()
            return carry

        lax.fori_loop(0, tm // ISSUE_UNROLL, issue, 0)
        for k in range(TOP_K):
            pltpu.make_async_copy(y_src.at[pl.ds(0, tm)], ybuf.at[k], sem).wait()
    wts = wts_ref[...]
    f = wts[:, 0:1] * ybuf[0]
    for k in range(1, TOP_K):
        f = f + wts[:, k:k + 1] * ybuf[k]
    out = x_ref[...] + gate_ref[...] * f
    if final:
        out = out * lax.rsqrt(jnp.mean(out * out, axis=-1, keepdims=True) + NORM_EPS) * fg_ref[...]
    o_ref[...] = out


def _combine(x, start, n, g_start, pos, wts, mods, row_fn, y, gathered, final_g, out_prev, *, name):
    d = x.shape[1]
    tm = ROW_TILE
    t0, gt0 = start // tm, g_start // tm
    in_specs, args = [], []
    if not gathered:
        in_specs.append(pl.BlockSpec((1, 1, tm * TOP_K), lambda i: (i + gt0, 0, 0), memory_space=pltpu.SMEM))
        args.append(pos.reshape(pos.shape[0] // tm, 1, tm * TOP_K))
    in_specs += [
        pl.BlockSpec((tm, d), lambda i: (i + t0, 0)),
        pl.BlockSpec((tm, TOP_K), lambda i: (i + gt0, 0)),
        _mod_spec(lambda i: row_fn(i + t0), 5, d),
    ]
    args += [x, wts, mods]
    if final_g is not None:
        in_specs.append(pl.BlockSpec((1, d), lambda i: (0, 0)))
        args.append(final_g.reshape(1, d))
    if gathered:
        in_specs.append(pl.BlockSpec((TOP_K, tm, d), lambda i: (0, i + gt0, 0)))
        scratch = []
    else:
        in_specs.append(pl.BlockSpec(memory_space=pl.ANY))
        scratch = [pltpu.VMEM((TOP_K, tm, d), F32), pltpu.SemaphoreType.DMA]
    args.append(y)
    aliases = {}
    if out_prev is not None:
        in_specs.append(pl.BlockSpec(memory_space=pl.ANY))
        args.append(out_prev)
        aliases = {len(args) - 1: 0}
    return pl.pallas_call(
        functools.partial(_combine_kernel, final=final_g is not None, gathered=gathered,
                          aliased=out_prev is not None),
        out_shape=jax.ShapeDtypeStruct(x.shape, F32),
        grid=(n // tm,),
        in_specs=in_specs,
        out_specs=pl.BlockSpec((tm, d), lambda i: (i + t0, 0)),
        scratch_shapes=scratch,
        input_output_aliases=aliases,
        compiler_params=_cparams("arbitrary"),
        name=name,
    )(*args)


def _route_group(xs_by_name, streams, mods, norm_g, w_rt, b_col, tri, tag):
    tm = EXPERT_TILE
    n_group = sum(s[2] for s in streams)
    cnt = jnp.zeros((N_EXPERTS, LANE), F32)
    h, infos, g_start = None, [], 0
    for s, (name, start, n, row_fn) in enumerate(streams):
        h, info, cnt = _router(xs_by_name[name], start, n, norm_g, mods, row_fn, w_rt, b_col, tri, cnt,
                               h, n_group, g_start, name=f"moe_router_{tag}{s}")
        infos.append(info)
        g_start += n
    info = jnp.concatenate(infos, axis=1) if len(infos) > 1 else infos[0]
    counts = cnt[:, 0].astype(jnp.int32)
    padded = ((counts + tm - 1) // tm) * tm
    ends = jnp.cumsum(padded)
    offsets = ends - padded
    n_slots = n_group * TOP_K + N_EXPERTS * tm
    n_active = (ends[-1] // tm).astype(jnp.int32).reshape(1)
    tile_start = jnp.minimum(jnp.arange(n_slots // tm, dtype=jnp.int32), n_active[0] - 1) * tm
    tile_expert = jnp.sum((ends[None, :] <= tile_start[:, None]).astype(jnp.int32), axis=1)
    tile_expert = jnp.minimum(tile_expert, N_EXPERTS - 1)
    eid = info[INFO_ID:INFO_ID + TOP_K].astype(jnp.int32)
    rank = info[INFO_RANK:INFO_RANK + TOP_K].astype(jnp.int32)
    base = jnp.zeros_like(eid)
    for e in range(N_EXPERTS):
        base = jnp.where(eid == e, offsets[e], base)
    return dict(
        n=n_group, h=h, n_slots=n_slots, n_active=n_active, tile_expert=tile_expert,
        pos=(base + rank).T,
        wts=info[INFO_W:INFO_W + TOP_K].T,
        pad_lo=(offsets + counts).astype(jnp.int32), pad_hi=ends.astype(jnp.int32),
    )


def _moe(xs_by_name, groups, layer, mods, norm_g, w_r, b_r, w_gu, b_gu, w_down, b_down, final_g):
    assert len(groups) in (1, 2)
    w_rt = w_r.T
    b_col = b_r.reshape(N_EXPERTS, 1)
    tok = np.arange(ROW_TILE)
    tri = jnp.asarray(tok[:, None] < tok[None, :], BF16)
    routed = [_route_group(xs_by_name, streams, mods, norm_g, w_rt, b_col, tri, "ab"[g])
              for g, streams in enumerate(groups)]
    weights = (layer, w_gu, b_gu, w_down, b_down)

    first = routed[0]
    slots = _dispatch(first["pos"], first["h"], None, first["n_slots"], name="moe_dispatch")
    slots = _padfill(slots, first["pad_lo"], first["pad_hi"])
    if len(groups) == 1:
        y, _ = _experts(slots, first["n_slots"], first["tile_expert"], first["n_active"], *weights)
        ys = [(y, False)]
    else:
        second = routed[1]
        y0, slots1 = _experts(slots, first["n_slots"], first["tile_expert"], first["n_active"], *weights,
                              side=("scatter", second["pos"], second["h"], second["n"], second["n_slots"]))
        slots1 = _padfill(slots1, second["pad_lo"], second["pad_hi"])
        y1, y0_by_choice = _experts(slots1, second["n_slots"], second["tile_expert"], second["n_active"],
                                    *weights, side=("gather", first["pos"], y0, first["n"], 0))
        ys = [(y0_by_choice, True), (y1, False)]

    outs = {}
    for g, (streams, r, (y, gathered)) in enumerate(zip(groups, routed, ys)):
        g_start = 0
        for s, (name, start, n, row_fn) in enumerate(streams):
            outs[name] = _combine(xs_by_name[name], start, n, g_start, r["pos"], r["wts"], mods, row_fn, y,
                                  gathered, final_g, outs.get(name), name=f"moe_combine_{'ab'[g]}{s}")
            g_start += n
    return outs


def _pad_heads(w):
    lead = w.shape[:-1]
    w4 = w.reshape(*lead, GLA_HEADS, GLA_DK)
    w4 = jnp.pad(w4, [(0, 0)] * len(lead) + [(0, 0), (0, HEAD_PAD - GLA_DK)])
    return w4.reshape(*lead, QK_PAD)


def _pack_gla_in(w_in):
    d = w_in.shape[0]
    lr = jnp.pad(w_in[:, OFF_AF:OFF_AB + GLA_LOWRANK], ((0, 0), (0, LANE - 2 * GLA_LOWRANK)))
    return jnp.concatenate([
        _pad_heads(w_in[:, OFF_Q:OFF_Q + GLA_QK]),
        _pad_heads(w_in[:, OFF_K:OFF_K + GLA_QK]),
        w_in[:, OFF_V:OFF_V + GLA_V],
        w_in[:, OFF_G:OFF_G + GLA_V],
        w_in[:, OFF_GLU:OFF_GLU + 2 * CONV_CH],
        lr,
    ], axis=1).astype(BF16)


def _pack_decay(wa_f, ba_f, wa_b, ba_b):
    wa = jnp.zeros((LANE, 2 * QK_PAD), F32)
    wa = wa.at[0:GLA_LOWRANK, 0:QK_PAD].set(_pad_heads(wa_f))
    wa = wa.at[GLA_LOWRANK:2 * GLA_LOWRANK, QK_PAD:].set(_pad_heads(wa_b))
    ba = jnp.concatenate([_pad_heads(ba_f), _pad_heads(ba_b)]).reshape(1, 2 * QK_PAD)
    return wa.astype(BF16), ba


def _rope_tables(t_len):
    t = jnp.arange(t_len)
    row_pos = (t // GRID_W).astype(F32)
    col_pos = (t % GRID_W).astype(F32)
    half = GLA_DK // 4
    inv_freq = ROPE_BASE ** (-jnp.arange(half, dtype=F32) / half)
    dim = np.arange(HEAD_PAD)
    real = dim < GLA_DK
    use_col = (dim % GLA_DK) >= GLA_DK // 2
    first = (dim % (GLA_DK // 2)) < half
    pos = jnp.where(jnp.asarray(use_col)[None, :], col_pos[:, None], row_pos[:, None])
    ang = pos * inv_freq[dim % half][None, :]
    cos = jnp.where(jnp.asarray(real)[None, :], jnp.cos(ang), 0.0)
    sin = jnp.where(jnp.asarray(real)[None, :], jnp.sin(ang), 0.0)
    sin = jnp.where(jnp.asarray(first)[None, :], -sin, sin)
    return jnp.tile(cos, (1, GLA_HEADS)), jnp.tile(sin, (1, GLA_HEADS))


def kernel(x, c, ctx, c_ctx, ada_w, ada_b, norm1_g, norm2_g, gla_conv_w_in, gla_wa_fwd, gla_ba_fwd,
           gla_wa_bwd, gla_ba_bwd, gla_norm_g, conv_dw_w, conv_dw_b, conv_ln_g, conv_ln_b,
           gla_conv_w_out, na_w_qkv, na_rpb, na_w_out, router_w, router_b, expert_w_gu, expert_b_gu,
           expert_w_down, expert_b_down, final_norm_g):
    b, t_len, d = x.shape
    l_ctx = ctx.shape[1]
    depth = ada_w.shape[0]
    assert b + 1 <= MOD_ROWS and t_len % 512 == 0 and l_ctx % ROW_TILE == 0

    c_all = jnp.concatenate([c, c_ctx[None, :], jnp.zeros((MOD_ROWS - b - 1, d), F32)], axis=0)
    mods_all = _mods(c_all, ada_w, ada_b)

    x_lat = x.reshape(b * t_len, d)
    x_ctx = ctx.reshape(b * l_ctx, d)
    tm_lat = 512
    tm_ctx = min(512, l_ctx)

    def lat_row(i, tm=tm_lat):
        return (i * tm) // t_len

    def ctx_row(i):
        return b

    def lat_row_moe(i):
        return (i * ROW_TILE) // t_len

    for layer in range(depth):
        last = layer == depth - 1
        j = layer // 2
        mods = mods_all[layer]
        if layer % 2 == 0:
            w_pack = _pack_gla_in(gla_conv_w_in[j])
            wa, ba = _pack_decay(gla_wa_fwd[j], gla_ba_fwd[j], gla_wa_bwd[j], gla_ba_bwd[j])
            w_out = gla_conv_w_out[j].astype(BF16)
            zero_state = jnp.zeros((b, GLA_HEADS, GLA_DV, HEAD_PAD), F32)
            p_c = _norm_mod_matmul(x_ctx, norm1_g[layer], mods, ctx_row, (1, 0), w_pack, tm=tm_ctx,
                                   gla_layout=True, name="gla_in_ctx")
            y_gla_c, s_f, s_b = _gla(p_c, l_ctx, wa, ba, gla_norm_g[j], zero_state, zero_state)
            p_l = _norm_mod_matmul(x_lat, norm1_g[layer], mods, lat_row, (1, 0), w_pack, tm=tm_lat,
                                   gla_layout=True, rope=_rope_tables(t_len), name="gla_in_lat")
            y_gla_l, _, _ = _gla(p_l, t_len, wa, ba, gla_norm_g[j], s_f, s_b)
            y_conv_l = _conv(p_l, t_len, conv_dw_w[j], conv_dw_b[j], conv_ln_g[j], conv_ln_b[j])
            x_lat = _matmul_residual([y_gla_l, y_conv_l], w_out, x_lat, mods, lat_row, 2, tm=tm_lat,
                                     name="mix_out_lat")
            if not last:
                y_conv_c = _conv(p_c, l_ctx, conv_dw_w[j], conv_dw_b[j], conv_ln_g[j], conv_ln_b[j])
                x_ctx = _matmul_residual([y_gla_c, y_conv_c], w_out, x_ctx, mods, ctx_row, 2, tm=tm_ctx,
                                         name="mix_out_ctx")
        else:
            w_qkv = na_w_qkv[j].astype(BF16)
            kv_c = _norm_mod_matmul(x_ctx, norm1_g[layer], mods, ctx_row, (1, 0), w_qkv[:, NA_WIDTH:],
                                    tm=tm_ctx, name="na_kv_ctx")
            qkv = _norm_mod_matmul(x_lat, norm1_g[layer], mods, lat_row, (1, 0), w_qkv, tm=tm_lat,
                                   name="na_qkv_lat")
            o_l = _neighbourhood_attention(qkv, kv_c, na_rpb[j], t_len, l_ctx)
            x_lat = _matmul_residual([o_l], na_w_out[j].astype(BF16), x_lat, mods, lat_row, 2, tm=tm_lat,
                                     name="na_out_lat")
            if not last:
                raise NotImplementedError("context output of an attention layer is only needed mid-stack")
        n_lat = b * t_len
        half = (n_lat // 2 // EXPERT_TILE) * EXPERT_TILE
        streams = {"lat": x_lat}
        groups = [[("lat", 0, half, lat_row_moe)], [("lat", half, n_lat - half, lat_row_moe)]]
        if not last:
            streams["ctx"] = x_ctx
            groups[1].append(("ctx", 0, b * l_ctx, ctx_row))
        outs = _moe(streams, groups, layer, mods, norm2_g[layer], router_w[layer], router_b[layer],
                    expert_w_gu, expert_b_gu, expert_w_down, expert_b_down, final_norm_g if last else None)
        x_lat = outs["lat"]
        if not last:
            x_ctx = outs["ctx"]
    return x_lat.reshape(b, t_len, d)
```

```python
import functools

import numpy as np
import jax
import jax.numpy as jnp
from jax import lax
from jax.experimental import pallas as pl
from jax.experimental.pallas import tpu as pltpu

F32 = jnp.float32
BF16 = jnp.bfloat16
HIGHEST = lax.Precision.HIGHEST

NORM_EPS = 1e-6
ROPE_BASE = 10000.0
GRID_W = 64

GLA_HEADS = 4
GLA_DK = 64
GLA_DV = 128
GLA_LOWRANK = 16
GLA_TEMP = 16.0
GLA_CHUNK = 64
GLA_QK = GLA_HEADS * GLA_DK
GLA_V = GLA_HEADS * GLA_DV
CONV_CH = 512
CONV_WIDTH = 31
NA_HEADS = 16
NA_HEAD_DIM = 64
NA_WIDTH = NA_HEADS * NA_HEAD_DIM
NA_WIN_R = 8
NA_WIN_C = 16
N_EXPERTS = 32
TOP_K = 4
D_EXPERT = 1024
SWIGLU_LIMIT = 7.0
SWIGLU_ALPHA = 1.702

OFF_Q = 0
OFF_G = OFF_Q + GLA_QK
OFF_GLU = OFF_G + GLA_V
OFF_K = OFF_GLU + 2 * CONV_CH
OFF_V = OFF_K + GLA_QK
OFF_AF = OFF_V + GLA_V
OFF_AB = OFF_AF + GLA_LOWRANK

LANE = 128
VMEM_LIMIT = 56 * 1024 * 1024

HEAD_PAD = LANE
QK_PAD = GLA_HEADS * HEAD_PAD
PK_Q, PK_K, PK_V, PK_G, PK_A, PK_GT, PK_LR = 0, 512, 1024, 1536, 2048, 2560, 3072
PK_WIDTH = PK_LR + LANE

MOD_ROWS = 40
NEG_BIG = -1e30
ROW_TILE = 256
EXPERT_TILE = 512
ISSUE_UNROLL = 8
EXPERT_SPLIT = 4


def _cparams(*sem):
    return pltpu.CompilerParams(dimension_semantics=sem, vmem_limit_bytes=VMEM_LIMIT)


def _norm_mod(x, g, scale, shift):
    y = x * lax.rsqrt(jnp.mean(x * x, axis=-1, keepdims=True) + NORM_EPS) * g
    return y * (1.0 + scale) + shift


def _sigmoid(x):
    return 1.0 / (1.0 + jnp.exp(-x))


def _mods_kernel(c_ref, w_ref, b_ref, o_ref):
    c = c_ref[...]
    s = c * _sigmoid(c)
    o_ref[...] = jnp.dot(s, w_ref[...], precision=HIGHEST, preferred_element_type=F32) + b_ref[...]


def _mods(c_all, ada_w, ada_b):
    depth, d, n6 = ada_w.shape
    nb = 512
    out = pl.pallas_call(
        _mods_kernel,
        out_shape=jax.ShapeDtypeStruct((depth, MOD_ROWS, n6), F32),
        grid=(depth, n6 // nb),
        in_specs=[
            pl.BlockSpec((MOD_ROWS, d), lambda l, j: (0, 0)),
            pl.BlockSpec((None, d, nb), lambda l, j: (l, 0, j)),
            pl.BlockSpec((None, 1, nb), lambda l, j: (l, 0, j)),
        ],
        out_specs=pl.BlockSpec((None, MOD_ROWS, nb), lambda l, j: (l, 0, j)),
        compiler_params=_cparams("arbitrary", "arbitrary"),
        name="adaln_mods",
    )(c_all, ada_w, ada_b.reshape(depth, 1, n6))
    return out.reshape(depth, MOD_ROWS * 6, 1, d)


def _mod_spec(row_fn, chunk, d):
    return pl.BlockSpec((None, 1, d), lambda i, *_: (row_fn(i) * 6 + chunk, 0, 0))


def _nmm_kernel(x_ref, g_ref, sc_ref, sh_ref, w_ref, *rest, gla_layout, rope):
    if rope:
        cos_ref, sin_ref, o_ref = rest
    else:
        (o_ref,) = rest
    h = _norm_mod(x_ref[...], g_ref[...], sc_ref[...], sh_ref[...]).astype(BF16)
    nout = o_ref.shape[1]
    chunk = 512
    for j0 in range(0, nout, chunk):
        j1 = min(j0 + chunk, nout)
        acc = jnp.dot(h, w_ref[:, j0:j1], preferred_element_type=F32)
        if gla_layout and j0 in (PK_Q, PK_K):
            if rope:
                lane = lax.broadcasted_iota(jnp.int32, acc.shape, 1)
                first = (lane % 32) < 16
                rot = jnp.where(first, pltpu.roll(acc, QK_PAD - 16, 1), pltpu.roll(acc, 16, 1))
                acc = acc * cos_ref[...] + rot * sin_ref[...]
            if j0 == PK_Q:
                acc = acc * (GLA_DK ** -0.5)
        o_ref[:, j0:j1] = acc.astype(o_ref.dtype)


def _norm_mod_matmul(x, g, mods, row_fn, chunks, w, *, tm, gla_layout=False, rope=None, name):
    n, d = x.shape
    nout = w.shape[1]
    in_specs = [
        pl.BlockSpec((tm, d), lambda i: (i, 0)),
        pl.BlockSpec((1, d), lambda i: (0, 0)),
        _mod_spec(row_fn, chunks[0], d),
        _mod_spec(row_fn, chunks[1], d),
        pl.BlockSpec((d, nout), lambda i: (0, 0)),
    ]
    args = [x, g.reshape(1, d), mods, mods, w]
    if rope is not None:
        cos_t, sin_t = rope
        t_tiles = cos_t.shape[0] // tm
        in_specs += [pl.BlockSpec((tm, QK_PAD), lambda i: (i % t_tiles, 0))] * 2
        args += [cos_t, sin_t]
    return pl.pallas_call(
        functools.partial(_nmm_kernel, gla_layout=gla_layout, rope=rope is not None),
        out_shape=jax.ShapeDtypeStruct((n, nout), BF16),
        grid=(n // tm,),
        in_specs=in_specs,
        out_specs=pl.BlockSpec((tm, nout), lambda i: (i, 0)),
        compiler_params=_cparams("arbitrary"),
        name=name,
    )(*args)


def _mmres_kernel(*refs, n_parts):
    a_refs = refs[:n_parts]
    w_ref, x_ref, gate_ref, o_ref = refs[n_parts:]
    acc = None
    k0 = 0
    for a_ref in a_refs:
        kk = a_ref.shape[1]
        t = jnp.dot(a_ref[...], w_ref[k0:k0 + kk, :], preferred_element_type=F32)
        acc = t if acc is None else acc + t
        k0 += kk
    o_ref[...] = x_ref[...] + gate_ref[...] * acc


def _matmul_residual(parts, w, x, mods, row_fn, gate_chunk, *, tm, name):
    n, d = x.shape
    in_specs = [pl.BlockSpec((tm, a.shape[1]), lambda i: (i, 0)) for a in parts]
    in_specs += [
        pl.BlockSpec(w.shape, lambda i: (0, 0)),
        pl.BlockSpec((tm, d), lambda i: (i, 0)),
        _mod_spec(row_fn, gate_chunk, d),
    ]
    return pl.pallas_call(
        functools.partial(_mmres_kernel, n_parts=len(parts)),
        out_shape=jax.ShapeDtypeStruct((n, d), F32),
        grid=(n // tm,),
        in_specs=in_specs,
        out_specs=pl.BlockSpec((tm, d), lambda i: (i, 0)),
        compiler_params=_cparams("arbitrary"),
        name=name,
    )(*parts, w, x, mods)


def _gla_kernel(q_ref, k_ref, v_ref, g_ref, a_ref, wa_ref, ba_ref, ng_ref, s0f_ref, s0b_ref,
                y_ref, sf_ref, sb_ref, of_ref, ob_ref):
    t_len = q_ref.shape[0]
    n_chunks = t_len // GLA_CHUNK
    c = GLA_CHUNK
    sf_ref[...] = s0f_ref[...]
    sb_ref[...] = s0b_ref[...]

    row = lax.broadcasted_iota(jnp.int32, (c, c), 0)
    col = lax.broadcasted_iota(jnp.int32, (c, c), 1)
    lane = lax.broadcasted_iota(jnp.int32, (1, QK_PAD), 1)
    real_lane = ((lane % HEAD_PAD) < GLA_DK).astype(F32)

    def dir_step(ci, forward):
        r0 = pl.multiple_of(ci * c, c)
        rows = pl.ds(r0, c)
        off = 0 if forward else QK_PAD
        z = jnp.dot(a_ref[rows, :], wa_ref[:, off:off + QK_PAD], preferred_element_type=F32)
        z = z + ba_ref[:, off:off + QK_PAD]
        log_sig = jnp.minimum(z, 0.0) - jnp.log1p(jnp.exp(-jnp.abs(z)))
        la = log_sig * (1.0 / GLA_TEMP) * real_lane
        keep = (row >= col) if forward else (row <= col)
        bc = jnp.dot(keep.astype(F32), la, precision=HIGHEST, preferred_element_type=F32)
        b_last = bc[c - 1:c, :] if forward else bc[0:1, :]
        q = q_ref[rows, :].astype(F32)
        k = k_ref[rows, :].astype(F32)
        v = v_ref[rows, :]
        qt = (q * jnp.exp(bc)).astype(BF16)
        kt = (k * jnp.exp(-bc)).astype(BF16)
        kd = (k * jnp.exp(b_last - bc)).astype(BF16)
        decay = jnp.exp(b_last)
        s_ref = sf_ref if forward else sb_ref
        o_ref = of_ref if forward else ob_ref
        for h in range(GLA_HEADS):
            sl = slice(h * HEAD_PAD, (h + 1) * HEAD_PAD)
            att = lax.dot_general(qt[:, sl], kt[:, sl], (((1,), (1,)), ((), ())),
                                  preferred_element_type=F32)
            att = jnp.where(keep, att, 0.0).astype(BF16)
            st = s_ref[0, h]
            o = jnp.dot(att, v[:, sl], preferred_element_type=F32)
            o = o + lax.dot_general(qt[:, sl], st.astype(BF16), (((1,), (1,)), ((), ())),
                                    preferred_element_type=F32)
            kv_t = lax.dot_general(v[:, sl], kd[:, sl], (((0,), (0,)), ((), ())),
                                   preferred_element_type=F32)
            s_ref[0, h] = st * decay[:, sl] + kv_t
            o_ref[rows, sl] = o

    def step(i, carry):
        dir_step(i, True)
        dir_step(n_chunks - 1 - i, False)
        return carry

    lax.fori_loop(0, n_chunks, step, 0)

    blk = min(256, t_len)

    def post(i, carry):
        rows = pl.ds(pl.multiple_of(i * blk, blk), blk)
        o = of_ref[rows, :] + ob_ref[rows, :]
        g = g_ref[rows, :].astype(F32)
        gate = g * _sigmoid(g)
        for h in range(GLA_HEADS):
            sl = slice(h * GLA_DV, (h + 1) * GLA_DV)
            oh = o[:, sl]
            oh = oh * lax.rsqrt(jnp.mean(oh * oh, axis=-1, keepdims=True) + NORM_EPS)
            y_ref[rows, sl] = (oh * ng_ref[:, sl] * gate[:, sl]).astype(y_ref.dtype)
        return carry

    lax.fori_loop(0, t_len // blk, post, 0)


def _gla(p, t_len, wa, ba, norm_g, s0f, s0b):
    n = p.shape[0]
    b = n // t_len

    def col(width, start):
        return pl.BlockSpec((t_len, width), lambda i: (i, start // width))

    st_spec = pl.BlockSpec((1, GLA_HEADS, GLA_DV, HEAD_PAD), lambda i: (i, 0, 0, 0))
    st_shape = jax.ShapeDtypeStruct((b, GLA_HEADS, GLA_DV, HEAD_PAD), F32)
    return pl.pallas_call(
        _gla_kernel,
        out_shape=(jax.ShapeDtypeStruct((n, GLA_V), BF16), st_shape, st_shape),
        grid=(b,),
        in_specs=[
            col(QK_PAD, PK_Q), col(QK_PAD, PK_K), col(GLA_V, PK_V), col(GLA_V, PK_G), col(LANE, PK_LR),
            pl.BlockSpec(wa.shape, lambda i: (0, 0)),
            pl.BlockSpec(ba.shape, lambda i: (0, 0)),
            pl.BlockSpec((1, GLA_V), lambda i: (0, 0)),
            st_spec, st_spec,
        ],
        out_specs=(pl.BlockSpec((t_len, GLA_V), lambda i: (i, 0)), st_spec, st_spec),
        scratch_shapes=[pltpu.VMEM((t_len, GLA_V), F32), pltpu.VMEM((t_len, GLA_V), F32)],
        compiler_params=_cparams("arbitrary"),
        name="gla_scan",
    )(p, p, p, p, p, wa, ba, norm_g.reshape(1, GLA_V), s0f, s0b)


CONV_HALO = 16
CONV_ROWS = 64


def _conv_kernel(a_ref, gt_ref, w_ref, cb_ref, lg_ref, lb_ref, y_ref, u_ref):
    t_len = a_ref.shape[0]
    zeros = jnp.zeros((CONV_HALO, CONV_CH), F32)
    u_ref[0:CONV_HALO, :] = zeros
    u_ref[CONV_HALO + t_len:2 * CONV_HALO + t_len, :] = zeros
    blk = min(256, t_len)

    def glu(i, carry):
        r0 = pl.multiple_of(i * blk, blk)
        a = a_ref[pl.ds(r0, blk), :].astype(F32)
        gt = gt_ref[pl.ds(r0, blk), :].astype(F32)
        u_ref[pl.ds(r0 + CONV_HALO, blk), :] = a * _sigmoid(gt)
        return carry

    lax.fori_loop(0, t_len // blk, glu, 0)
    shift = CONV_HALO - CONV_WIDTH // 2
    win_rows = CONV_ROWS + 2 * CONV_HALO

    def tile(i, carry):
        r0 = pl.multiple_of(i * CONV_ROWS, CONV_ROWS)
        parts = []
        for lb in range(CONV_CH // LANE):
            ls = slice(lb * LANE, (lb + 1) * LANE)
            win = u_ref[pl.ds(r0, win_rows), ls]
            acc = jnp.zeros((CONV_ROWS, LANE), F32)
            for b in range(8):
                wb = win if b == 0 else pltpu.roll(win, win_rows - b, 0)
                for a in range(win_rows // 8):
                    j = 8 * a + b - shift
                    if 0 <= j < CONV_WIDTH:
                        acc = acc + wb[8 * a:8 * a + CONV_ROWS, :] * w_ref[j:j + 1, ls]
            parts.append(acc)
        y = jnp.concatenate(parts, axis=1) + cb_ref[...]
        mu = jnp.mean(y, axis=-1, keepdims=True)
        yc = y - mu
        var = jnp.mean(yc * yc, axis=-1, keepdims=True)
        yn = yc * lax.rsqrt(var + NORM_EPS) * lg_ref[...] + lb_ref[...]
        y_ref[pl.ds(r0, CONV_ROWS), :] = (yn * _sigmoid(yn)).astype(y_ref.dtype)
        return carry

    lax.fori_loop(0, t_len // CONV_ROWS, tile, 0)


def _conv(p, t_len, conv_w, conv_b, ln_g, ln_b):
    n = p.shape[0]
    w_pad = jnp.pad(conv_w, ((0, 32 - CONV_WIDTH), (0, 0)))
    vec = pl.BlockSpec((1, CONV_CH), lambda i: (0, 0))
    return pl.pallas_call(
        _conv_kernel,
        out_shape=jax.ShapeDtypeStruct((n, CONV_CH), BF16),
        grid=(n // t_len,),
        in_specs=[
            pl.BlockSpec((t_len, CONV_CH), lambda i: (i, PK_A // CONV_CH)),
            pl.BlockSpec((t_len, CONV_CH), lambda i: (i, PK_GT // CONV_CH)),
            pl.BlockSpec((32, CONV_CH), lambda i: (0, 0)),
            vec, vec, vec,
        ],
        out_specs=pl.BlockSpec((t_len, CONV_CH), lambda i: (i, 0)),
        scratch_shapes=[pltpu.VMEM((t_len + 2 * CONV_HALO, CONV_CH), F32)],
        compiler_params=_cparams("arbitrary"),
        name="conformer_conv",
    )(p, p, w_pad, conv_b.reshape(1, -1), ln_g.reshape(1, -1), ln_b.reshape(1, -1))


NA_QROWS = 4
NA_BAND = 12


def _na_geometry(rows):
    assert rows >= NA_BAND and rows % NA_QROWS == 0
    win_r = min(NA_WIN_R, rows)
    starts, classes, sigs = [], [], []
    for rb in range(rows // NA_QROWS):
        bs = int(np.clip(rb * NA_QROWS - NA_WIN_R // 2, 0, rows - NA_BAND))
        sig = tuple((rb * NA_QROWS + i - bs,
                     int(np.clip(rb * NA_QROWS + i - win_r // 2, 0, rows - win_r)) - bs)
                    for i in range(NA_QROWS))
        if sig not in sigs:
            sigs.append(sig)
        starts.append(bs)
        classes.append(sigs.index(sig))
    return starts, classes, sigs, win_r


def _na_bias(rpb, sigs, win_r):
    w = GRID_W
    cols = np.arange(w)
    c_start = np.clip(cols - NA_WIN_C // 2, 0, w - NA_WIN_C)
    col_ok = (cols[None, :] >= c_start[:, None]) & (cols[None, :] < c_start[:, None] + NA_WIN_C)
    dc = np.clip(cols[None, :] - cols[:, None] + NA_WIN_C - 1, 0, 2 * NA_WIN_C - 2)
    rpb_col = jnp.take(rpb.astype(F32), jnp.asarray(dc.reshape(-1)), axis=2)
    rpb_col = rpb_col.reshape(NA_HEADS, 2 * NA_WIN_R - 1, w, w)
    rpb_col = jnp.where(jnp.asarray(col_ok)[None, None], rpb_col, NEG_BIG)
    kj = np.arange(NA_BAND)
    out = []
    for sig in sigs:
        q_rel = np.array([s[0] for s in sig])
        r_rel = np.array([s[1] for s in sig])
        row_ok = (kj[None, :] >= r_rel[:, None]) & (kj[None, :] < r_rel[:, None] + win_r)
        dr = np.clip(kj[None, :] - q_rel[:, None] + NA_WIN_R - 1, 0, 2 * NA_WIN_R - 2)
        bias = jnp.take(rpb_col, jnp.asarray(dr.reshape(-1)), axis=1)
        bias = bias.reshape(NA_HEADS, NA_QROWS, NA_BAND, w, w)
        bias = jnp.where(jnp.asarray(row_ok)[None, :, :, None, None], bias, NEG_BIG)
        out.append(bias.transpose(0, 1, 3, 2, 4).reshape(NA_HEADS, NA_QROWS * w, NA_BAND * w))
    return jnp.stack(out, axis=0)


def _na_kernel(q_ref, k_ref, v_ref, kc_ref, vc_ref, bias_ref, o_ref, *, starts, classes):
    w = GRID_W
    scale = NA_HEAD_DIM ** -0.5
    nq = NA_QROWS * w
    lane = lax.broadcasted_iota(jnp.int32, (nq, LANE), 1)
    kc = kc_ref[...]
    vc = vc_ref[...]
    nt = (((1,), (1,)), ((), ()))
    for rb, (bs, cls) in enumerate(zip(starts, classes)):
        q2 = q_ref[rb * nq:(rb + 1) * nq, :]
        kb = k_ref[bs * w:(bs + NA_BAND) * w, :]
        vb = v_ref[bs * w:(bs + NA_BAND) * w, :]
        acc = None
        for hh in range(2):
            mine = (lane >= hh * NA_HEAD_DIM) & (lane < (hh + 1) * NA_HEAD_DIM)
            qh = jnp.where(mine, q2, jnp.zeros_like(q2)) * scale
            s_loc = lax.dot_general(qh, kb, nt, preferred_element_type=F32) + bias_ref[cls, hh]
            s_ctx = lax.dot_general(qh, kc, nt, preferred_element_type=F32)
            m = jnp.maximum(jnp.max(s_loc, axis=-1, keepdims=True), jnp.max(s_ctx, axis=-1, keepdims=True))
            p_loc = jnp.exp(s_loc - m)
            p_ctx = jnp.exp(s_ctx - m)
            den = jnp.sum(p_loc, axis=-1, keepdims=True) + jnp.sum(p_ctx, axis=-1, keepdims=True)
            o = jnp.dot(p_loc.astype(BF16), vb, preferred_element_type=F32)
            o = o + jnp.dot(p_ctx.astype(BF16), vc, preferred_element_type=F32)
            o = o / den
            acc = o if acc is None else jnp.where(mine, o, acc)
        o_ref[rb * nq:(rb + 1) * nq, :] = acc.astype(o_ref.dtype)


def _neighbourhood_attention(qkv, kv_ctx, rpb, t_len, l_ctx):
    n = qkv.shape[0]
    b = n // t_len
    rows = t_len // GRID_W
    starts, classes, sigs, win_r = _na_geometry(rows)
    bias = _na_bias(rpb, sigs, win_r)
    n_pairs = NA_HEADS // 2
    nq, nk = NA_QROWS * GRID_W, NA_BAND * GRID_W
    return pl.pallas_call(
        functools.partial(_na_kernel, starts=starts, classes=classes),
        out_shape=jax.ShapeDtypeStruct((n, NA_WIDTH), BF16),
        grid=(n_pairs, b),
        in_specs=[
            pl.BlockSpec((t_len, LANE), lambda j, i: (i, j)),
            pl.BlockSpec((t_len, LANE), lambda j, i: (i, n_pairs + j)),
            pl.BlockSpec((t_len, LANE), lambda j, i: (i, 2 * n_pairs + j)),
            pl.BlockSpec((l_ctx, LANE), lambda j, i: (i, j)),
            pl.BlockSpec((l_ctx, LANE), lambda j, i: (i, n_pairs + j)),
            pl.BlockSpec((len(sigs), 2, nq, nk), lambda j, i: (0, j, 0, 0)),
        ],
        out_specs=pl.BlockSpec((t_len, LANE), lambda j, i: (i, j)),
        compiler_params=_cparams("arbitrary", "arbitrary"),
        name="neighbourhood_attention",
    )(qkv, qkv, qkv, kv_ctx, kv_ctx, bias)


INFO_ID, INFO_W, INFO_RANK = 0, TOP_K, 2 * TOP_K
INFO_ROWS = 16


def _router_kernel(x_ref, g_ref, sc_ref, sh_ref, wrt_ref, br_ref, tri_ref, cnt0_ref, *rest):
    h_ref, info_ref, cnt_ref = rest[-3:]
    tm = x_ref.shape[0]

    @pl.when(pl.program_id(0) == 0)
    def _():
        cnt_ref[...] = cnt0_ref[...]

    h = _norm_mod(x_ref[...], g_ref[...], sc_ref[...], sh_ref[...])
    h_ref[...] = h
    logits = lax.dot_general(wrt_ref[...], h, (((1,), (1,)), ((), ())), precision=HIGHEST,
                             preferred_element_type=F32) + br_ref[...]
    sub = lax.broadcasted_iota(jnp.int32, (N_EXPERTS, tm), 0)
    cur = logits
    vals, ids = [], []
    for _ in range(TOP_K):
        m = jnp.max(cur, axis=0, keepdims=True)
        idx = jnp.min(jnp.where(cur == m, sub, N_EXPERTS), axis=0, keepdims=True)
        vals.append(m)
        ids.append(idx)
        cur = jnp.where(sub == idx, NEG_BIG, cur)
    ex = [jnp.exp(v - vals[0]) for v in vals]
    den = ex[0] + ex[1] + ex[2] + ex[3]
    onehot = jnp.zeros((N_EXPERTS, tm), F32)
    for idx in ids:
        onehot = onehot + (sub == idx).astype(F32)
    before = jnp.dot(onehot.astype(BF16), tri_ref[...], preferred_element_type=F32)
    running = cnt_ref[:, 0:1]
    base = running + before
    row = lax.broadcasted_iota(jnp.int32, (INFO_ROWS, tm), 0)
    info = jnp.zeros((INFO_ROWS, tm), F32)
    for k in range(TOP_K):
        rank = jnp.sum(jnp.where(sub == ids[k], base, 0.0), axis=0, keepdims=True)
        info = info + jnp.where(row == INFO_ID + k, ids[k].astype(F32), 0.0)
        info = info + jnp.where(row == INFO_W + k, ex[k] / den, 0.0)
        info = info + jnp.where(row == INFO_RANK + k, rank, 0.0)
    info_ref[...] = info
    cnt_ref[...] = jnp.broadcast_to(running + jnp.sum(onehot, axis=1, keepdims=True), cnt_ref.shape)


def _router(x, start, n, g, mods, row_fn, w_rt, b_r, tri, cnt0, h_prev, h_rows, h_start, *, name):
    d = x.shape[1]
    tm = ROW_TILE
    t0, ht0 = start // tm, h_start // tm
    in_specs = [
        pl.BlockSpec((tm, d), lambda i: (i + t0, 0)),
        pl.BlockSpec((1, d), lambda i: (0, 0)),
        _mod_spec(lambda i: row_fn(i + t0), 4, d),
        _mod_spec(lambda i: row_fn(i + t0), 3, d),
        pl.BlockSpec((N_EXPERTS, d), lambda i: (0, 0)),
        pl.BlockSpec((N_EXPERTS, 1), lambda i: (0, 0)),
        pl.BlockSpec((tm, tm), lambda i: (0, 0)),
        pl.BlockSpec((N_EXPERTS, LANE), lambda i: (0, 0)),
    ]
    args = [x, g.reshape(1, d), mods, mods, w_rt, b_r, tri, cnt0]
    aliases = {}
    if h_prev is not None:
        in_specs.append(pl.BlockSpec(memory_space=pl.ANY))
        args.append(h_prev)
        aliases = {len(args) - 1: 0}
    return pl.pallas_call(
        _router_kernel,
        out_shape=(jax.ShapeDtypeStruct((h_rows, d), F32),
                   jax.ShapeDtypeStruct((INFO_ROWS, n), F32),
                   jax.ShapeDtypeStruct((N_EXPERTS, LANE), F32)),
        grid=(n // tm,),
        in_specs=in_specs,
        out_specs=(pl.BlockSpec((tm, d), lambda i: (i + ht0, 0)),
                   pl.BlockSpec((INFO_ROWS, tm), lambda i: (0, i)),
                   pl.BlockSpec((N_EXPERTS, LANE), lambda i: (0, 0))),
        input_output_aliases=aliases,
        compiler_params=_cparams("arbitrary"),
        name=name,
    )(*args)


def _row_copy(src_ref, src_row, dst_ref, dst_row, sem):
    return pltpu.make_async_copy(src_ref.at[pl.ds(src_row, 1)], dst_ref.at[pl.ds(dst_row, 1)], sem)


def _dispatch_kernel(pos_ref, h_ref, *rest):
    xs_ref, sem = rest[-2], rest[-1]
    tm = h_ref.shape[0]

    def issue(rb, carry):
        for u in range(ISSUE_UNROLL):
            r = rb * ISSUE_UNROLL + u
            for k in range(TOP_K):
                _row_copy(h_ref, r, xs_ref, pos_ref[0, 0, r * TOP_K + k], sem).start()
        return carry

    lax.fori_loop(0, tm // ISSUE_UNROLL, issue, 0)
    for _ in range(TOP_K):
        pltpu.make_async_copy(h_ref, xs_ref.at[pl.ds(0, tm)], sem).wait()


def _dispatch(pos, h, xs, n_slots, *, name):
    n, d = h.shape
    tm = ROW_TILE
    in_specs = [
        pl.BlockSpec((1, 1, tm * TOP_K), lambda i: (i, 0, 0), memory_space=pltpu.SMEM),
        pl.BlockSpec((tm, d), lambda i: (i, 0)),
    ]
    args = [pos.reshape(n // tm, 1, tm * TOP_K), h]
    aliases = {}
    if xs is not None:
        in_specs.append(pl.BlockSpec(memory_space=pl.ANY))
        args.append(xs)
        aliases = {2: 0}
    return pl.pallas_call(
        _dispatch_kernel,
        out_shape=jax.ShapeDtypeStruct((n_slots, d), F32),
        grid=(n // tm,),
        in_specs=in_specs,
        out_specs=pl.BlockSpec(memory_space=pl.ANY),
        scratch_shapes=[pltpu.SemaphoreType.DMA],
        input_output_aliases=aliases,
        compiler_params=_cparams("arbitrary"),
        name=name,
    )(*args)


PAD_ROWS = 8


def _padfill_kernel(lo_ref, mid_ref, hi_ref, xs_in, xs_ref, zero_ref, sem):
    del xs_in
    zero_ref[...] = jnp.zeros(zero_ref.shape, F32)
    for phase in range(2):
        for e in range(N_EXPERTS):
            def row(r, carry):
                cp = _row_copy(zero_ref, 0, xs_ref, r, sem)
                cp.start() if phase == 0 else cp.wait()
                return carry

            def block(b, carry):
                r = pl.multiple_of(mid_ref[e] + b * PAD_ROWS, PAD_ROWS)
                cp = pltpu.make_async_copy(zero_ref, xs_ref.at[pl.ds(r, PAD_ROWS)], sem)
                cp.start() if phase == 0 else cp.wait()
                return carry

            lax.fori_loop(lo_ref[e], mid_ref[e], row, 0)
            lax.fori_loop(0, (hi_ref[e] - mid_ref[e]) // PAD_ROWS, block, 0)


def _padfill(xs, lo, hi):
    n_rows, d = xs.shape
    mid = jnp.minimum(((lo + PAD_ROWS - 1) // PAD_ROWS) * PAD_ROWS, hi)
    return pl.pallas_call(
        _padfill_kernel,
        out_shape=jax.ShapeDtypeStruct((n_rows, d), F32),
        grid_spec=pltpu.PrefetchScalarGridSpec(
            num_scalar_prefetch=3,
            grid=(1,),
            in_specs=[pl.BlockSpec(memory_space=pl.ANY)],
            out_specs=pl.BlockSpec(memory_space=pl.ANY),
            scratch_shapes=[pltpu.VMEM((PAD_ROWS, d), F32), pltpu.SemaphoreType.DMA],
        ),
        input_output_aliases={3: 0},
        compiler_params=_cparams("arbitrary"),
        name="moe_padfill",
    )(lo, mid, hi, xs)


def _expert_kernel(te_ref, na_ref, xs_ref, wgu_ref, bgu_ref, wd_ref, bd_ref, *rest, side, q, n_side):
    if side is None:
        y_ref, wgu_bf, wd_bf = rest
    elif side == "scatter":
        spos_ref, src, y_ref, dst, wgu_bf, wd_bf, sem = rest
    else:
        spos_ref, src, y_ref, dst, wgu_bf, wd_bf, sem = rest
    i = pl.program_id(0)
    active = i < na_ref[0]
    del n_side

    def side_issue(part=0, parts=1):
        for j in range(part * q // parts, (part + 1) * q // parts):
            if side == "scatter":
                _row_copy(src, j // TOP_K, dst, spos_ref[0, 0, j], sem).start()
            else:
                _row_copy(src, spos_ref[0, 0, j], dst.at[j % TOP_K], j // TOP_K, sem).start()

    def side_wait():
        for k in range(TOP_K):
            if side == "scatter":
                pltpu.make_async_copy(src, dst.at[pl.ds(0, q // TOP_K)], sem).wait()
            else:
                pltpu.make_async_copy(src.at[pl.ds(0, q // TOP_K)], dst.at[k], sem).wait()

    first_of_expert = (i == 0) | (te_ref[i] != te_ref[jnp.maximum(i - 1, 0)])

    @pl.when(active & first_of_expert)
    def _():
        rows = 256
        for r in range(0, wgu_ref.shape[0], rows):
            wgu_bf[r:r + rows, :] = wgu_ref[r:r + rows, :].astype(BF16)
        for r in range(0, wd_ref.shape[0], rows):
            wd_bf[r:r + rows, :] = wd_ref[r:r + rows, :].astype(BF16)

    @pl.when(active)
    def _():
        x = xs_ref[...].astype(BF16)
        f = wd_ref.shape[0]
        half = f // EXPERT_SPLIT
        y = None
        part, parts = 0, 3 * EXPERT_SPLIT

        def issue_next():
            nonlocal part
            if side is not None:
                side_issue(part, parts)
            part += 1

        for lo in range(0, f, half):
            issue_next()
            gt = jnp.dot(x, wgu_bf[:, lo:lo + half], preferred_element_type=F32) + bgu_ref[:, lo:lo + half]
            issue_next()
            up = jnp.dot(x, wgu_bf[:, f + lo:f + lo + half], preferred_element_type=F32)
            up = up + bgu_ref[:, f + lo:f + lo + half]
            gt = jnp.minimum(gt, SWIGLU_LIMIT)
            up = jnp.clip(up, -SWIGLU_LIMIT, SWIGLU_LIMIT)
            act = (up + 1.0) * gt * _sigmoid(SWIGLU_ALPHA * gt)
            issue_next()
            t = jnp.dot(act.astype(BF16), wd_bf[lo:lo + half, :], preferred_element_type=F32)
            y = t if y is None else y + t
        y_ref[...] = y + bd_ref[...]
        if side is not None:
            side_wait()

    @pl.when(jnp.logical_not(active))
    def _():
        y_ref[...] = jnp.zeros(y_ref.shape, F32)
        if side is not None:
            side_issue()
            side_wait()


def _experts(xs, n_slots, tile_expert, n_active, layer, w_gu, b_gu, w_down, b_down, side=None):
    d = xs.shape[1]
    tm = EXPERT_TILE
    depth, n_e, _, two_f = w_gu.shape
    n_steps = n_slots // tm
    in_specs = [
        pl.BlockSpec((tm, d), lambda i, te, na: (jnp.minimum(i, na[0] - 1), 0)),
        pl.BlockSpec((None, None, d, two_f), lambda i, te, na: (layer, te[i], 0, 0)),
        pl.BlockSpec((None, None, 1, two_f), lambda i, te, na: (layer, te[i], 0, 0)),
        pl.BlockSpec((None, None, two_f // 2, d), lambda i, te, na: (layer, te[i], 0, 0)),
        pl.BlockSpec((None, None, 1, d), lambda i, te, na: (layer, te[i], 0, 0)),
    ]
    args = [xs, w_gu, b_gu.reshape(depth, n_e, 1, two_f), w_down, b_down.reshape(depth, n_e, 1, d)]
    out_shape = [jax.ShapeDtypeStruct((n_slots, d), F32)]
    out_specs = [pl.BlockSpec((tm, d), lambda i, te, na: (i, 0))]
    scratch = [pltpu.VMEM((d, two_f), BF16), pltpu.VMEM((two_f // 2, d), BF16)]
    kind, q, n_side = None, 0, 0
    if side is not None:
        kind, pos, src, n_side, dst_slots = side
        pairs = n_side * TOP_K
        per_step = PAD_ROWS * TOP_K
        q = -(-(-(-pairs // n_steps)) // per_step) * per_step
        toks = q // TOP_K
        last_block = (n_side - 1) // toks
        spos_spec = pl.BlockSpec((1, 1, q), lambda i, te, na: (i, 0, 0), memory_space=pltpu.SMEM)
        if kind == "scatter":
            spare = jnp.full((n_steps * q - pairs,), dst_slots, jnp.int32)
            in_specs += [spos_spec, pl.BlockSpec((toks, d), lambda i, te, na: (jnp.minimum(i, last_block), 0))]
            out_shape.append(jax.ShapeDtypeStruct((dst_slots + PAD_ROWS, d), F32))
            out_specs.append(pl.BlockSpec(memory_space=pl.ANY))
        else:
            spare = jnp.zeros((n_steps * q - pairs,), jnp.int32)
            in_specs += [spos_spec, pl.BlockSpec(memory_space=pl.ANY)]
            out_shape.append(jax.ShapeDtypeStruct((TOP_K, n_steps * toks, d), F32))
            out_specs.append(pl.BlockSpec((TOP_K, toks, d), lambda i, te, na: (0, i, 0)))
        spos = jnp.concatenate([pos.reshape(-1), spare]).reshape(n_steps, 1, q)
        args += [spos, src]
        scratch.append(pltpu.SemaphoreType.DMA)
    outs = pl.pallas_call(
        functools.partial(_expert_kernel, side=kind, q=q, n_side=n_side),
        out_shape=tuple(out_shape),
        grid_spec=pltpu.PrefetchScalarGridSpec(
            num_scalar_prefetch=2,
            grid=(n_steps,),
            in_specs=in_specs,
            out_specs=tuple(out_specs),
            scratch_shapes=scratch,
        ),
        compiler_params=_cparams("arbitrary"),
        name="moe_experts" if kind is None else f"moe_experts_{kind}",
    )(tile_expert, n_active, *args)
    return outs if side is not None else (outs[0], None)


def _combine_kernel(*refs, final, gathered, aliased):
    refs = list(refs)
    pos_ref = None if gathered else refs.pop(0)
    x_ref, wts_ref, gate_ref = refs[:3]
    refs = refs[3:]
    fg_ref = refs.pop(0) if final else None
    y_src = refs.pop(0)
    if aliased:
        refs.pop(0)
    o_ref = refs.pop(0)
    tm = x_ref.shape[0]
    if gathered:
        ybuf = y_src
    else:
        ybuf, sem = refs

        def issue(rb, carry):
            for u in range(ISSUE_UNROLL):
                r = rb * ISSUE_UNROLL + u
                for k in range(TOP_K):
                    _row_copy(y_src, pos_ref[0, 0, r * TOP_K + k], ybuf.at[k], r, sem).start()
            return carry

        lax.fori_loop(0, tm // ISSUE_UNROLL, issue, 0)
        for k in range(TOP_K):
            pltpu.make_async_copy(y_src.at[pl.ds(0, tm)], ybuf.at[k], sem).wait()
    wts = wts_ref[...]
    f = wts[:, 0:1] * ybuf[0]
    for k in range(1, TOP_K):
        f = f + wts[:, k:k + 1] * ybuf[k]
    out = x_ref[...] + gate_ref[...] * f
    if final:
        out = out * lax.rsqrt(jnp.mean(out * out, axis=-1, keepdims=True) + NORM_EPS) * fg_ref[...]
    o_ref[...] = out


def _combine(x, start, n, g_start, pos, wts, mods, row_fn, y, gathered, final_g, out_prev, *, name):
    d = x.shape[1]
    tm = ROW_TILE
    t0, gt0 = start // tm, g_start // tm
    in_specs, args = [], []
    if not gathered:
        in_specs.append(pl.BlockSpec((1, 1, tm * TOP_K), lambda i: (i + gt0, 0, 0), memory_space=pltpu.SMEM))
        args.append(pos.reshape(pos.shape[0] // tm, 1, tm * TOP_K))
    in_specs += [
        pl.BlockSpec((tm, d), lambda i: (i + t0, 0)),
        pl.BlockSpec((tm, TOP_K), lambda i: (i + gt0, 0)),
        _mod_spec(lambda i: row_fn(i + t0), 5, d),
    ]
    args += [x, wts, mods]
    if final_g is not None:
        in_specs.append(pl.BlockSpec((1, d), lambda i: (0, 0)))
        args.append(final_g.reshape(1, d))
    if gathered:
        in_specs.append(pl.BlockSpec((TOP_K, tm, d), lambda i: (0, i + gt0, 0)))
        scratch = []
    else:
        in_specs.append(pl.BlockSpec(memory_space=pl.ANY))
        scratch = [pltpu.VMEM((TOP_K, tm, d), F32), pltpu.SemaphoreType.DMA]
    args.append(y)
    aliases = {}
    if out_prev is not None:
        in_specs.append(pl.BlockSpec(memory_space=pl.ANY))
        args.append(out_prev)
        aliases = {len(args) - 1: 0}
    return pl.pallas_call(
        functools.partial(_combine_kernel, final=final_g is not None, gathered=gathered,
                          aliased=out_prev is not None),
        out_shape=jax.ShapeDtypeStruct(x.shape, F32),
        grid=(n // tm,),
        in_specs=in_specs,
        out_specs=pl.BlockSpec((tm, d), lambda i: (i + t0, 0)),
        scratch_shapes=scratch,
        input_output_aliases=aliases,
        compiler_params=_cparams("arbitrary"),
        name=name,
    )(*args)


def _route_group(xs_by_name, streams, mods, norm_g, w_rt, b_col, tri, tag):
    tm = EXPERT_TILE
    n_group = sum(s[2] for s in streams)
    cnt = jnp.zeros((N_EXPERTS, LANE), F32)
    h, infos, g_start = None, [], 0
    for s, (name, start, n, row_fn) in enumerate(streams):
        h, info, cnt = _router(xs_by_name[name], start, n, norm_g, mods, row_fn, w_rt, b_col, tri, cnt,
                               h, n_group, g_start, name=f"moe_router_{tag}{s}")
        infos.append(info)
        g_start += n
    info = jnp.concatenate(infos, axis=1) if len(infos) > 1 else infos[0]
    counts = cnt[:, 0].astype(jnp.int32)
    padded = ((counts + tm - 1) // tm) * tm
    ends = jnp.cumsum(padded)
    offsets = ends - padded
    n_slots = n_group * TOP_K + N_EXPERTS * tm
    n_active = (ends[-1] // tm).astype(jnp.int32).reshape(1)
    tile_start = jnp.minimum(jnp.arange(n_slots // tm, dtype=jnp.int32), n_active[0] - 1) * tm
    tile_expert = jnp.sum((ends[None, :] <= tile_start[:, None]).astype(jnp.int32), axis=1)
    tile_expert = jnp.minimum(tile_expert, N_EXPERTS - 1)
    eid = info[INFO_ID:INFO_ID + TOP_K].astype(jnp.int32)
    rank = info[INFO_RANK:INFO_RANK + TOP_K].astype(jnp.int32)
    base = jnp.zeros_like(eid)
    for e in range(N_EXPERTS):
        base = jnp.where(eid == e, offsets[e], base)
    return dict(
        n=n_group, h=h, n_slots=n_slots, n_active=n_active, tile_expert=tile_expert,
        pos=(base + rank).T,
        wts=info[INFO_W:INFO_W + TOP_K].T,
        pad_lo=(offsets + counts).astype(jnp.int32), pad_hi=ends.astype(jnp.int32),
    )


def _moe(xs_by_name, groups, layer, mods, norm_g, w_r, b_r, w_gu, b_gu, w_down, b_down, final_g):
    assert len(groups) in (1, 2)
    w_rt = w_r.T
    b_col = b_r.reshape(N_EXPERTS, 1)
    tok = np.arange(ROW_TILE)
    tri = jnp.asarray(tok[:, None] < tok[None, :], BF16)
    routed = [_route_group(xs_by_name, streams, mods, norm_g, w_rt, b_col, tri, "ab"[g])
              for g, streams in enumerate(groups)]
    weights = (layer, w_gu, b_gu, w_down, b_down)

    first = routed[0]
    slots = _dispatch(first["pos"], first["h"], None, first["n_slots"], name="moe_dispatch")
    slots = _padfill(slots, first["pad_lo"], first["pad_hi"])
    if len(groups) == 1:
        y, _ = _experts(slots, first["n_slots"], first["tile_expert"], first["n_active"], *weights)
        ys = [(y, False)]
    else:
        second = routed[1]
        y0, slots1 = _experts(slots, first["n_slots"], first["tile_expert"], first["n_active"], *weights,
                              side=("scatter", second["pos"], second["h"], second["n"], second["n_slots"]))
        slots1 = _padfill(slots1, second["pad_lo"], second["pad_hi"])
        y1, y0_by_choice = _experts(slots1, second["n_slots"], second["tile_expert"], second["n_active"],
                                    *weights, side=("gather", first["pos"], y0, first["n"], 0))
        ys = [(y0_by_choice, True), (y1, False)]

    outs = {}
    for g, (streams, r, (y, gathered)) in enumerate(zip(groups, routed, ys)):
        g_start = 0
        for s, (name, start, n, row_fn) in enumerate(streams):
            outs[name] = _combine(xs_by_name[name], start, n, g_start, r["pos"], r["wts"], mods, row_fn, y,
                                  gathered, final_g, outs.get(name), name=f"moe_combine_{'ab'[g]}{s}")
            g_start += n
    return outs


def _pad_heads(w):
    lead = w.shape[:-1]
    w4 = w.reshape(*lead, GLA_HEADS, GLA_DK)
    w4 = jnp.pad(w4, [(0, 0)] * len(lead) + [(0, 0), (0, HEAD_PAD - GLA_DK)])
    return w4.reshape(*lead, QK_PAD)


def _pack_gla_in(w_in):
    d = w_in.shape[0]
    lr = jnp.pad(w_in[:, OFF_AF:OFF_AB + GLA_LOWRANK], ((0, 0), (0, LANE - 2 * GLA_LOWRANK)))
    return jnp.concatenate([
        _pad_heads(w_in[:, OFF_Q:OFF_Q + GLA_QK]),
        _pad_heads(w_in[:, OFF_K:OFF_K + GLA_QK]),
        w_in[:, OFF_V:OFF_V + GLA_V],
        w_in[:, OFF_G:OFF_G + GLA_V],
        w_in[:, OFF_GLU:OFF_GLU + 2 * CONV_CH],
        lr,
    ], axis=1).astype(BF16)


def _pack_decay(wa_f, ba_f, wa_b, ba_b):
    wa = jnp.zeros((LANE, 2 * QK_PAD), F32)
    wa = wa.at[0:GLA_LOWRANK, 0:QK_PAD].set(_pad_heads(wa_f))
    wa = wa.at[GLA_LOWRANK:2 * GLA_LOWRANK, QK_PAD:].set(_pad_heads(wa_b))
    ba = jnp.concatenate([_pad_heads(ba_f), _pad_heads(ba_b)]).reshape(1, 2 * QK_PAD)
    return wa.astype(BF16), ba


def _rope_tables(t_len):
    t = jnp.arange(t_len)
    row_pos = (t // GRID_W).astype(F32)
    col_pos = (t % GRID_W).astype(F32)
    half = GLA_DK // 4
    inv_freq = ROPE_BASE ** (-jnp.arange(half, dtype=F32) / half)
    dim = np.arange(HEAD_PAD)
    real = dim < GLA_DK
    use_col = (dim % GLA_DK) >= GLA_DK // 2
    first = (dim % (GLA_DK // 2)) < half
    pos = jnp.where(jnp.asarray(use_col)[None, :], col_pos[:, None], row_pos[:, None])
    ang = pos * inv_freq[dim % half][None, :]
    cos = jnp.where(jnp.asarray(real)[None, :], jnp.cos(ang), 0.0)
    sin = jnp.where(jnp.asarray(real)[None, :], jnp.sin(ang), 0.0)
    sin = jnp.where(jnp.asarray(first)[None, :], -sin, sin)
    return jnp.tile(cos, (1, GLA_HEADS)), jnp.tile(sin, (1, GLA_HEADS))


def kernel(x, c, ctx, c_ctx, ada_w, ada_b, norm1_g, norm2_g, gla_conv_w_in, gla_wa_fwd, gla_ba_fwd,
           gla_wa_bwd, gla_ba_bwd, gla_norm_g, conv_dw_w, conv_dw_b, conv_ln_g, conv_ln_b,
           gla_conv_w_out, na_w_qkv, na_rpb, na_w_out, router_w, router_b, expert_w_gu, expert_b_gu,
           expert_w_down, expert_b_down, final_norm_g):
    b, t_len, d = x.shape
    l_ctx = ctx.shape[1]
    depth = ada_w.shape[0]
    assert b + 1 <= MOD_ROWS and t_len % 512 == 0 and l_ctx % ROW_TILE == 0

    c_all = jnp.concatenate([c, c_ctx[None, :], jnp.zeros((MOD_ROWS - b - 1, d), F32)], axis=0)
    mods_all = _mods(c_all, ada_w, ada_b)

    x_lat = x.reshape(b * t_len, d)
    x_ctx = ctx.reshape(b * l_ctx, d)
    tm_lat = 512
    tm_ctx = min(512, l_ctx)

    def lat_row(i, tm=tm_lat):
        return (i * tm) // t_len

    def ctx_row(i):
        return b

    def lat_row_moe(i):
        return (i * ROW_TILE) // t_len

    for layer in range(depth):
        last = layer == depth - 1
        j = layer // 2
        mods = mods_all[layer]
        if layer % 2 == 0:
            w_pack = _pack_gla_in(gla_conv_w_in[j])
            wa, ba = _pack_decay(gla_wa_fwd[j], gla_ba_fwd[j], gla_wa_bwd[j], gla_ba_bwd[j])
            w_out = gla_conv_w_out[j].astype(BF16)
            zero_state = jnp.zeros((b, GLA_HEADS, GLA_DV, HEAD_PAD), F32)
            p_c = _norm_mod_matmul(x_ctx, norm1_g[layer], mods, ctx_row, (1, 0), w_pack, tm=tm_ctx,
                                   gla_layout=True, name="gla_in_ctx")
            y_gla_c, s_f, s_b = _gla(p_c, l_ctx, wa, ba, gla_norm_g[j], zero_state, zero_state)
            p_l = _norm_mod_matmul(x_lat, norm1_g[layer], mods, lat_row, (1, 0), w_pack, tm=tm_lat,
                                   gla_layout=True, rope=_rope_tables(t_len), name="gla_in_lat")
            y_gla_l, _, _ = _gla(p_l, t_len, wa, ba, gla_norm_g[j], s_f, s_b)
            y_conv_l = _conv(p_l, t_len, conv_dw_w[j], conv_dw_b[j], conv_ln_g[j], conv_ln_b[j])
            x_lat = _matmul_residual([y_gla_l, y_conv_l], w_out, x_lat, mods, lat_row, 2, tm=tm_lat,
                                     name="mix_out_lat")
            if not last:
                y_conv_c = _conv(p_c, l_ctx, conv_dw_w[j], conv_dw_b[j], conv_ln_g[j], conv_ln_b[j])
                x_ctx = _matmul_residual([y_gla_c, y_conv_c], w_out, x_ctx, mods, ctx_row, 2, tm=tm_ctx,
                                         name="mix_out_ctx")
        else:
            w_qkv = na_w_qkv[j].astype(BF16)
            kv_c = _norm_mod_matmul(x_ctx, norm1_g[layer], mods, ctx_row, (1, 0), w_qkv[:, NA_WIDTH:],
                                    tm=tm_ctx, name="na_kv_ctx")
            qkv = _norm_mod_matmul(x_lat, norm1_g[layer], mods, lat_row, (1, 0), w_qkv, tm=tm_lat,
                                   name="na_qkv_lat")
            o_l = _neighbourhood_attention(qkv, kv_c, na_rpb[j], t_len, l_ctx)
            x_lat = _matmul_residual([o_l], na_w_out[j].astype(BF16), x_lat, mods, lat_row, 2, tm=tm_lat,
                                     name="na_out_lat")
            if not last:
                raise NotImplementedError("context output of an attention layer is only needed mid-stack")
        n_lat = b * t_len
        half = (n_lat // 2 // EXPERT_TILE) * EXPERT_TILE
        streams = {"lat": x_lat}
        groups = [[("lat", 0, half, lat_row_moe)], [("lat", half, n_lat - half, lat_row_moe)]]
        if not last:
            streams["ctx"] = x_ctx
            groups[1].append(("ctx", 0, b * l_ctx, ctx_row))
        outs = _moe(streams, groups, layer, mods, norm2_g[layer], router_w[layer], router_b[layer],
                    expert_w_gu, expert_b_gu, expert_w_down, expert_b_down, final_norm_g if last else None)
        x_lat = outs["lat"]
        if not last:
            x_ctx = outs["ctx"]
    return x_lat.reshape(b, t_len, d)
```

```python
import functools

import numpy as np
import jax
import jax.numpy as jnp
from jax import lax
from jax.experimental import pallas as pl
from jax.experimental.pallas import tpu as pltpu

F32 = jnp.float32
BF16 = jnp.bfloat16
HIGHEST = lax.Precision.HIGHEST

NORM_EPS = 1e-6
ROPE_BASE = 10000.0
GRID_W = 64

GLA_HEADS = 4
GLA_DK = 64
GLA_DV = 128
GLA_LOWRANK = 16
GLA_TEMP = 16.0
GLA_CHUNK = 64
GLA_QK = GLA_HEADS * GLA_DK
GLA_V = GLA_HEADS * GLA_DV
CONV_CH = 512
CONV_WIDTH = 31
NA_HEADS = 16
NA_HEAD_DIM = 64
NA_WIDTH = NA_HEADS * NA_HEAD_DIM
NA_WIN_R = 8
NA_WIN_C = 16
N_EXPERTS = 32
TOP_K = 4
D_EXPERT = 1024
SWIGLU_LIMIT = 7.0
SWIGLU_ALPHA = 1.702

OFF_Q = 0
OFF_G = OFF_Q + GLA_QK
OFF_GLU = OFF_G + GLA_V
OFF_K = OFF_GLU + 2 * CONV_CH
OFF_V = OFF_K + GLA_QK
OFF_AF = OFF_V + GLA_V
OFF_AB = OFF_AF + GLA_LOWRANK

LANE = 128
VMEM_LIMIT = 56 * 1024 * 1024

HEAD_PAD = LANE
QK_PAD = GLA_HEADS * HEAD_PAD
PK_Q, PK_K, PK_V, PK_G, PK_A, PK_GT, PK_LR = 0, 512, 1024, 1536, 2048, 2560, 3072
PK_WIDTH = PK_LR + LANE

MOD_ROWS = 40
NEG_BIG = -1e30
ROW_TILE = 256
EXPERT_TILE = 512
ISSUE_UNROLL = 8
EXPERT_SPLIT = 2


def _cparams(*sem):
    return pltpu.CompilerParams(dimension_semantics=sem, vmem_limit_bytes=VMEM_LIMIT)


def _norm_mod(x, g, scale, shift):
    y = x * lax.rsqrt(jnp.mean(x * x, axis=-1, keepdims=True) + NORM_EPS) * g
    return y * (1.0 + scale) + shift


def _sigmoid(x):
    return 1.0 / (1.0 + jnp.exp(-x))


def _mods_kernel(c_ref, w_ref, b_ref, o_ref):
    c = c_ref[...]
    s = c * _sigmoid(c)
    o_ref[...] = jnp.dot(s, w_ref[...], precision=HIGHEST, preferred_element_type=F32) + b_ref[...]


def _mods(c_all, ada_w, ada_b):
    depth, d, n6 = ada_w.shape
    nb = 512
    out = pl.pallas_call(
        _mods_kernel,
        out_shape=jax.ShapeDtypeStruct((depth, MOD_ROWS, n6), F32),
        grid=(depth, n6 // nb),
        in_specs=[
            pl.BlockSpec((MOD_ROWS, d), lambda l, j: (0, 0)),
            pl.BlockSpec((None, d, nb), lambda l, j: (l, 0, j)),
            pl.BlockSpec((None, 1, nb), lambda l, j: (l, 0, j)),
        ],
        out_specs=pl.BlockSpec((None, MOD_ROWS, nb), lambda l, j: (l, 0, j)),
        compiler_params=_cparams("arbitrary", "arbitrary"),
        name="adaln_mods",
    )(c_all, ada_w, ada_b.reshape(depth, 1, n6))
    return out.reshape(depth, MOD_ROWS * 6, 1, d)


def _mod_spec(row_fn, chunk, d):
    return pl.BlockSpec((None, 1, d), lambda i, *_: (row_fn(i) * 6 + chunk, 0, 0))


def _nmm_kernel(x_ref, g_ref, sc_ref, sh_ref, w_ref, *rest, gla_layout, rope):
    if rope:
        cos_ref, sin_ref, o_ref = rest
    else:
        (o_ref,) = rest
    h = _norm_mod(x_ref[...], g_ref[...], sc_ref[...], sh_ref[...]).astype(BF16)
    nout = o_ref.shape[1]
    chunk = 512
    for j0 in range(0, nout, chunk):
        j1 = min(j0 + chunk, nout)
        acc = jnp.dot(h, w_ref[:, j0:j1], preferred_element_type=F32)
        if gla_layout and j0 in (PK_Q, PK_K):
            if rope:
                lane = lax.broadcasted_iota(jnp.int32, acc.shape, 1)
                first = (lane % 32) < 16
                rot = jnp.where(first, pltpu.roll(acc, QK_PAD - 16, 1), pltpu.roll(acc, 16, 1))
                acc = acc * cos_ref[...] + rot * sin_ref[...]
            if j0 == PK_Q:
                acc = acc * (GLA_DK ** -0.5)
        o_ref[:, j0:j1] = acc.astype(o_ref.dtype)


def _norm_mod_matmul(x, g, mods, row_fn, chunks, w, *, tm, gla_layout=False, rope=None, name):
    n, d = x.shape
    nout = w.shape[1]
    in_specs = [
        pl.BlockSpec((tm, d), lambda i: (i, 0)),
        pl.BlockSpec((1, d), lambda i: (0, 0)),
        _mod_spec(row_fn, chunks[0], d),
        _mod_spec(row_fn, chunks[1], d),
        pl.BlockSpec((d, nout), lambda i: (0, 0)),
    ]
    args = [x, g.reshape(1, d), mods, mods, w]
    if rope is not None:
        cos_t, sin_t = rope
        t_tiles = cos_t.shape[0] // tm
        in_specs += [pl.BlockSpec((tm, QK_PAD), lambda i: (i % t_tiles, 0))] * 2
        args += [cos_t, sin_t]
    return pl.pallas_call(
        functools.partial(_nmm_kernel, gla_layout=gla_layout, rope=rope is not None),
        out_shape=jax.ShapeDtypeStruct((n, nout), BF16),
        grid=(n // tm,),
        in_specs=in_specs,
        out_specs=pl.BlockSpec((tm, nout), lambda i: (i, 0)),
        compiler_params=_cparams("arbitrary"),
        name=name,
    )(*args)


def _mmres_kernel(*refs, n_parts):
    a_refs = refs[:n_parts]
    w_ref, x_ref, gate_ref, o_ref = refs[n_parts:]
    acc = None
    k0 = 0
    for a_ref in a_refs:
        kk = a_ref.shape[1]
        t = jnp.dot(a_ref[...], w_ref[k0:k0 + kk, :], preferred_element_type=F32)
        acc = t if acc is None else acc + t
        k0 += kk
    o_ref[...] = x_ref[...] + gate_ref[...] * acc


def _matmul_residual(parts, w, x, mods, row_fn, gate_chunk, *, tm, name):
    n, d = x.shape
    in_specs = [pl.BlockSpec((tm, a.shape[1]), lambda i: (i, 0)) for a in parts]
    in_specs += [
        pl.BlockSpec(w.shape, lambda i: (0, 0)),
        pl.BlockSpec((tm, d), lambda i: (i, 0)),
        _mod_spec(row_fn, gate_chunk, d),
    ]
    return pl.pallas_call(
        functools.partial(_mmres_kernel, n_parts=len(parts)),
        out_shape=jax.ShapeDtypeStruct((n, d), F32),
        grid=(n // tm,),
        in_specs=in_specs,
        out_specs=pl.BlockSpec((tm, d), lambda i: (i, 0)),
        compiler_params=_cparams("arbitrary"),
        name=name,
    )(*parts, w, x, mods)


def _gla_kernel(q_ref, k_ref, v_ref, g_ref, a_ref, wa_ref, ba_ref, ng_ref, s0f_ref, s0b_ref,
                y_ref, sf_ref, sb_ref, of_ref, ob_ref):
    t_len = q_ref.shape[0]
    n_chunks = t_len // GLA_CHUNK
    c = GLA_CHUNK
    sf_ref[...] = s0f_ref[...]
    sb_ref[...] = s0b_ref[...]

    row = lax.broadcasted_iota(jnp.int32, (c, c), 0)
    col = lax.broadcasted_iota(jnp.int32, (c, c), 1)
    row3 = lax.broadcasted_iota(jnp.int32, (c, 3 * c), 0)
    col3 = lax.broadcasted_iota(jnp.int32, (c, 3 * c), 1) % c
    lane = lax.broadcasted_iota(jnp.int32, (1, QK_PAD), 1)
    real_lane = ((lane % HEAD_PAD) < GLA_DK).astype(F32)

    def dir_step(ci, forward):
        r0 = pl.multiple_of(ci * c, c)
        rows = pl.ds(r0, c)
        off = 0 if forward else QK_PAD
        z = jnp.dot(a_ref[rows, :], wa_ref[:, off:off + QK_PAD], preferred_element_type=F32)
        z = z + ba_ref[:, off:off + QK_PAD]
        log_sig = jnp.minimum(z, 0.0) - jnp.log1p(jnp.exp(-jnp.abs(z)))
        la = log_sig * (1.0 / GLA_TEMP) * real_lane
        keep = (row >= col) if forward else (row <= col)
        keep3 = (row3 >= col3) if forward else (row3 <= col3)
        la_hi = la.astype(BF16)
        rem = la - la_hi.astype(F32)
        la_mid = rem.astype(BF16)
        la_lo = (rem - la_mid.astype(F32)).astype(BF16)
        bc = jnp.dot(keep3.astype(BF16), jnp.concatenate([la_hi, la_mid, la_lo], axis=0),
                     preferred_element_type=F32)
        b_last = bc[c - 1:c, :] if forward else bc[0:1, :]
        q = q_ref[rows, :].astype(F32)
        k = k_ref[rows, :].astype(F32)
        v = v_ref[rows, :]
        qt = (q * jnp.exp(bc)).astype(BF16)
        kt = (k * jnp.exp(-bc)).astype(BF16)
        kd = (k * jnp.exp(b_last - bc)).astype(BF16)
        decay = jnp.exp(b_last)
        s_ref = sf_ref if forward else sb_ref
        o_ref = of_ref if forward else ob_ref
        for h in range(GLA_HEADS):
            sl = slice(h * HEAD_PAD, (h + 1) * HEAD_PAD)
            att = lax.dot_general(qt[:, sl], kt[:, sl], (((1,), (1,)), ((), ())),
                                  preferred_element_type=F32)
            att = jnp.where(keep, att, 0.0).astype(BF16)
            st = s_ref[0, h]
            o = jnp.dot(att, v[:, sl], preferred_element_type=F32)
            o = o + lax.dot_general(qt[:, sl], st.astype(BF16), (((1,), (1,)), ((), ())),
                                    preferred_element_type=F32)
            kv_t = lax.dot_general(v[:, sl], kd[:, sl], (((0,), (0,)), ((), ())),
                                   preferred_element_type=F32)
            s_ref[0, h] = st * decay[:, sl] + kv_t
            o_ref[rows, sl] = o

    def step(i, carry):
        dir_step(i, True)
        dir_step(n_chunks - 1 - i, False)
        return carry

    lax.fori_loop(0, n_chunks, step, 0)

    blk = min(256, t_len)

    def post(i, carry):
        rows = pl.ds(pl.multiple_of(i * blk, blk), blk)
        o = of_ref[rows, :] + ob_ref[rows, :]
        g = g_ref[rows, :].astype(F32)
        gate = g * _sigmoid(g)
        for h in range(GLA_HEADS):
            sl = slice(h * GLA_DV, (h + 1) * GLA_DV)
            oh = o[:, sl]
            oh = oh * lax.rsqrt(jnp.mean(oh * oh, axis=-1, keepdims=True) + NORM_EPS)
            y_ref[rows, sl] = (oh * ng_ref[:, sl] * gate[:, sl]).astype(y_ref.dtype)
        return carry

    lax.fori_loop(0, t_len // blk, post, 0)


def _gla(p, t_len, wa, ba, norm_g, s0f, s0b):
    n = p.shape[0]
    b = n // t_len

    def col(width, start):
        return pl.BlockSpec((t_len, width), lambda i: (i, start // width))

    st_spec = pl.BlockSpec((1, GLA_HEADS, GLA_DV, HEAD_PAD), lambda i: (i, 0, 0, 0))
    st_shape = jax.ShapeDtypeStruct((b, GLA_HEADS, GLA_DV, HEAD_PAD), F32)
    return pl.pallas_call(
        _gla_kernel,
        out_shape=(jax.ShapeDtypeStruct((n, GLA_V), BF16), st_shape, st_shape),
        grid=(b,),
        in_specs=[
            col(QK_PAD, PK_Q), col(QK_PAD, PK_K), col(GLA_V, PK_V), col(GLA_V, PK_G), col(LANE, PK_LR),
            pl.BlockSpec(wa.shape, lambda i: (0, 0)),
            pl.BlockSpec(ba.shape, lambda i: (0, 0)),
            pl.BlockSpec((1, GLA_V), lambda i: (0, 0)),
            st_spec, st_spec,
        ],
        out_specs=(pl.BlockSpec((t_len, GLA_V), lambda i: (i, 0)), st_spec, st_spec),
        scratch_shapes=[pltpu.VMEM((t_len, GLA_V), F32), pltpu.VMEM((t_len, GLA_V), F32)],
        compiler_params=_cparams("arbitrary"),
        name="gla_scan",
    )(p, p, p, p, p, wa, ba, norm_g.reshape(1, GLA_V), s0f, s0b)


CONV_HALO = 16
CONV_ROWS = 64


def _conv_kernel(a_ref, gt_ref, w_ref, cb_ref, lg_ref, lb_ref, y_ref, u_ref):
    t_len = a_ref.shape[0]
    zeros = jnp.zeros((CONV_HALO, CONV_CH), F32)
    u_ref[0:CONV_HALO, :] = zeros
    u_ref[CONV_HALO + t_len:2 * CONV_HALO + t_len, :] = zeros
    blk = min(256, t_len)

    def glu(i, carry):
        r0 = pl.multiple_of(i * blk, blk)
        a = a_ref[pl.ds(r0, blk), :].astype(F32)
        gt = gt_ref[pl.ds(r0, blk), :].astype(F32)
        u_ref[pl.ds(r0 + CONV_HALO, blk), :] = a * _sigmoid(gt)
        return carry

    lax.fori_loop(0, t_len // blk, glu, 0)
    shift = CONV_HALO - CONV_WIDTH // 2
    win_rows = CONV_ROWS + 2 * CONV_HALO

    def tile(i, carry):
        r0 = pl.multiple_of(i * CONV_ROWS, CONV_ROWS)
        parts = []
        for lb in range(CONV_CH // LANE):
            ls = slice(lb * LANE, (lb + 1) * LANE)
            win = u_ref[pl.ds(r0, win_rows), ls]
            acc = jnp.zeros((CONV_ROWS, LANE), F32)
            for b in range(8):
                wb = win if b == 0 else pltpu.roll(win, win_rows - b, 0)
                for a in range(win_rows // 8):
                    j = 8 * a + b - shift
                    if 0 <= j < CONV_WIDTH:
                        acc = acc + wb[8 * a:8 * a + CONV_ROWS, :] * w_ref[j:j + 1, ls]
            parts.append(acc)
        y = jnp.concatenate(parts, axis=1) + cb_ref[...]
        mu = jnp.mean(y, axis=-1, keepdims=True)
        yc = y - mu
        var = jnp.mean(yc * yc, axis=-1, keepdims=True)
        yn = yc * lax.rsqrt(var + NORM_EPS) * lg_ref[...] + lb_ref[...]
        y_ref[pl.ds(r0, CONV_ROWS), :] = (yn * _sigmoid(yn)).astype(y_ref.dtype)
        return carry

    lax.fori_loop(0, t_len // CONV_ROWS, tile, 0)


def _conv(p, t_len, conv_w, conv_b, ln_g, ln_b):
    n = p.shape[0]
    w_pad = jnp.pad(conv_w, ((0, 32 - CONV_WIDTH), (0, 0)))
    vec = pl.BlockSpec((1, CONV_CH), lambda i: (0, 0))
    return pl.pallas_call(
        _conv_kernel,
        out_shape=jax.ShapeDtypeStruct((n, CONV_CH), BF16),
        grid=(n // t_len,),
        in_specs=[
            pl.BlockSpec((t_len, CONV_CH), lambda i: (i, PK_A // CONV_CH)),
            pl.BlockSpec((t_len, CONV_CH), lambda i: (i, PK_GT // CONV_CH)),
            pl.BlockSpec((32, CONV_CH), lambda i: (0, 0)),
            vec, vec, vec,
        ],
        out_specs=pl.BlockSpec((t_len, CONV_CH), lambda i: (i, 0)),
        scratch_shapes=[pltpu.VMEM((t_len + 2 * CONV_HALO, CONV_CH), F32)],
        compiler_params=_cparams("arbitrary"),
        name="conformer_conv",
    )(p, p, w_pad, conv_b.reshape(1, -1), ln_g.reshape(1, -1), ln_b.reshape(1, -1))


NA_QROWS = 4
NA_BAND = 12


def _na_geometry(rows):
    assert rows >= NA_BAND and rows % NA_QROWS == 0
    win_r = min(NA_WIN_R, rows)
    starts, classes, sigs = [], [], []
    for rb in range(rows // NA_QROWS):
        bs = int(np.clip(rb * NA_QROWS - NA_WIN_R // 2, 0, rows - NA_BAND))
        sig = tuple((rb * NA_QROWS + i - bs,
                     int(np.clip(rb * NA_QROWS + i - win_r // 2, 0, rows - win_r)) - bs)
                    for i in range(NA_QROWS))
        if sig not in sigs:
            sigs.append(sig)
        starts.append(bs)
        classes.append(sigs.index(sig))
    return starts, classes, sigs, win_r


def _na_bias(rpb, sigs, win_r):
    w = GRID_W
    cols = np.arange(w)
    c_start = np.clip(cols - NA_WIN_C // 2, 0, w - NA_WIN_C)
    col_ok = (cols[None, :] >= c_start[:, None]) & (cols[None, :] < c_start[:, None] + NA_WIN_C)
    dc = np.clip(cols[None, :] - cols[:, None] + NA_WIN_C - 1, 0, 2 * NA_WIN_C - 2)
    rpb_col = jnp.take(rpb.astype(F32), jnp.asarray(dc.reshape(-1)), axis=2)
    rpb_col = rpb_col.reshape(NA_HEADS, 2 * NA_WIN_R - 1, w, w)
    rpb_col = jnp.where(jnp.asarray(col_ok)[None, None], rpb_col, NEG_BIG)
    kj = np.arange(NA_BAND)
    out = []
    for sig in sigs:
        q_rel = np.array([s[0] for s in sig])
        r_rel = np.array([s[1] for s in sig])
        row_ok = (kj[None, :] >= r_rel[:, None]) & (kj[None, :] < r_rel[:, None] + win_r)
        dr = np.clip(kj[None, :] - q_rel[:, None] + NA_WIN_R - 1, 0, 2 * NA_WIN_R - 2)
        bias = jnp.take(rpb_col, jnp.asarray(dr.reshape(-1)), axis=1)
        bias = bias.reshape(NA_HEADS, NA_QROWS, NA_BAND, w, w)
        bias = jnp.where(jnp.asarray(row_ok)[None, :, :, None, None], bias, NEG_BIG)
        out.append(bias.transpose(0, 1, 3, 2, 4).reshape(NA_HEADS, NA_QROWS * w, NA_BAND * w))
    return jnp.stack(out, axis=0)


def _na_kernel(q_ref, k_ref, v_ref, kc_ref, vc_ref, bias_ref, o_ref, *, starts, classes):
    w = GRID_W
    scale = NA_HEAD_DIM ** -0.5
    nq = NA_QROWS * w
    kc = kc_ref[...]
    vc = vc_ref[...]
    nt = (((1,), (1,)), ((), ()))

    def head_lanes(rows, hh):
        lane = lax.broadcasted_iota(jnp.int32, (rows, LANE), 1)
        return (lane >= hh * NA_HEAD_DIM) & (lane < (hh + 1) * NA_HEAD_DIM)

    vc_h = [jnp.where(head_lanes(vc.shape[0], hh), vc, jnp.ones_like(vc)) for hh in range(2)]
    for rb, (bs, cls) in enumerate(zip(starts, classes)):
        q2 = q_ref[rb * nq:(rb + 1) * nq, :]
        kb = k_ref[bs * w:(bs + NA_BAND) * w, :]
        vb = v_ref[bs * w:(bs + NA_BAND) * w, :]
        acc = None
        for hh in range(2):
            mine = head_lanes(nq, hh)
            qh = jnp.where(mine, q2, jnp.zeros_like(q2)) * scale
            s_loc = lax.dot_general(qh, kb, nt, preferred_element_type=F32) + bias_ref[cls, hh]
            s_ctx = lax.dot_general(qh, kc, nt, preferred_element_type=F32)
            m = jnp.maximum(jnp.max(s_loc, axis=-1, keepdims=True), jnp.max(s_ctx, axis=-1, keepdims=True))
            p_loc = jnp.exp((s_loc - m).astype(BF16))
            p_ctx = jnp.exp((s_ctx - m).astype(BF16))
            vb_h = jnp.where(head_lanes(vb.shape[0], hh), vb, jnp.ones_like(vb))
            o = jnp.dot(p_loc, vb_h, preferred_element_type=F32)
            o = o + jnp.dot(p_ctx, vc_h[hh], preferred_element_type=F32)
            o = o / pltpu.roll(o, NA_HEAD_DIM, 1)
            acc = o if acc is None else jnp.where(mine, o, acc)
        o_ref[rb * nq:(rb + 1) * nq, :] = acc.astype(o_ref.dtype)


def _neighbourhood_attention(qkv, kv_ctx, rpb, t_len, l_ctx):
    n = qkv.shape[0]
    b = n // t_len
    rows = t_len // GRID_W
    starts, classes, sigs, win_r = _na_geometry(rows)
    bias = _na_bias(rpb, sigs, win_r)
    n_pairs = NA_HEADS // 2
    nq, nk = NA_QROWS * GRID_W, NA_BAND * GRID_W
    return pl.pallas_call(
        functools.partial(_na_kernel, starts=starts, classes=classes),
        out_shape=jax.ShapeDtypeStruct((n, NA_WIDTH), BF16),
        grid=(n_pairs, b),
        in_specs=[
            pl.BlockSpec((t_len, LANE), lambda j, i: (i, j)),
            pl.BlockSpec((t_len, LANE), lambda j, i: (i, n_pairs + j)),
            pl.BlockSpec((t_len, LANE), lambda j, i: (i, 2 * n_pairs + j)),
            pl.BlockSpec((l_ctx, LANE), lambda j, i: (i, j)),
            pl.BlockSpec((l_ctx, LANE), lambda j, i: (i, n_pairs + j)),
            pl.BlockSpec((len(sigs), 2, nq, nk), lambda j, i: (0, j, 0, 0)),
        ],
        out_specs=pl.BlockSpec((t_len, LANE), lambda j, i: (i, j)),
        compiler_params=_cparams("arbitrary", "arbitrary"),
        name="neighbourhood_attention",
    )(qkv, qkv, qkv, kv_ctx, kv_ctx, bias)


INFO_ID, INFO_W, INFO_RANK = 0, TOP_K, 2 * TOP_K
INFO_ROWS = 16


def _router_kernel(x_ref, g_ref, sc_ref, sh_ref, wrt_ref, br_ref, tri_ref, cnt0_ref, *rest):
    h_ref, info_ref, cnt_ref = rest[-3:]
    tm = x_ref.shape[0]

    @pl.when(pl.program_id(0) == 0)
    def _():
        cnt_ref[...] = cnt0_ref[...]

    h = _norm_mod(x_ref[...], g_ref[...], sc_ref[...], sh_ref[...])
    h_ref[...] = h
    logits = lax.dot_general(wrt_ref[...], h, (((1,), (1,)), ((), ())), precision=HIGHEST,
                             preferred_element_type=F32) + br_ref[...]
    sub = lax.broadcasted_iota(jnp.int32, (N_EXPERTS, tm), 0)
    cur = logits
    vals, ids = [], []
    for _ in range(TOP_K):
        m = jnp.max(cur, axis=0, keepdims=True)
        idx = jnp.min(jnp.where(cur == m, sub, N_EXPERTS), axis=0, keepdims=True)
        vals.append(m)
        ids.append(idx)
        cur = jnp.where(sub == idx, NEG_BIG, cur)
    ex = [jnp.exp(v - vals[0]) for v in vals]
    den = ex[0] + ex[1] + ex[2] + ex[3]
    onehot = jnp.zeros((N_EXPERTS, tm), F32)
    for idx in ids:
        onehot = onehot + (sub == idx).astype(F32)
    before = jnp.dot(onehot.astype(BF16), tri_ref[...], preferred_element_type=F32)
    running = cnt_ref[:, 0:1]
    base = running + before
    row = lax.broadcasted_iota(jnp.int32, (INFO_ROWS, tm), 0)
    info = jnp.zeros((INFO_ROWS, tm), F32)
    for k in range(TOP_K):
        rank = jnp.sum(jnp.where(sub == ids[k], base, 0.0), axis=0, keepdims=True)
        info = info + jnp.where(row == INFO_ID + k, ids[k].astype(F32), 0.0)
        info = info + jnp.where(row == INFO_W + k, ex[k] / den, 0.0)
        info = info + jnp.where(row == INFO_RANK + k, rank, 0.0)
    info_ref[...] = info
    cnt_ref[...] = jnp.broadcast_to(running + jnp.sum(onehot, axis=1, keepdims=True), cnt_ref.shape)


def _router(x, start, n, g, mods, row_fn, w_rt, b_r, tri, cnt0, h_prev, h_rows, h_start, *, name):
    d = x.shape[1]
    tm = ROW_TILE
    t0, ht0 = start // tm, h_start // tm
    in_specs = [
        pl.BlockSpec((tm, d), lambda i: (i + t0, 0)),
        pl.BlockSpec((1, d), lambda i: (0, 0)),
        _mod_spec(lambda i: row_fn(i + t0), 4, d),
        _mod_spec(lambda i: row_fn(i + t0), 3, d),
        pl.BlockSpec((N_EXPERTS, d), lambda i: (0, 0)),
        pl.BlockSpec((N_EXPERTS, 1), lambda i: (0, 0)),
        pl.BlockSpec((tm, tm), lambda i: (0, 0)),
        pl.BlockSpec((N_EXPERTS, LANE), lambda i: (0, 0)),
    ]
    args = [x, g.reshape(1, d), mods, mods, w_rt, b_r, tri, cnt0]
    aliases = {}
    if h_prev is not None:
        in_specs.append(pl.BlockSpec(memory_space=pl.ANY))
        args.append(h_prev)
        aliases = {len(args) - 1: 0}
    return pl.pallas_call(
        _router_kernel,
        out_shape=(jax.ShapeDtypeStruct((h_rows, d), F32),
                   jax.ShapeDtypeStruct((INFO_ROWS, n), F32),
                   jax.ShapeDtypeStruct((N_EXPERTS, LANE), F32)),
        grid=(n // tm,),
        in_specs=in_specs,
        out_specs=(pl.BlockSpec((tm, d), lambda i: (i + ht0, 0)),
                   pl.BlockSpec((INFO_ROWS, tm), lambda i: (0, i)),
                   pl.BlockSpec((N_EXPERTS, LANE), lambda i: (0, 0))),
        input_output_aliases=aliases,
        compiler_params=_cparams("arbitrary"),
        name=name,
    )(*args)


def _row_copy(src_ref, src_row, dst_ref, dst_row, sem):
    return pltpu.make_async_copy(src_ref.at[pl.ds(src_row, 1)], dst_ref.at[pl.ds(dst_row, 1)], sem)


def _dispatch_kernel(pos_ref, h_ref, *rest):
    xs_ref, sem = rest[-2], rest[-1]
    tm = h_ref.shape[0]

    def issue(rb, carry):
        for u in range(ISSUE_UNROLL):
            r = rb * ISSUE_UNROLL + u
            for k in range(TOP_K):
                _row_copy(h_ref, r, xs_ref, pos_ref[0, 0, r * TOP_K + k], sem).start()
        return carry

    lax.fori_loop(0, tm // ISSUE_UNROLL, issue, 0)
    for _ in range(TOP_K):
        pltpu.make_async_copy(h_ref, xs_ref.at[pl.ds(0, tm)], sem).wait()


def _dispatch(pos, h, xs, n_slots, *, name):
    n, d = h.shape
    tm = ROW_TILE
    in_specs = [
        pl.BlockSpec((1, 1, tm * TOP_K), lambda i: (i, 0, 0), memory_space=pltpu.SMEM),
        pl.BlockSpec((tm, d), lambda i: (i, 0)),
    ]
    args = [pos.reshape(n // tm, 1, tm * TOP_K), h]
    aliases = {}
    if xs is not None:
        in_specs.append(pl.BlockSpec(memory_space=pl.ANY))
        args.append(xs)
        aliases = {2: 0}
    return pl.pallas_call(
        _dispatch_kernel,
        out_shape=jax.ShapeDtypeStruct((n_slots, d), F32),
        grid=(n // tm,),
        in_specs=in_specs,
        out_specs=pl.BlockSpec(memory_space=pl.ANY),
        scratch_shapes=[pltpu.SemaphoreType.DMA],
        input_output_aliases=aliases,
        compiler_params=_cparams("arbitrary"),
        name=name,
    )(*args)


PAD_ROWS = 8


def _padfill_kernel(lo_ref, mid_ref, hi_ref, xs_in, xs_ref, zero_ref, sem):
    del xs_in
    zero_ref[...] = jnp.zeros(zero_ref.shape, F32)
    for phase in range(2):
        for e in range(N_EXPERTS):
            def row(r, carry):
                cp = _row_copy(zero_ref, 0, xs_ref, r, sem)
                cp.start() if phase == 0 else cp.wait()
                return carry

            def block(b, carry):
                r = pl.multiple_of(mid_ref[e] + b * PAD_ROWS, PAD_ROWS)
                cp = pltpu.make_async_copy(zero_ref, xs_ref.at[pl.ds(r, PAD_ROWS)], sem)
                cp.start() if phase == 0 else cp.wait()
                return carry

            lax.fori_loop(lo_ref[e], mid_ref[e], row, 0)
            lax.fori_loop(0, (hi_ref[e] - mid_ref[e]) // PAD_ROWS, block, 0)


def _padfill(xs, lo, hi):
    n_rows, d = xs.shape
    mid = jnp.minimum(((lo + PAD_ROWS - 1) // PAD_ROWS) * PAD_ROWS, hi)
    return pl.pallas_call(
        _padfill_kernel,
        out_shape=jax.ShapeDtypeStruct((n_rows, d), F32),
        grid_spec=pltpu.PrefetchScalarGridSpec(
            num_scalar_prefetch=3,
            grid=(1,),
            in_specs=[pl.BlockSpec(memory_space=pl.ANY)],
            out_specs=pl.BlockSpec(memory_space=pl.ANY),
            scratch_shapes=[pltpu.VMEM((PAD_ROWS, d), F32), pltpu.SemaphoreType.DMA],
        ),
        input_output_aliases={3: 0},
        compiler_params=_cparams("arbitrary"),
        name="moe_padfill",
    )(lo, mid, hi, xs)


def _expert_kernel(te_ref, na_ref, xs_ref, wgu_ref, bgu_ref, wd_ref, bd_ref, *rest, side, q, n_side):
    if side is None:
        y_ref, wgu_bf, wd_bf = rest
    elif side == "scatter":
        spos_ref, src, y_ref, dst, wgu_bf, wd_bf, sem = rest
    else:
        spos_ref, src, y_ref, dst, wgu_bf, wd_bf, sem = rest
    i = pl.program_id(0)
    active = i < na_ref[0]
    del n_side

    def side_issue(part=0, parts=1):
        for j in range(part * q // parts, (part + 1) * q // parts):
            if side == "scatter":
                _row_copy(src, j // TOP_K, dst, spos_ref[0, 0, j], sem).start()
            else:
                _row_copy(src, spos_ref[0, 0, j], dst.at[j % TOP_K], j // TOP_K, sem).start()

    def side_wait():
        for k in range(TOP_K):
            if side == "scatter":
                pltpu.make_async_copy(src, dst.at[pl.ds(0, q // TOP_K)], sem).wait()
            else:
                pltpu.make_async_copy(src.at[pl.ds(0, q // TOP_K)], dst.at[k], sem).wait()

    first_of_expert = (i == 0) | (te_ref[i] != te_ref[jnp.maximum(i - 1, 0)])

    @pl.when(active & first_of_expert)
    def _():
        rows = 256
        for r in range(0, wgu_ref.shape[0], rows):
            wgu_bf[r:r + rows, :] = wgu_ref[r:r + rows, :].astype(BF16)
        for r in range(0, wd_ref.shape[0], rows):
            wd_bf[r:r + rows, :] = wd_ref[r:r + rows, :].astype(BF16)

    @pl.when(active)
    def _():
        x = xs_ref[...].astype(BF16)
        f = wd_ref.shape[0]
        half = f // EXPERT_SPLIT
        y = None
        part, parts = 0, 3 * EXPERT_SPLIT

        def issue_next():
            nonlocal part
            if side is not None:
                side_issue(part, parts)
            part += 1

        for lo in range(0, f, half):
            issue_next()
            gt = jnp.dot(x, wgu_bf[:, lo:lo + half], preferred_element_type=F32) + bgu_ref[:, lo:lo + half]
            issue_next()
            up = jnp.dot(x, wgu_bf[:, f + lo:f + lo + half], preferred_element_type=F32)
            up = up + bgu_ref[:, f + lo:f + lo + half]
            gt = jnp.minimum(gt, SWIGLU_LIMIT)
            up = jnp.clip(up, -SWIGLU_LIMIT, SWIGLU_LIMIT)
            act = (up + 1.0) * gt * _sigmoid(SWIGLU_ALPHA * gt)
            issue_next()
            t = jnp.dot(act.astype(BF16), wd_bf[lo:lo + half, :], preferred_element_type=F32)
            y = t if y is None else y + t
        y_ref[...] = y + bd_ref[...]
        if side is not None:
            side_wait()

    @pl.when(jnp.logical_not(active))
    def _():
        y_ref[...] = jnp.zeros(y_ref.shape, F32)
        if side is not None:
            side_issue()
            side_wait()


def _experts(xs, n_slots, tile_expert, n_active, layer, w_gu, b_gu, w_down, b_down, side=None):
    d = xs.shape[1]
    tm = EXPERT_TILE
    depth, n_e, _, two_f = w_gu.shape
    n_steps = n_slots // tm
    in_specs = [
        pl.BlockSpec((tm, d), lambda i, te, na: (jnp.minimum(i, na[0] - 1), 0)),
        pl.BlockSpec((None, None, d, two_f), lambda i, te, na: (layer, te[i], 0, 0)),
        pl.BlockSpec((None, None, 1, two_f), lambda i, te, na: (layer, te[i], 0, 0)),
        pl.BlockSpec((None, None, two_f // 2, d), lambda i, te, na: (layer, te[i], 0, 0)),
        pl.BlockSpec((None, None, 1, d), lambda i, te, na: (layer, te[i], 0, 0)),
    ]
    args = [xs, w_gu, b_gu.reshape(depth, n_e, 1, two_f), w_down, b_down.reshape(depth, n_e, 1, d)]
    out_shape = [jax.ShapeDtypeStruct((n_slots, d), F32)]
    out_specs = [pl.BlockSpec((tm, d), lambda i, te, na: (i, 0))]
    scratch = [pltpu.VMEM((d, two_f), BF16), pltpu.VMEM((two_f // 2, d), BF16)]
    kind, q, n_side = None, 0, 0
    if side is not None:
        kind, pos, src, n_side, dst_slots = side
        pairs = n_side * TOP_K
        per_step = PAD_ROWS * TOP_K
        q = -(-(-(-pairs // n_steps)) // per_step) * per_step
        toks = q // TOP_K
        last_block = (n_side - 1) // toks
        spos_spec = pl.BlockSpec((1, 1, q), lambda i, te, na: (i, 0, 0), memory_space=pltpu.SMEM)
        if kind == "scatter":
            spare = jnp.full((n_steps * q - pairs,), dst_slots, jnp.int32)
            in_specs += [spos_spec, pl.BlockSpec((toks, d), lambda i, te, na: (jnp.minimum(i, last_block), 0))]
            out_shape.append(jax.ShapeDtypeStruct((dst_slots + PAD_ROWS, d), F32))
            out_specs.append(pl.BlockSpec(memory_space=pl.ANY))
        else:
            spare = jnp.zeros((n_steps * q - pairs,), jnp.int32)
            in_specs += [spos_spec, pl.BlockSpec(memory_space=pl.ANY)]
            out_shape.append(jax.ShapeDtypeStruct((TOP_K, n_steps * toks, d), F32))
            out_specs.append(pl.BlockSpec((TOP_K, toks, d), lambda i, te, na: (0, i, 0)))
        spos = jnp.concatenate([pos.reshape(-1), spare]).reshape(n_steps, 1, q)
        args += [spos, src]
        scratch.append(pltpu.SemaphoreType.DMA)
    outs = pl.pallas_call(
        functools.partial(_expert_kernel, side=kind, q=q, n_side=n_side),
        out_shape=tuple(out_shape),
        grid_spec=pltpu.PrefetchScalarGridSpec(
            num_scalar_prefetch=2,
            grid=(n_steps,),
            in_specs=in_specs,
            out_specs=tuple(out_specs),
            scratch_shapes=scratch,
        ),
        compiler_params=_cparams("arbitrary"),
        name="moe_experts" if kind is None else f"moe_experts_{kind}",
    )(tile_expert, n_active, *args)
    return outs if side is not None else (outs[0], None)


def _combine_kernel(*refs, final, gathered, aliased):
    refs = list(refs)
    pos_ref = None if gathered else refs.pop(0)
    x_ref, wts_ref, gate_ref = refs[:3]
    refs = refs[3:]
    fg_ref = refs.pop(0) if final else None
    y_src = refs.pop(0)
    if aliased:
        refs.pop(0)
    o_ref = refs.pop(0)
    tm = x_ref.shape[0]
    if gathered:
        ybuf = y_src
    else:
        ybuf, sem = refs

        def issue(rb, carry):
            for u in range(ISSUE_UNROLL):
                r = rb * ISSUE_UNROLL + u
                for k in range(TOP_K):
                    _row_copy(y_src, pos_ref[0, 0, r * TOP_K + k], ybuf.at[k], r, sem).start()
            return carry

        lax.fori_loop(0, tm // ISSUE_UNROLL, issue, 0)
        for k in range(TOP_K):
            pltpu.make_async_copy(y_src.at[pl.ds(0, tm)], ybuf.at[k], sem).wait()
    wts = wts_ref[...]
    f = wts[:, 0:1] * ybuf[0]
    for k in range(1, TOP_K):
        f = f + wts[:, k:k + 1] * ybuf[k]
    out = x_ref[...] + gate_ref[...] * f
    if final:
        out = out * lax.rsqrt(jnp.mean(out * out, axis=-1, keepdims=True) + NORM_EPS) * fg_ref[...]
    o_ref[...] = out


def _combine(x, start, n, g_start, pos, wts, mods, row_fn, y, gathered, final_g, out_prev, *, name):
    d = x.shape[1]
    tm = ROW_TILE
    t0, gt0 = start // tm, g_start // tm
    in_specs, args = [], []
    if not gathered:
        in_specs.append(pl.BlockSpec((1, 1, tm * TOP_K), lambda i: (i + gt0, 0, 0), memory_space=pltpu.SMEM))
        args.append(pos.reshape(pos.shape[0] // tm, 1, tm * TOP_K))
    in_specs += [
        pl.BlockSpec((tm, d), lambda i: (i + t0, 0)),
        pl.BlockSpec((tm, TOP_K), lambda i: (i + gt0, 0)),
        _mod_spec(lambda i: row_fn(i + t0), 5, d),
    ]
    args += [x, wts, mods]
    if final_g is not None:
        in_specs.append(pl.BlockSpec((1, d), lambda i: (0, 0)))
        args.append(final_g.reshape(1, d))
    if gathered:
        in_specs.append(pl.BlockSpec((TOP_K, tm, d), lambda i: (0, i + gt0, 0)))
        scratch = []
    else:
        in_specs.append(pl.BlockSpec(memory_space=pl.ANY))
        scratch = [pltpu.VMEM((TOP_K, tm, d), F32), pltpu.SemaphoreType.DMA]
    args.append(y)
    aliases = {}
    if out_prev is not None:
        in_specs.append(pl.BlockSpec(memory_space=pl.ANY))
        args.append(out_prev)
        aliases = {len(args) - 1: 0}
    return pl.pallas_call(
        functools.partial(_combine_kernel, final=final_g is not None, gathered=gathered,
                          aliased=out_prev is not None),
        out_shape=jax.ShapeDtypeStruct(x.shape, F32),
        grid=(n // tm,),
        in_specs=in_specs,
        out_specs=pl.BlockSpec((tm, d), lambda i: (i + t0, 0)),
        scratch_shapes=scratch,
        input_output_aliases=aliases,
        compiler_params=_cparams("arbitrary"),
        name=name,
    )(*args)


def _route_group(xs_by_name, streams, mods, norm_g, w_rt, b_col, tri, tag):
    tm = EXPERT_TILE
    n_group = sum(s[2] for s in streams)
    cnt = jnp.zeros((N_EXPERTS, LANE), F32)
    h, infos, g_start = None, [], 0
    for s, (name, start, n, row_fn) in enumerate(streams):
        h, info, cnt = _router(xs_by_name[name], start, n, norm_g, mods, row_fn, w_rt, b_col, tri, cnt,
                               h, n_group, g_start, name=f"moe_router_{tag}{s}")
        infos.append(info)
        g_start += n
    info = jnp.concatenate(infos, axis=1) if len(infos) > 1 else infos[0]
    counts = cnt[:, 0].astype(jnp.int32)
    padded = ((counts + tm - 1) // tm) * tm
    ends = jnp.cumsum(padded)
    offsets = ends - padded
    n_slots = n_group * TOP_K + N_EXPERTS * tm
    n_active = (ends[-1] // tm).astype(jnp.int32).reshape(1)
    tile_start = jnp.minimum(jnp.arange(n_slots // tm, dtype=jnp.int32), n_active[0] - 1) * tm
    tile_expert = jnp.sum((ends[None, :] <= tile_start[:, None]).astype(jnp.int32), axis=1)
    tile_expert = jnp.minimum(tile_expert, N_EXPERTS - 1)
    eid = info[INFO_ID:INFO_ID + TOP_K].astype(jnp.int32)
    rank = info[INFO_RANK:INFO_RANK + TOP_K].astype(jnp.int32)
    base = jnp.zeros_like(eid)
    for e in range(N_EXPERTS):
        base = jnp.where(eid == e, offsets[e], base)
    return dict(
        n=n_group, h=h, n_slots=n_slots, n_active=n_active, tile_expert=tile_expert,
        pos=(base + rank).T,
        wts=info[INFO_W:INFO_W + TOP_K].T,
        pad_lo=(offsets + counts).astype(jnp.int32), pad_hi=ends.astype(jnp.int32),
    )


def _moe(xs_by_name, groups, layer, mods, norm_g, w_r, b_r, w_gu, b_gu, w_down, b_down, final_g):
    assert len(groups) in (1, 2)
    w_rt = w_r.T
    b_col = b_r.reshape(N_EXPERTS, 1)
    tok = np.arange(ROW_TILE)
    tri = jnp.asarray(tok[:, None] < tok[None, :], BF16)
    routed = [_route_group(xs_by_name, streams, mods, norm_g, w_rt, b_col, tri, "ab"[g])
              for g, streams in enumerate(groups)]
    weights = (layer, w_gu, b_gu, w_down, b_down)

    first = routed[0]
    slots = _dispatch(first["pos"], first["h"], None, first["n_slots"], name="moe_dispatch")
    slots = _padfill(slots, first["pad_lo"], first["pad_hi"])
    if len(groups) == 1:
        y, _ = _experts(slots, first["n_slots"], first["tile_expert"], first["n_active"], *weights)
        ys = [(y, False)]
    else:
        second = routed[1]
        y0, slots1 = _experts(slots, first["n_slots"], first["tile_expert"], first["n_active"], *weights,
                              side=("scatter", second["pos"], second["h"], second["n"], second["n_slots"]))
        slots1 = _padfill(slots1, second["pad_lo"], second["pad_hi"])
        y1, y0_by_choice = _experts(slots1, second["n_slots"], second["tile_expert"], second["n_active"],
                                    *weights, side=("gather", first["pos"], y0, first["n"], 0))
        ys = [(y0_by_choice, True), (y1, False)]

    outs = {}
    for g, (streams, r, (y, gathered)) in enumerate(zip(groups, routed, ys)):
        g_start = 0
        for s, (name, start, n, row_fn) in enumerate(streams):
            outs[name] = _combine(xs_by_name[name], start, n, g_start, r["pos"], r["wts"], mods, row_fn, y,
                                  gathered, final_g, outs.get(name), name=f"moe_combine_{'ab'[g]}{s}")
            g_start += n
    return outs


def _pad_heads(w):
    lead = w.shape[:-1]
    w4 = w.reshape(*lead, GLA_HEADS, GLA_DK)
    w4 = jnp.pad(w4, [(0, 0)] * len(lead) + [(0, 0), (0, HEAD_PAD - GLA_DK)])
    return w4.reshape(*lead, QK_PAD)


def _pack_gla_in(w_in):
    d = w_in.shape[0]
    lr = jnp.pad(w_in[:, OFF_AF:OFF_AB + GLA_LOWRANK], ((0, 0), (0, LANE - 2 * GLA_LOWRANK)))
    return jnp.concatenate([
        _pad_heads(w_in[:, OFF_Q:OFF_Q + GLA_QK]),
        _pad_heads(w_in[:, OFF_K:OFF_K + GLA_QK]),
        w_in[:, OFF_V:OFF_V + GLA_V],
        w_in[:, OFF_G:OFF_G + GLA_V],
        w_in[:, OFF_GLU:OFF_GLU + 2 * CONV_CH],
        lr,
    ], axis=1).astype(BF16)


def _pack_decay(wa_f, ba_f, wa_b, ba_b):
    wa = jnp.zeros((LANE, 2 * QK_PAD), F32)
    wa = wa.at[0:GLA_LOWRANK, 0:QK_PAD].set(_pad_heads(wa_f))
    wa = wa.at[GLA_LOWRANK:2 * GLA_LOWRANK, QK_PAD:].set(_pad_heads(wa_b))
    ba = jnp.concatenate([_pad_heads(ba_f), _pad_heads(ba_b)]).reshape(1, 2 * QK_PAD)
    return wa.astype(BF16), ba


def _rope_tables(t_len):
    t = jnp.arange(t_len)
    row_pos = (t // GRID_W).astype(F32)
    col_pos = (t % GRID_W).astype(F32)
    half = GLA_DK // 4
    inv_freq = ROPE_BASE ** (-jnp.arange(half, dtype=F32) / half)
    dim = np.arange(HEAD_PAD)
    real = dim < GLA_DK
    use_col = (dim % GLA_DK) >= GLA_DK // 2
    first = (dim % (GLA_DK // 2)) < half
    pos = jnp.where(jnp.asarray(use_col)[None, :], col_pos[:, None], row_pos[:, None])
    ang = pos * inv_freq[dim % half][None, :]
    cos = jnp.where(jnp.asarray(real)[None, :], jnp.cos(ang), 0.0)
    sin = jnp.where(jnp.asarray(real)[None, :], jnp.sin(ang), 0.0)
    sin = jnp.where(jnp.asarray(first)[None, :], -sin, sin)
    return jnp.tile(cos, (1, GLA_HEADS)), jnp.tile(sin, (1, GLA_HEADS))


def kernel(x, c, ctx, c_ctx, ada_w, ada_b, norm1_g, norm2_g, gla_conv_w_in, gla_wa_fwd, gla_ba_fwd,
           gla_wa_bwd, gla_ba_bwd, gla_norm_g, conv_dw_w, conv_dw_b, conv_ln_g, conv_ln_b,
           gla_conv_w_out, na_w_qkv, na_rpb, na_w_out, router_w, router_b, expert_w_gu, expert_b_gu,
           expert_w_down, expert_b_down, final_norm_g):
    b, t_len, d = x.shape
    l_ctx = ctx.shape[1]
    depth = ada_w.shape[0]
    assert b + 1 <= MOD_ROWS and t_len % 512 == 0 and l_ctx % ROW_TILE == 0

    c_all = jnp.concatenate([c, c_ctx[None, :], jnp.zeros((MOD_ROWS - b - 1, d), F32)], axis=0)
    mods_all = _mods(c_all, ada_w, ada_b)

    x_lat = x.reshape(b * t_len, d)
    x_ctx = ctx.reshape(b * l_ctx, d)
    tm_lat = 512
    tm_ctx = min(512, l_ctx)

    def lat_row(i, tm=tm_lat):
        return (i * tm) // t_len

    def ctx_row(i):
        return b

    def lat_row_moe(i):
        return (i * ROW_TILE) // t_len

    for layer in range(depth):
        last = layer == depth - 1
        j = layer // 2
        mods = mods_all[layer]
        if layer % 2 == 0:
            w_pack = _pack_gla_in(gla_conv_w_in[j])
            wa, ba = _pack_decay(gla_wa_fwd[j], gla_ba_fwd[j], gla_wa_bwd[j], gla_ba_bwd[j])
            w_out = gla_conv_w_out[j].astype(BF16)
            zero_state = jnp.zeros((b, GLA_HEADS, GLA_DV, HEAD_PAD), F32)
            p_c = _norm_mod_matmul(x_ctx, norm1_g[layer], mods, ctx_row, (1, 0), w_pack, tm=tm_ctx,
                                   gla_layout=True, name="gla_in_ctx")
            y_gla_c, s_f, s_b = _gla(p_c, l_ctx, wa, ba, gla_norm_g[j], zero_state, zero_state)
            p_l = _norm_mod_matmul(x_lat, norm1_g[layer], mods, lat_row, (1, 0), w_pack, tm=tm_lat,
                                   gla_layout=True, rope=_rope_tables(t_len), name="gla_in_lat")
            y_gla_l, _, _ = _gla(p_l, t_len, wa, ba, gla_norm_g[j], s_f, s_b)
            y_conv_l = _conv(p_l, t_len, conv_dw_w[j], conv_dw_b[j], conv_ln_g[j], conv_ln_b[j])
            x_lat = _matmul_residual([y_gla_l, y_conv_l], w_out, x_lat, mods, lat_row, 2, tm=tm_lat,
                                     name="mix_out_lat")
            if not last:
                y_conv_c = _conv(p_c, l_ctx, conv_dw_w[j], conv_dw_b[j], conv_ln_g[j], conv_ln_b[j])
                x_ctx = _matmul_residual([y_gla_c, y_conv_c], w_out, x_ctx, mods, ctx_row, 2, tm=tm_ctx,
                                         name="mix_out_ctx")
        else:
            w_qkv = na_w_qkv[j].astype(BF16)
            kv_c = _norm_mod_matmul(x_ctx, norm1_g[layer], mods, ctx_row, (1, 0), w_qkv[:, NA_WIDTH:],
                                    tm=tm_ctx, name="na_kv_ctx")
            qkv = _norm_mod_matmul(x_lat, norm1_g[layer], mods, lat_row, (1, 0), w_qkv, tm=tm_lat,
                                   name="na_qkv_lat")
            o_l = _neighbourhood_attention(qkv, kv_c, na_rpb[j], t_len, l_ctx)
            x_lat = _matmul_residual([o_l], na_w_out[j].astype(BF16), x_lat, mods, lat_row, 2, tm=tm_lat,
                                     name="na_out_lat")
            if not last:
                raise NotImplementedError("context output of an attention layer is only needed mid-stack")
        streams = {"lat": x_lat}
        groups = [[("lat", 0, b * t_len, lat_row_moe)]]
        if not last:
            streams["ctx"] = x_ctx
            groups[0].append(("ctx", 0, b * l_ctx, ctx_row))
        outs = _moe(streams, groups, layer, mods, norm2_g[layer], router_w[layer], router_b[layer],
                    expert_w_gu, expert_b_gu, expert_w_down, expert_b_down, final_norm_g if last else None)
        x_lat = outs["lat"]
        if not last:
            x_ctx = outs["ctx"]
    return x_lat.reshape(b, t_len, d)
```

```python
import functools

import numpy as np
import jax
import jax.numpy as jnp
from jax import lax
from jax.experimental import pallas as pl
from jax.experimental.pallas import tpu as pltpu

F32 = jnp.float32
BF16 = jnp.bfloat16
HIGHEST = lax.Precision.HIGHEST

NORM_EPS = 1e-6
ROPE_BASE = 10000.0
GRID_W = 64

GLA_HEADS = 4
GLA_DK = 64
GLA_DV = 128
GLA_LOWRANK = 16
GLA_TEMP = 16.0
GLA_CHUNK = 64
GLA_QK = GLA_HEADS * GLA_DK
GLA_V = GLA_HEADS * GLA_DV
CONV_CH = 512
CONV_WIDTH = 31
NA_HEADS = 16
NA_HEAD_DIM = 64
NA_WIDTH = NA_HEADS * NA_HEAD_DIM
NA_WIN_R = 8
NA_WIN_C = 16
N_EXPERTS = 32
TOP_K = 4
D_EXPERT = 1024
SWIGLU_LIMIT = 7.0
SWIGLU_ALPHA = 1.702

OFF_Q = 0
OFF_G = OFF_Q + GLA_QK
OFF_GLU = OFF_G + GLA_V
OFF_K = OFF_GLU + 2 * CONV_CH
OFF_V = OFF_K + GLA_QK
OFF_AF = OFF_V + GLA_V
OFF_AB = OFF_AF + GLA_LOWRANK

LANE = 128
VMEM_LIMIT = 56 * 1024 * 1024

HEAD_PAD = LANE
QK_PAD = GLA_HEADS * HEAD_PAD
PK_Q, PK_K, PK_V, PK_G, PK_A, PK_GT, PK_LR = 0, 512, 1024, 1536, 2048, 2560, 3072
PK_WIDTH = PK_LR + LANE

MOD_ROWS = 40
NEG_BIG = -1e30
ROW_TILE = 256
COPY_TILE = 512
EXPERT_TILE = 512
ISSUE_UNROLL = 8
EXPERT_SPLIT = 2


def _cparams(*sem):
    return pltpu.CompilerParams(dimension_semantics=sem, vmem_limit_bytes=VMEM_LIMIT)


def _norm_mod(x, g, scale, shift):
    y = x * lax.rsqrt(jnp.mean(x * x, axis=-1, keepdims=True) + NORM_EPS) * g
    return y * (1.0 + scale) + shift


def _sigmoid(x):
    return 1.0 / (1.0 + jnp.exp(-x))


def _mods_kernel(c_ref, w_ref, b_ref, o_ref):
    c = c_ref[...]
    s = c * _sigmoid(c)
    o_ref[...] = jnp.dot(s, w_ref[...], precision=HIGHEST, preferred_element_type=F32) + b_ref[...]


def _mods(c_all, ada_w, ada_b):
    depth, d, n6 = ada_w.shape
    nb = 512
    out = pl.pallas_call(
        _mods_kernel,
        out_shape=jax.ShapeDtypeStruct((depth, MOD_ROWS, n6), F32),
        grid=(depth, n6 // nb),
        in_specs=[
            pl.BlockSpec((MOD_ROWS, d), lambda l, j: (0, 0)),
            pl.BlockSpec((None, d, nb), lambda l, j: (l, 0, j)),
            pl.BlockSpec((None, 1, nb), lambda l, j: (l, 0, j)),
        ],
        out_specs=pl.BlockSpec((None, MOD_ROWS, nb), lambda l, j: (l, 0, j)),
        compiler_params=_cparams("arbitrary", "arbitrary"),
        name="adaln_mods",
    )(c_all, ada_w, ada_b.reshape(depth, 1, n6))
    return out.reshape(depth, MOD_ROWS * 6, 1, d)


def _mod_spec(row_fn, chunk, d):
    return pl.BlockSpec((None, 1, d), lambda i, *_: (row_fn(i) * 6 + chunk, 0, 0))


def _nmm_kernel(x_ref, g_ref, sc_ref, sh_ref, w_ref, *rest, gla_layout, rope):
    if rope:
        cos_ref, sin_ref, o_ref = rest
    else:
        (o_ref,) = rest
    h = _norm_mod(x_ref[...], g_ref[...], sc_ref[...], sh_ref[...]).astype(BF16)
    nout = o_ref.shape[1]
    chunk = 512
    for j0 in range(0, nout, chunk):
        j1 = min(j0 + chunk, nout)
        acc = jnp.dot(h, w_ref[:, j0:j1], preferred_element_type=F32)
        if gla_layout and j0 in (PK_Q, PK_K):
            if rope:
                lane = lax.broadcasted_iota(jnp.int32, acc.shape, 1)
                first = (lane % 32) < 16
                rot = jnp.where(first, pltpu.roll(acc, QK_PAD - 16, 1), pltpu.roll(acc, 16, 1))
                acc = acc * cos_ref[...] + rot * sin_ref[...]
            if j0 == PK_Q:
                acc = acc * (GLA_DK ** -0.5)
        o_ref[:, j0:j1] = acc.astype(o_ref.dtype)


def _norm_mod_matmul(x, g, mods, row_fn, chunks, w, *, tm, gla_layout=False, rope=None, name):
    n, d = x.shape
    nout = w.shape[1]
    in_specs = [
        pl.BlockSpec((tm, d), lambda i: (i, 0)),
        pl.BlockSpec((1, d), lambda i: (0, 0)),
        _mod_spec(row_fn, chunks[0], d),
        _mod_spec(row_fn, chunks[1], d),
        pl.BlockSpec((d, nout), lambda i: (0, 0)),
    ]
    args = [x, g.reshape(1, d), mods, mods, w]
    if rope is not None:
        cos_t, sin_t = rope
        t_tiles = cos_t.shape[0] // tm
        in_specs += [pl.BlockSpec((tm, QK_PAD), lambda i: (i % t_tiles, 0))] * 2
        args += [cos_t, sin_t]
    return pl.pallas_call(
        functools.partial(_nmm_kernel, gla_layout=gla_layout, rope=rope is not None),
        out_shape=jax.ShapeDtypeStruct((n, nout), BF16),
        grid=(n // tm,),
        in_specs=in_specs,
        out_specs=pl.BlockSpec((tm, nout), lambda i: (i, 0)),
        compiler_params=_cparams("arbitrary"),
        name=name,
    )(*args)


def _mmres_kernel(*refs, n_parts):
    a_refs = refs[:n_parts]
    w_ref, x_ref, gate_ref, o_ref = refs[n_parts:]
    acc = None
    k0 = 0
    for a_ref in a_refs:
        kk = a_ref.shape[1]
        t = jnp.dot(a_ref[...], w_ref[k0:k0 + kk, :], preferred_element_type=F32)
        acc = t if acc is None else acc + t
        k0 += kk
    o_ref[...] = x_ref[...] + gate_ref[...] * acc


def _matmul_residual(parts, w, x, mods, row_fn, gate_chunk, *, tm, name):
    n, d = x.shape
    in_specs = [pl.BlockSpec((tm, a.shape[1]), lambda i: (i, 0)) for a in parts]
    in_specs += [
        pl.BlockSpec(w.shape, lambda i: (0, 0)),
        pl.BlockSpec((tm, d), lambda i: (i, 0)),
        _mod_spec(row_fn, gate_chunk, d),
    ]
    return pl.pallas_call(
        functools.partial(_mmres_kernel, n_parts=len(parts)),
        out_shape=jax.ShapeDtypeStruct((n, d), F32),
        grid=(n // tm,),
        in_specs=in_specs,
        out_specs=pl.BlockSpec((tm, d), lambda i: (i, 0)),
        compiler_params=_cparams("arbitrary"),
        name=name,
    )(*parts, w, x, mods)


def _gla_kernel(q_ref, k_ref, v_ref, g_ref, a_ref, wa_ref, ba_ref, ng_ref, s0f_ref, s0b_ref,
                y_ref, sf_ref, sb_ref, of_ref, ob_ref):
    t_len = q_ref.shape[0]
    n_chunks = t_len // GLA_CHUNK
    c = GLA_CHUNK
    sf_ref[...] = s0f_ref[...]
    sb_ref[...] = s0b_ref[...]

    row = lax.broadcasted_iota(jnp.int32, (c, c), 0)
    col = lax.broadcasted_iota(jnp.int32, (c, c), 1)
    row3 = lax.broadcasted_iota(jnp.int32, (c, 3 * c), 0)
    col3 = lax.broadcasted_iota(jnp.int32, (c, 3 * c), 1) % c
    lane = lax.broadcasted_iota(jnp.int32, (1, QK_PAD), 1)
    real_lane = ((lane % HEAD_PAD) < GLA_DK).astype(F32)

    def dir_step(ci, forward):
        r0 = pl.multiple_of(ci * c, c)
        rows = pl.ds(r0, c)
        off = 0 if forward else QK_PAD
        z = jnp.dot(a_ref[rows, :], wa_ref[:, off:off + QK_PAD], preferred_element_type=F32)
        z = z + ba_ref[:, off:off + QK_PAD]
        log_sig = jnp.minimum(z, 0.0) - jnp.log1p(jnp.exp(-jnp.abs(z)))
        la = log_sig * (1.0 / GLA_TEMP) * real_lane
        keep = (row >= col) if forward else (row <= col)
        keep3 = (row3 >= col3) if forward else (row3 <= col3)
        la_hi = la.astype(BF16)
        rem = la - la_hi.astype(F32)
        la_mid = rem.astype(BF16)
        la_lo = (rem - la_mid.astype(F32)).astype(BF16)
        bc = jnp.dot(keep3.astype(BF16), jnp.concatenate([la_hi, la_mid, la_lo], axis=0),
                     preferred_element_type=F32)
        b_last = bc[c - 1:c, :] if forward else bc[0:1, :]
        q = q_ref[rows, :].astype(F32)
        k = k_ref[rows, :].astype(F32)
        v = v_ref[rows, :]
        qt = (q * jnp.exp(bc)).astype(BF16)
        kt = (k * jnp.exp(-bc)).astype(BF16)
        kd = (k * jnp.exp(b_last - bc)).astype(BF16)
        decay = jnp.exp(b_last)
        s_ref = sf_ref if forward else sb_ref
        o_ref = of_ref if forward else ob_ref
        for h in range(GLA_HEADS):
            sl = slice(h * HEAD_PAD, (h + 1) * HEAD_PAD)
            att = lax.dot_general(qt[:, sl], kt[:, sl], (((1,), (1,)), ((), ())),
                                  preferred_element_type=F32)
            att = jnp.where(keep, att, 0.0).astype(BF16)
            st = s_ref[0, h]
            o = jnp.dot(att, v[:, sl], preferred_element_type=F32)
            o = o + lax.dot_general(qt[:, sl], st.astype(BF16), (((1,), (1,)), ((), ())),
                                    preferred_element_type=F32)
            kv_t = lax.dot_general(v[:, sl], kd[:, sl], (((0,), (0,)), ((), ())),
                                   preferred_element_type=F32)
            s_ref[0, h] = st * decay[:, sl] + kv_t
            o_ref[rows, sl] = o

    def step(i, carry):
        dir_step(i, True)
        dir_step(n_chunks - 1 - i, False)
        return carry

    lax.fori_loop(0, n_chunks, step, 0)

    blk = min(256, t_len)

    def post(i, carry):
        rows = pl.ds(pl.multiple_of(i * blk, blk), blk)
        o = of_ref[rows, :] + ob_ref[rows, :]
        g = g_ref[rows, :].astype(F32)
        gate = g * _sigmoid(g)
        for h in range(GLA_HEADS):
            sl = slice(h * GLA_DV, (h + 1) * GLA_DV)
            oh = o[:, sl]
            oh = oh * lax.rsqrt(jnp.mean(oh * oh, axis=-1, keepdims=True) + NORM_EPS)
            y_ref[rows, sl] = (oh * ng_ref[:, sl] * gate[:, sl]).astype(y_ref.dtype)
        return carry

    lax.fori_loop(0, t_len // blk, post, 0)


def _gla(p, t_len, wa, ba, norm_g, s0f, s0b):
    n = p.shape[0]
    b = n // t_len

    def col(width, start):
        return pl.BlockSpec((t_len, width), lambda i: (i, start // width))

    st_spec = pl.BlockSpec((1, GLA_HEADS, GLA_DV, HEAD_PAD), lambda i: (i, 0, 0, 0))
    st_shape = jax.ShapeDtypeStruct((b, GLA_HEADS, GLA_DV, HEAD_PAD), F32)
    return pl.pallas_call(
        _gla_kernel,
        out_shape=(jax.ShapeDtypeStruct((n, GLA_V), BF16), st_shape, st_shape),
        grid=(b,),
        in_specs=[
            col(QK_PAD, PK_Q), col(QK_PAD, PK_K), col(GLA_V, PK_V), col(GLA_V, PK_G), col(LANE, PK_LR),
            pl.BlockSpec(wa.shape, lambda i: (0, 0)),
            pl.BlockSpec(ba.shape, lambda i: (0, 0)),
            pl.BlockSpec((1, GLA_V), lambda i: (0, 0)),
            st_spec, st_spec,
        ],
        out_specs=(pl.BlockSpec((t_len, GLA_V), lambda i: (i, 0)), st_spec, st_spec),
        scratch_shapes=[pltpu.VMEM((t_len, GLA_V), F32), pltpu.VMEM((t_len, GLA_V), F32)],
        compiler_params=_cparams("arbitrary"),
        name="gla_scan",
    )(p, p, p, p, p, wa, ba, norm_g.reshape(1, GLA_V), s0f, s0b)


CONV_HALO = 16
CONV_ROWS = 64


def _conv_kernel(a_ref, gt_ref, w_ref, cb_ref, lg_ref, lb_ref, y_ref, u_ref):
    t_len = a_ref.shape[0]
    zeros = jnp.zeros((CONV_HALO, CONV_CH), F32)
    u_ref[0:CONV_HALO, :] = zeros
    u_ref[CONV_HALO + t_len:2 * CONV_HALO + t_len, :] = zeros
    blk = min(256, t_len)

    def glu(i, carry):
        r0 = pl.multiple_of(i * blk, blk)
        a = a_ref[pl.ds(r0, blk), :].astype(F32)
        gt = gt_ref[pl.ds(r0, blk), :].astype(F32)
        u_ref[pl.ds(r0 + CONV_HALO, blk), :] = a * _sigmoid(gt)
        return carry

    lax.fori_loop(0, t_len // blk, glu, 0)
    shift = CONV_HALO - CONV_WIDTH // 2
    win_rows = CONV_ROWS + 2 * CONV_HALO

    def tile(i, carry):
        r0 = pl.multiple_of(i * CONV_ROWS, CONV_ROWS)
        parts = []
        for lb in range(CONV_CH // LANE):
            ls = slice(lb * LANE, (lb + 1) * LANE)
            win = u_ref[pl.ds(r0, win_rows), ls]
            acc = jnp.zeros((CONV_ROWS, LANE), F32)
            for b in range(8):
                wb = win if b == 0 else pltpu.roll(win, win_rows - b, 0)
                for a in range(win_rows // 8):
                    j = 8 * a + b - shift
                    if 0 <= j < CONV_WIDTH:
                        acc = acc + wb[8 * a:8 * a + CONV_ROWS, :] * w_ref[j:j + 1, ls]
            parts.append(acc)
        y = jnp.concatenate(parts, axis=1) + cb_ref[...]
        mu = jnp.mean(y, axis=-1, keepdims=True)
        yc = y - mu
        var = jnp.mean(yc * yc, axis=-1, keepdims=True)
        yn = yc * lax.rsqrt(var + NORM_EPS) * lg_ref[...] + lb_ref[...]
        y_ref[pl.ds(r0, CONV_ROWS), :] = (yn * _sigmoid(yn)).astype(y_ref.dtype)
        return carry

    lax.fori_loop(0, t_len // CONV_ROWS, tile, 0)


def _conv(p, t_len, conv_w, conv_b, ln_g, ln_b):
    n = p.shape[0]
    w_pad = jnp.pad(conv_w, ((0, 32 - CONV_WIDTH), (0, 0)))
    vec = pl.BlockSpec((1, CONV_CH), lambda i: (0, 0))
    return pl.pallas_call(
        _conv_kernel,
        out_shape=jax.ShapeDtypeStruct((n, CONV_CH), BF16),
        grid=(n // t_len,),
        in_specs=[
            pl.BlockSpec((t_len, CONV_CH), lambda i: (i, PK_A // CONV_CH)),
            pl.BlockSpec((t_len, CONV_CH), lambda i: (i, PK_GT // CONV_CH)),
            pl.BlockSpec((32, CONV_CH), lambda i: (0, 0)),
            vec, vec, vec,
        ],
        out_specs=pl.BlockSpec((t_len, CONV_CH), lambda i: (i, 0)),
        scratch_shapes=[pltpu.VMEM((t_len + 2 * CONV_HALO, CONV_CH), F32)],
        compiler_params=_cparams("arbitrary"),
        name="conformer_conv",
    )(p, p, w_pad, conv_b.reshape(1, -1), ln_g.reshape(1, -1), ln_b.reshape(1, -1))


NA_QROWS = 4
NA_BAND = 12


def _na_geometry(rows):
    assert rows >= NA_BAND and rows % NA_QROWS == 0
    win_r = min(NA_WIN_R, rows)
    starts, classes, sigs = [], [], []
    for rb in range(rows // NA_QROWS):
        bs = int(np.clip(rb * NA_QROWS - NA_WIN_R // 2, 0, rows - NA_BAND))
        sig = tuple((rb * NA_QROWS + i - bs,
                     int(np.clip(rb * NA_QROWS + i - win_r // 2, 0, rows - win_r)) - bs)
                    for i in range(NA_QROWS))
        if sig not in sigs:
            sigs.append(sig)
        starts.append(bs)
        classes.append(sigs.index(sig))
    return starts, classes, sigs, win_r


def _na_bias(rpb, sigs, win_r):
    w = GRID_W
    cols = np.arange(w)
    c_start = np.clip(cols - NA_WIN_C // 2, 0, w - NA_WIN_C)
    col_ok = (cols[None, :] >= c_start[:, None]) & (cols[None, :] < c_start[:, None] + NA_WIN_C)
    dc = np.clip(cols[None, :] - cols[:, None] + NA_WIN_C - 1, 0, 2 * NA_WIN_C - 2)
    rpb_col = jnp.take(rpb.astype(F32), jnp.asarray(dc.reshape(-1)), axis=2)
    rpb_col = rpb_col.reshape(NA_HEADS, 2 * NA_WIN_R - 1, w, w)
    rpb_col = jnp.where(jnp.asarray(col_ok)[None, None], rpb_col, NEG_BIG)
    kj = np.arange(NA_BAND)
    out = []
    for sig in sigs:
        q_rel = np.array([s[0] for s in sig])
        r_rel = np.array([s[1] for s in sig])
        row_ok = (kj[None, :] >= r_rel[:, None]) & (kj[None, :] < r_rel[:, None] + win_r)
        dr = np.clip(kj[None, :] - q_rel[:, None] + NA_WIN_R - 1, 0, 2 * NA_WIN_R - 2)
        bias = jnp.take(rpb_col, jnp.asarray(dr.reshape(-1)), axis=1)
        bias = bias.reshape(NA_HEADS, NA_QROWS, NA_BAND, w, w)
        bias = jnp.where(jnp.asarray(row_ok)[None, :, :, None, None], bias, NEG_BIG)
        out.append(bias.transpose(0, 1, 3, 2, 4).reshape(NA_HEADS, NA_QROWS * w, NA_BAND * w))
    return jnp.stack(out, axis=0)


def _na_kernel(q_ref, k_ref, v_ref, kc_ref, vc_ref, bias_ref, o_ref, *, starts, classes):
    w = GRID_W
    scale = NA_HEAD_DIM ** -0.5
    nq = NA_QROWS * w
    kc = kc_ref[...]
    vc = vc_ref[...]
    nt = (((1,), (1,)), ((), ()))

    def head_lanes(rows, hh):
        lane = lax.broadcasted_iota(jnp.int32, (rows, LANE), 1)
        return (lane >= hh * NA_HEAD_DIM) & (lane < (hh + 1) * NA_HEAD_DIM)

    vc_h = [jnp.where(head_lanes(vc.shape[0], hh), vc, jnp.ones_like(vc)) for hh in range(2)]
    for rb, (bs, cls) in enumerate(zip(starts, classes)):
        q2 = q_ref[rb * nq:(rb + 1) * nq, :]
        kb = k_ref[bs * w:(bs + NA_BAND) * w, :]
        vb = v_ref[bs * w:(bs + NA_BAND) * w, :]
        acc = None
        for hh in range(2):
            mine = head_lanes(nq, hh)
            qh = jnp.where(mine, q2, jnp.zeros_like(q2)) * scale
            s_loc = lax.dot_general(qh, kb, nt, preferred_element_type=F32) + bias_ref[cls, hh]
            s_ctx = lax.dot_general(qh, kc, nt, preferred_element_type=F32)
            m = jnp.maximum(jnp.max(s_loc, axis=-1, keepdims=True), jnp.max(s_ctx, axis=-1, keepdims=True))
            p_loc = jnp.exp((s_loc - m).astype(BF16))
            p_ctx = jnp.exp((s_ctx - m).astype(BF16))
            vb_h = jnp.where(head_lanes(vb.shape[0], hh), vb, jnp.ones_like(vb))
            o = jnp.dot(p_loc, vb_h, preferred_element_type=F32)
            o = o + jnp.dot(p_ctx, vc_h[hh], preferred_element_type=F32)
            o = o / pltpu.roll(o, NA_HEAD_DIM, 1)
            acc = o if acc is None else jnp.where(mine, o, acc)
        o_ref[rb * nq:(rb + 1) * nq, :] = acc.astype(o_ref.dtype)


def _neighbourhood_attention(qkv, kv_ctx, rpb, t_len, l_ctx):
    n = qkv.shape[0]
    b = n // t_len
    rows = t_len // GRID_W
    starts, classes, sigs, win_r = _na_geometry(rows)
    bias = _na_bias(rpb, sigs, win_r)
    n_pairs = NA_HEADS // 2
    nq, nk = NA_QROWS * GRID_W, NA_BAND * GRID_W
    return pl.pallas_call(
        functools.partial(_na_kernel, starts=starts, classes=classes),
        out_shape=jax.ShapeDtypeStruct((n, NA_WIDTH), BF16),
        grid=(n_pairs, b),
        in_specs=[
            pl.BlockSpec((t_len, LANE), lambda j, i: (i, j)),
            pl.BlockSpec((t_len, LANE), lambda j, i: (i, n_pairs + j)),
            pl.BlockSpec((t_len, LANE), lambda j, i: (i, 2 * n_pairs + j)),
            pl.BlockSpec((l_ctx, LANE), lambda j, i: (i, j)),
            pl.BlockSpec((l_ctx, LANE), lambda j, i: (i, n_pairs + j)),
            pl.BlockSpec((len(sigs), 2, nq, nk), lambda j, i: (0, j, 0, 0)),
        ],
        out_specs=pl.BlockSpec((t_len, LANE), lambda j, i: (i, j)),
        compiler_params=_cparams("arbitrary", "arbitrary"),
        name="neighbourhood_attention",
    )(qkv, qkv, qkv, kv_ctx, kv_ctx, bias)


INFO_ID, INFO_W, INFO_RANK = 0, TOP_K, 2 * TOP_K
INFO_ROWS = 16


def _router_kernel(x_ref, g_ref, sc_ref, sh_ref, wrt_ref, br_ref, tri_ref, cnt0_ref, *rest):
    h_ref, info_ref, cnt_ref = rest[-3:]
    tm = x_ref.shape[0]

    @pl.when(pl.program_id(0) == 0)
    def _():
        cnt_ref[...] = cnt0_ref[...]

    h = _norm_mod(x_ref[...], g_ref[...], sc_ref[...], sh_ref[...])
    h_ref[...] = h
    logits = lax.dot_general(wrt_ref[...], h, (((1,), (1,)), ((), ())), precision=HIGHEST,
                             preferred_element_type=F32) + br_ref[...]
    sub = lax.broadcasted_iota(jnp.int32, (N_EXPERTS, tm), 0)
    cur = logits
    vals, ids = [], []
    for _ in range(TOP_K):
        m = jnp.max(cur, axis=0, keepdims=True)
        idx = jnp.min(jnp.where(cur == m, sub, N_EXPERTS), axis=0, keepdims=True)
        vals.append(m)
        ids.append(idx)
        cur = jnp.where(sub == idx, NEG_BIG, cur)
    ex = [jnp.exp(v - vals[0]) for v in vals]
    den = ex[0] + ex[1] + ex[2] + ex[3]
    onehot = jnp.zeros((N_EXPERTS, tm), F32)
    for idx in ids:
        onehot = onehot + (sub == idx).astype(F32)
    before = jnp.dot(onehot.astype(BF16), tri_ref[...], preferred_element_type=F32)
    running = cnt_ref[:, 0:1]
    base = running + before
    row = lax.broadcasted_iota(jnp.int32, (INFO_ROWS, tm), 0)
    info = jnp.zeros((INFO_ROWS, tm), F32)
    for k in range(TOP_K):
        rank = jnp.sum(jnp.where(sub == ids[k], base, 0.0), axis=0, keepdims=True)
        info = info + jnp.where(row == INFO_ID + k, ids[k].astype(F32), 0.0)
        info = info + jnp.where(row == INFO_W + k, ex[k] / den, 0.0)
        info = info + jnp.where(row == INFO_RANK + k, rank, 0.0)
    info_ref[...] = info
    cnt_ref[...] = jnp.broadcast_to(running + jnp.sum(onehot, axis=1, keepdims=True), cnt_ref.shape)


def _router(x, g, mods, row_fn, w_rt, b_r, tri, cnt0, h_prev, h_rows, h_start, *, name):
    n, d = x.shape
    tm = ROW_TILE
    ht0 = h_start // tm
    in_specs = [
        pl.BlockSpec((tm, d), lambda i: (i, 0)),
        pl.BlockSpec((1, d), lambda i: (0, 0)),
        _mod_spec(lambda i: row_fn(i * tm), 4, d),
        _mod_spec(lambda i: row_fn(i * tm), 3, d),
        pl.BlockSpec((N_EXPERTS, d), lambda i: (0, 0)),
        pl.BlockSpec((N_EXPERTS, 1), lambda i: (0, 0)),
        pl.BlockSpec((tm, tm), lambda i: (0, 0)),
        pl.BlockSpec((N_EXPERTS, LANE), lambda i: (0, 0)),
    ]
    args = [x, g.reshape(1, d), mods, mods, w_rt, b_r, tri, cnt0]
    aliases = {}
    if h_prev is not None:
        in_specs.append(pl.BlockSpec(memory_space=pl.ANY))
        args.append(h_prev)
        aliases = {len(args) - 1: 0}
    return pl.pallas_call(
        _router_kernel,
        out_shape=(jax.ShapeDtypeStruct((h_rows, d), F32),
                   jax.ShapeDtypeStruct((INFO_ROWS, n), F32),
                   jax.ShapeDtypeStruct((N_EXPERTS, LANE), F32)),
        grid=(n // tm,),
        in_specs=in_specs,
        out_specs=(pl.BlockSpec((tm, d), lambda i: (i + ht0, 0)),
                   pl.BlockSpec((INFO_ROWS, tm), lambda i: (0, i)),
                   pl.BlockSpec((N_EXPERTS, LANE), lambda i: (0, 0))),
        input_output_aliases=aliases,
        compiler_params=_cparams("arbitrary"),
        name=name,
    )(*args)


def _row_copy(src_ref, src_row, dst_ref, dst_row, sem):
    return pltpu.make_async_copy(src_ref.at[pl.ds(src_row, 1)], dst_ref.at[pl.ds(dst_row, 1)], sem)


def _dispatch_kernel(pos_ref, h_ref, xs_ref, sem):
    tm = h_ref.shape[0]

    def issue(rb, carry):
        base = pl.multiple_of(rb * ISSUE_UNROLL, ISSUE_UNROLL)
        for u in range(ISSUE_UNROLL):
            for k in range(TOP_K):
                _row_copy(h_ref, base + u, xs_ref, pos_ref[0, 0, (base + u) * TOP_K + k], sem).start()
        return carry

    lax.fori_loop(0, tm // ISSUE_UNROLL, issue, 0)
    for _ in range(TOP_K):
        pltpu.make_async_copy(h_ref, xs_ref.at[pl.ds(0, tm)], sem).wait()


def _dispatch(pos, h, n_slots):
    n, d = h.shape
    tm = COPY_TILE
    return pl.pallas_call(
        _dispatch_kernel,
        out_shape=jax.ShapeDtypeStruct((n_slots, d), F32),
        grid=(n // tm,),
        in_specs=[
            pl.BlockSpec((1, 1, tm * TOP_K), lambda i: (i, 0, 0), memory_space=pltpu.SMEM),
            pl.BlockSpec((tm, d), lambda i: (i, 0)),
        ],
        out_specs=pl.BlockSpec(memory_space=pl.ANY),
        scratch_shapes=[pltpu.SemaphoreType.DMA],
        compiler_params=_cparams("arbitrary"),
        name="moe_dispatch",
    )(pos.reshape(n // tm, 1, tm * TOP_K), h)


PAD_ROWS = 8


def _padfill_kernel(lo_ref, mid_ref, hi_ref, xs_in, xs_ref, zero_ref, sem):
    del xs_in
    zero_ref[...] = jnp.zeros(zero_ref.shape, F32)
    for phase in range(2):
        for e in range(N_EXPERTS):
            def row(r, carry):
                cp = _row_copy(zero_ref, 0, xs_ref, r, sem)
                cp.start() if phase == 0 else cp.wait()
                return carry

            def block(b, carry):
                r = pl.multiple_of(mid_ref[e] + b * PAD_ROWS, PAD_ROWS)
                cp = pltpu.make_async_copy(zero_ref, xs_ref.at[pl.ds(r, PAD_ROWS)], sem)
                cp.start() if phase == 0 else cp.wait()
                return carry

            lax.fori_loop(lo_ref[e], mid_ref[e], row, 0)
            lax.fori_loop(0, (hi_ref[e] - mid_ref[e]) // PAD_ROWS, block, 0)


def _padfill(xs, lo, hi):
    mid = jnp.minimum(((lo + PAD_ROWS - 1) // PAD_ROWS) * PAD_ROWS, hi)
    return pl.pallas_call(
        _padfill_kernel,
        out_shape=jax.ShapeDtypeStruct(xs.shape, F32),
        grid_spec=pltpu.PrefetchScalarGridSpec(
            num_scalar_prefetch=3,
            grid=(1,),
            in_specs=[pl.BlockSpec(memory_space=pl.ANY)],
            out_specs=pl.BlockSpec(memory_space=pl.ANY),
            scratch_shapes=[pltpu.VMEM((PAD_ROWS,) + xs.shape[1:], F32), pltpu.SemaphoreType.DMA],
        ),
        input_output_aliases={3: 0},
        compiler_params=_cparams("arbitrary"),
        name="moe_padfill",
    )(lo, mid, hi, xs)


def _expert_kernel(te_ref, na_ref, xs_ref, wgu_ref, bgu_ref, wd_ref, bd_ref, y_ref, wgu_bf, wd_bf):
    i = pl.program_id(0)
    active = i < na_ref[0]
    first_of_expert = (i == 0) | (te_ref[i] != te_ref[jnp.maximum(i - 1, 0)])

    @pl.when(active & first_of_expert)
    def _():
        rows = 256
        for r in range(0, wgu_ref.shape[0], rows):
            wgu_bf[r:r + rows, :] = wgu_ref[r:r + rows, :].astype(BF16)
        for r in range(0, wd_ref.shape[0], rows):
            wd_bf[r:r + rows, :] = wd_ref[r:r + rows, :].astype(BF16)

    @pl.when(active)
    def _():
        x = xs_ref[...].astype(BF16)
        f = wd_ref.shape[0]
        half = f // EXPERT_SPLIT
        y = None
        for lo in range(0, f, half):
            gt = jnp.dot(x, wgu_bf[:, lo:lo + half], preferred_element_type=F32) + bgu_ref[:, lo:lo + half]
            up = jnp.dot(x, wgu_bf[:, f + lo:f + lo + half], preferred_element_type=F32)
            up = up + bgu_ref[:, f + lo:f + lo + half]
            gt = jnp.minimum(gt, SWIGLU_LIMIT)
            up = jnp.clip(up, -SWIGLU_LIMIT, SWIGLU_LIMIT)
            act = (up + 1.0) * gt * _sigmoid(SWIGLU_ALPHA * gt)
            t = jnp.dot(act.astype(BF16), wd_bf[lo:lo + half, :], preferred_element_type=F32)
            y = t if y is None else y + t
        y_ref[...] = y + bd_ref[...]

    @pl.when(jnp.logical_not(active))
    def _():
        y_ref[...] = jnp.zeros(y_ref.shape, F32)


def _experts(xs, tile_expert, n_active, layer, w_gu, b_gu, w_down, b_down):
    n_slots, d = xs.shape
    tm = EXPERT_TILE
    depth, n_e, _, two_f = w_gu.shape
    return pl.pallas_call(
        _expert_kernel,
        out_shape=jax.ShapeDtypeStruct(xs.shape, F32),
        grid_spec=pltpu.PrefetchScalarGridSpec(
            num_scalar_prefetch=2,
            grid=(n_slots // tm,),
            in_specs=[
                pl.BlockSpec((tm, d), lambda i, te, na: (jnp.minimum(i, na[0] - 1), 0)),
                pl.BlockSpec((None, None, d, two_f), lambda i, te, na: (layer, te[i], 0, 0)),
                pl.BlockSpec((None, None, 1, two_f), lambda i, te, na: (layer, te[i], 0, 0)),
                pl.BlockSpec((None, None, two_f // 2, d), lambda i, te, na: (layer, te[i], 0, 0)),
                pl.BlockSpec((None, None, 1, d), lambda i, te, na: (layer, te[i], 0, 0)),
            ],
            out_specs=pl.BlockSpec((tm, d), lambda i, te, na: (i, 0)),
            scratch_shapes=[pltpu.VMEM((d, two_f), BF16), pltpu.VMEM((two_f // 2, d), BF16)],
        ),
        compiler_params=_cparams("arbitrary"),
        name="moe_experts",
    )(tile_expert, n_active, xs, w_gu, b_gu.reshape(depth, n_e, 1, two_f), w_down,
      b_down.reshape(depth, n_e, 1, d))


def _combine_kernel(pos_ref, x_ref, wts_ref, gate_ref, *rest, final):
    if final:
        fg_ref, y_hbm, o_ref, ybuf, sem = rest
    else:
        y_hbm, o_ref, ybuf, sem = rest
    tm = x_ref.shape[0]

    def issue(rb, carry):
        base = pl.multiple_of(rb * ISSUE_UNROLL, ISSUE_UNROLL)
        for u in range(ISSUE_UNROLL):
            for k in range(TOP_K):
                _row_copy(y_hbm, pos_ref[0, 0, (base + u) * TOP_K + k], ybuf.at[k], base + u, sem).start()
        return carry

    lax.fori_loop(0, tm // ISSUE_UNROLL, issue, 0)
    for k in range(TOP_K):
        pltpu.make_async_copy(y_hbm.at[pl.ds(0, tm)], ybuf.at[k], sem).wait()
    wts = wts_ref[...]
    f = wts[:, 0:1] * ybuf[0]
    for k in range(1, TOP_K):
        f = f + wts[:, k:k + 1] * ybuf[k]
    out = x_ref[...] + gate_ref[...] * f
    if final:
        out = out * lax.rsqrt(jnp.mean(out * out, axis=-1, keepdims=True) + NORM_EPS) * fg_ref[...]
    o_ref[...] = out


def _combine(x, g_start, pos, wts, mods, row_fn, y, final_g, *, name):
    n, d = x.shape
    tm = COPY_TILE
    gt0 = g_start // tm
    in_specs = [
        pl.BlockSpec((1, 1, tm * TOP_K), lambda i: (i + gt0, 0, 0), memory_space=pltpu.SMEM),
        pl.BlockSpec((tm, d), lambda i: (i, 0)),
        pl.BlockSpec((tm, TOP_K), lambda i: (i + gt0, 0)),
        _mod_spec(lambda i: row_fn(i * tm), 5, d),
    ]
    args = [pos.reshape(pos.shape[0] // tm, 1, tm * TOP_K), x, wts, mods]
    if final_g is not None:
        in_specs.append(pl.BlockSpec((1, d), lambda i: (0, 0)))
        args.append(final_g.reshape(1, d))
    in_specs.append(pl.BlockSpec(memory_space=pl.ANY))
    args.append(y)
    return pl.pallas_call(
        functools.partial(_combine_kernel, final=final_g is not None),
        out_shape=jax.ShapeDtypeStruct((n, d), F32),
        grid=(n // tm,),
        in_specs=in_specs,
        out_specs=pl.BlockSpec((tm, d), lambda i: (i, 0)),
        scratch_shapes=[pltpu.VMEM((TOP_K, tm) + y.shape[1:], F32), pltpu.SemaphoreType.DMA],
        compiler_params=_cparams("arbitrary"),
        name=name,
    )(*args)


def _moe(streams, layer, mods, norm_g, w_r, b_r, w_gu, b_gu, w_down, b_down, final_g):
    tm = EXPERT_TILE
    w_rt = w_r.T
    b_col = b_r.reshape(N_EXPERTS, 1)
    tok = np.arange(ROW_TILE)
    tri = jnp.asarray(tok[:, None] < tok[None, :], BF16)
    n_total = sum(x.shape[0] for x, _ in streams)
    cnt = jnp.zeros((N_EXPERTS, LANE), F32)
    h, infos, g_start = None, [], 0
    for s, (x, row_fn) in enumerate(streams):
        h, info, cnt = _router(x, norm_g, mods, row_fn, w_rt, b_col, tri, cnt, h, n_total, g_start,
                               name=f"moe_router_{s}")
        infos.append(info)
        g_start += x.shape[0]
    info = jnp.concatenate(infos, axis=1) if len(infos) > 1 else infos[0]
    counts = cnt[:, 0].astype(jnp.int32)
    padded = ((counts + tm - 1) // tm) * tm
    ends = jnp.cumsum(padded)
    offsets = ends - padded
    n_slots = n_total * TOP_K + N_EXPERTS * tm
    n_active = (ends[-1] // tm).astype(jnp.int32).reshape(1)
    tile_start = jnp.minimum(jnp.arange(n_slots // tm, dtype=jnp.int32), n_active[0] - 1) * tm
    tile_expert = jnp.sum((ends[None, :] <= tile_start[:, None]).astype(jnp.int32), axis=1)
    tile_expert = jnp.minimum(tile_expert, N_EXPERTS - 1)
    eid = info[INFO_ID:INFO_ID + TOP_K].astype(jnp.int32)
    rank = info[INFO_RANK:INFO_RANK + TOP_K].astype(jnp.int32)
    base = jnp.zeros_like(eid)
    for e in range(N_EXPERTS):
        base = jnp.where(eid == e, offsets[e], base)
    pos = (base + rank).T
    wts = info[INFO_W:INFO_W + TOP_K].T

    slots = _dispatch(pos, h, n_slots)
    slots = _padfill(slots, (offsets + counts).astype(jnp.int32), ends.astype(jnp.int32))
    y = _experts(slots, tile_expert, n_active, layer, w_gu, b_gu, w_down, b_down)
    outs, g_start = [], 0
    for s, (x, row_fn) in enumerate(streams):
        outs.append(_combine(x, g_start, pos, wts, mods, row_fn, y, final_g, name=f"moe_combine_{s}"))
        g_start += x.shape[0]
    return outs


def _pad_heads(w):
    lead = w.shape[:-1]
    w4 = w.reshape(*lead, GLA_HEADS, GLA_DK)
    w4 = jnp.pad(w4, [(0, 0)] * len(lead) + [(0, 0), (0, HEAD_PAD - GLA_DK)])
    return w4.reshape(*lead, QK_PAD)


def _pack_gla_in(w_in):
    d = w_in.shape[0]
    lr = jnp.pad(w_in[:, OFF_AF:OFF_AB + GLA_LOWRANK], ((0, 0), (0, LANE - 2 * GLA_LOWRANK)))
    return jnp.concatenate([
        _pad_heads(w_in[:, OFF_Q:OFF_Q + GLA_QK]),
        _pad_heads(w_in[:, OFF_K:OFF_K + GLA_QK]),
        w_in[:, OFF_V:OFF_V + GLA_V],
        w_in[:, OFF_G:OFF_G + GLA_V],
        w_in[:, OFF_GLU:OFF_GLU + 2 * CONV_CH],
        lr,
    ], axis=1).astype(BF16)


def _pack_decay(wa_f, ba_f, wa_b, ba_b):
    wa = jnp.zeros((LANE, 2 * QK_PAD), F32)
    wa = wa.at[0:GLA_LOWRANK, 0:QK_PAD].set(_pad_heads(wa_f))
    wa = wa.at[GLA_LOWRANK:2 * GLA_LOWRANK, QK_PAD:].set(_pad_heads(wa_b))
    ba = jnp.concatenate([_pad_heads(ba_f), _pad_heads(ba_b)]).reshape(1, 2 * QK_PAD)
    return wa.astype(BF16), ba


def _rope_tables(t_len):
    t = jnp.arange(t_len)
    row_pos = (t // GRID_W).astype(F32)
    col_pos = (t % GRID_W).astype(F32)
    half = GLA_DK // 4
    inv_freq = ROPE_BASE ** (-jnp.arange(half, dtype=F32) / half)
    dim = np.arange(HEAD_PAD)
    real = dim < GLA_DK
    use_col = (dim % GLA_DK) >= GLA_DK // 2
    first = (dim % (GLA_DK // 2)) < half
    pos = jnp.where(jnp.asarray(use_col)[None, :], col_pos[:, None], row_pos[:, None])
    ang = pos * inv_freq[dim % half][None, :]
    cos = jnp.where(jnp.asarray(real)[None, :], jnp.cos(ang), 0.0)
    sin = jnp.where(jnp.asarray(real)[None, :], jnp.sin(ang), 0.0)
    sin = jnp.where(jnp.asarray(first)[None, :], -sin, sin)
    return jnp.tile(cos, (1, GLA_HEADS)), jnp.tile(sin, (1, GLA_HEADS))


def kernel(x, c, ctx, c_ctx, ada_w, ada_b, norm1_g, norm2_g, gla_conv_w_in, gla_wa_fwd, gla_ba_fwd,
           gla_wa_bwd, gla_ba_bwd, gla_norm_g, conv_dw_w, conv_dw_b, conv_ln_g, conv_ln_b,
           gla_conv_w_out, na_w_qkv, na_rpb, na_w_out, router_w, router_b, expert_w_gu, expert_b_gu,
           expert_w_down, expert_b_down, final_norm_g):
    b, t_len, d = x.shape
    l_ctx = ctx.shape[1]
    depth = ada_w.shape[0]
    assert b + 1 <= MOD_ROWS and t_len % COPY_TILE == 0 and l_ctx % ROW_TILE == 0
    assert (b * l_ctx) % COPY_TILE == 0

    c_all = jnp.concatenate([c, c_ctx[None, :], jnp.zeros((MOD_ROWS - b - 1, d), F32)], axis=0)
    mods_all = _mods(c_all, ada_w, ada_b)

    x_lat = x.reshape(b * t_len, d)
    x_ctx = ctx.reshape(b * l_ctx, d)
    tm_lat = 512
    tm_ctx = min(512, l_ctx)

    def lat_row(i, tm=tm_lat):
        return (i * tm) // t_len

    def ctx_row(i):
        return b

    def lat_token_row(t):
        return t // t_len

    def ctx_token_row(t):
        return b

    for layer in range(depth):
        last = layer == depth - 1
        j = layer // 2
        mods = mods_all[layer]
        if layer % 2 == 0:
            w_pack = _pack_gla_in(gla_conv_w_in[j])
            wa, ba = _pack_decay(gla_wa_fwd[j], gla_ba_fwd[j], gla_wa_bwd[j], gla_ba_bwd[j])
            w_out = gla_conv_w_out[j].astype(BF16)
            zero_state = jnp.zeros((b, GLA_HEADS, GLA_DV, HEAD_PAD), F32)
            p_c = _norm_mod_matmul(x_ctx, norm1_g[layer], mods, ctx_row, (1, 0), w_pack, tm=tm_ctx,
                                   gla_layout=True, name="gla_in_ctx")
            y_gla_c, s_f, s_b = _gla(p_c, l_ctx, wa, ba, gla_norm_g[j], zero_state, zero_state)
            p_l = _norm_mod_matmul(x_lat, norm1_g[layer], mods, lat_row, (1, 0), w_pack, tm=tm_lat,
                                   gla_layout=True, rope=_rope_tables(t_len), name="gla_in_lat")
            y_gla_l, _, _ = _gla(p_l, t_len, wa, ba, gla_norm_g[j], s_f, s_b)
            y_conv_l = _conv(p_l, t_len, conv_dw_w[j], conv_dw_b[j], conv_ln_g[j], conv_ln_b[j])
            x_lat = _matmul_residual([y_gla_l, y_conv_l], w_out, x_lat, mods, lat_row, 2, tm=tm_lat,
                                     name="mix_out_lat")
            if not last:
                y_conv_c = _conv(p_c, l_ctx, conv_dw_w[j], conv_dw_b[j], conv_ln_g[j], conv_ln_b[j])
                x_ctx = _matmul_residual([y_gla_c, y_conv_c], w_out, x_ctx, mods, ctx_row, 2, tm=tm_ctx,
                                         name="mix_out_ctx")
        else:
            w_qkv = na_w_qkv[j].astype(BF16)
            kv_c = _norm_mod_matmul(x_ctx, norm1_g[layer], mods, ctx_row, (1, 0), w_qkv[:, NA_WIDTH:],
                                    tm=tm_ctx, name="na_kv_ctx")
            qkv = _norm_mod_matmul(x_lat, norm1_g[layer], mods, lat_row, (1, 0), w_qkv, tm=tm_lat,
                                   name="na_qkv_lat")
            o_l = _neighbourhood_attention(qkv, kv_c, na_rpb[j], t_len, l_ctx)
            x_lat = _matmul_residual([o_l], na_w_out[j].astype(BF16), x_lat, mods, lat_row, 2, tm=tm_lat,
                                     name="na_out_lat")
            if not last:
                raise NotImplementedError("context output of an attention layer is only needed mid-stack")
        streams = [(x_lat, lat_token_row)]
        if not last:
            streams.append((x_ctx, ctx_token_row))
        outs = _moe(streams, layer, mods, norm2_g[layer], router_w[layer], router_b[layer],
                    expert_w_gu, expert_b_gu, expert_w_down, expert_b_down, final_norm_g if last else None)
        x_lat = outs[0]
        if not last:
            x_ctx = outs[1]
    return x_lat.reshape(b, t_len, d)
```

```python
import functools

import numpy as np
import jax
import jax.numpy as jnp
from jax import lax
from jax.experimental import pallas as pl
from jax.experimental.pallas import tpu as pltpu

F32 = jnp.float32
BF16 = jnp.bfloat16
HIGHEST = lax.Precision.HIGHEST

NORM_EPS = 1e-6
ROPE_BASE = 10000.0
GRID_W = 64

GLA_HEADS = 4
GLA_DK = 64
GLA_DV = 128
GLA_LOWRANK = 16
GLA_TEMP = 16.0
GLA_CHUNK = 64
GLA_QK = GLA_HEADS * GLA_DK
GLA_V = GLA_HEADS * GLA_DV
CONV_CH = 512
CONV_WIDTH = 31
NA_HEADS = 16
NA_HEAD_DIM = 64
NA_WIDTH = NA_HEADS * NA_HEAD_DIM
NA_WIN_R = 8
NA_WIN_C = 16
N_EXPERTS = 32
TOP_K = 4
D_EXPERT = 1024
SWIGLU_LIMIT = 7.0
SWIGLU_ALPHA = 1.702

OFF_Q = 0
OFF_G = OFF_Q + GLA_QK
OFF_GLU = OFF_G + GLA_V
OFF_K = OFF_GLU + 2 * CONV_CH
OFF_V = OFF_K + GLA_QK
OFF_AF = OFF_V + GLA_V
OFF_AB = OFF_AF + GLA_LOWRANK

LANE = 128
VMEM_LIMIT = 56 * 1024 * 1024

HEAD_PAD = LANE
QK_PAD = GLA_HEADS * HEAD_PAD
PK_Q, PK_K, PK_V, PK_G, PK_A, PK_GT, PK_LR = 0, 512, 1024, 1536, 2048, 2560, 3072
PK_WIDTH = PK_LR + LANE

MOD_ROWS = 40
NEG_BIG = -1e30
ROW_TILE = 512
COPY_TILE = 1024
EXPERT_TILE = 512
ISSUE_UNROLL = 8
EXPERT_SPLIT = 2


def _cparams(*sem):
    return pltpu.CompilerParams(dimension_semantics=sem, vmem_limit_bytes=VMEM_LIMIT)


def _norm_mod(x, g, scale, shift):
    y = x * lax.rsqrt(jnp.mean(x * x, axis=-1, keepdims=True) + NORM_EPS) * g
    return y * (1.0 + scale) + shift


def _sigmoid(x):
    return 1.0 / (1.0 + jnp.exp(-x))


def _mods_kernel(c_ref, w_ref, b_ref, o_ref):
    c = c_ref[...]
    s = c * _sigmoid(c)
    o_ref[...] = jnp.dot(s, w_ref[...], precision=HIGHEST, preferred_element_type=F32) + b_ref[...]


def _mods(c_all, ada_w, ada_b):
    depth, d, n6 = ada_w.shape
    nb = 512
    out = pl.pallas_call(
        _mods_kernel,
        out_shape=jax.ShapeDtypeStruct((depth, MOD_ROWS, n6), F32),
        grid=(depth, n6 // nb),
        in_specs=[
            pl.BlockSpec((MOD_ROWS, d), lambda l, j: (0, 0)),
            pl.BlockSpec((None, d, nb), lambda l, j: (l, 0, j)),
            pl.BlockSpec((None, 1, nb), lambda l, j: (l, 0, j)),
        ],
        out_specs=pl.BlockSpec((None, MOD_ROWS, nb), lambda l, j: (l, 0, j)),
        compiler_params=_cparams("arbitrary", "arbitrary"),
        name="adaln_mods",
    )(c_all, ada_w, ada_b.reshape(depth, 1, n6))
    return out.reshape(depth, MOD_ROWS * 6, 1, d)


def _mod_spec(row_fn, chunk, d):
    return pl.BlockSpec((None, 1, d), lambda i, *_: (row_fn(i) * 6 + chunk, 0, 0))


def _nmm_kernel(x_ref, g_ref, sc_ref, sh_ref, w_ref, *rest, gla_layout, rope):
    if rope:
        cos_ref, sin_ref, o_ref = rest
    else:
        (o_ref,) = rest
    h = _norm_mod(x_ref[...], g_ref[...], sc_ref[...], sh_ref[...]).astype(BF16)
    nout = o_ref.shape[1]
    chunk = 512
    for j0 in range(0, nout, chunk):
        j1 = min(j0 + chunk, nout)
        acc = jnp.dot(h, w_ref[:, j0:j1], preferred_element_type=F32)
        if gla_layout and j0 in (PK_Q, PK_K):
            if rope:
                lane = lax.broadcasted_iota(jnp.int32, acc.shape, 1)
                first = (lane % 32) < 16
                rot = jnp.where(first, pltpu.roll(acc, QK_PAD - 16, 1), pltpu.roll(acc, 16, 1))
                acc = acc * cos_ref[...] + rot * sin_ref[...]
            if j0 == PK_Q:
                acc = acc * (GLA_DK ** -0.5)
        o_ref[:, j0:j1] = acc.astype(o_ref.dtype)


def _norm_mod_matmul(x, g, mods, row_fn, chunks, w, *, tm, gla_layout=False, rope=None, name):
    n, d = x.shape
    nout = w.shape[1]
    in_specs = [
        pl.BlockSpec((tm, d), lambda i: (i, 0)),
        pl.BlockSpec((1, d), lambda i: (0, 0)),
        _mod_spec(row_fn, chunks[0], d),
        _mod_spec(row_fn, chunks[1], d),
        pl.BlockSpec((d, nout), lambda i: (0, 0)),
    ]
    args = [x, g.reshape(1, d), mods, mods, w]
    if rope is not None:
        cos_t, sin_t = rope
        t_tiles = cos_t.shape[0] // tm
        in_specs += [pl.BlockSpec((tm, QK_PAD), lambda i: (i % t_tiles, 0))] * 2
        args += [cos_t, sin_t]
    return pl.pallas_call(
        functools.partial(_nmm_kernel, gla_layout=gla_layout, rope=rope is not None),
        out_shape=jax.ShapeDtypeStruct((n, nout), BF16),
        grid=(n // tm,),
        in_specs=in_specs,
        out_specs=pl.BlockSpec((tm, nout), lambda i: (i, 0)),
        compiler_params=_cparams("arbitrary"),
        name=name,
    )(*args)


def _mmres_kernel(*refs, n_parts):
    a_refs = refs[:n_parts]
    w_ref, x_ref, gate_ref, o_ref = refs[n_parts:]
    acc = None
    k0 = 0
    for a_ref in a_refs:
        kk = a_ref.shape[1]
        t = jnp.dot(a_ref[...], w_ref[k0:k0 + kk, :], preferred_element_type=F32)
        acc = t if acc is None else acc + t
        k0 += kk
    o_ref[...] = x_ref[...] + gate_ref[...] * acc


def _matmul_residual(parts, w, x, mods, row_fn, gate_chunk, *, tm, name):
    n, d = x.shape
    in_specs = [pl.BlockSpec((tm, a.shape[1]), lambda i: (i, 0)) for a in parts]
    in_specs += [
        pl.BlockSpec(w.shape, lambda i: (0, 0)),
        pl.BlockSpec((tm, d), lambda i: (i, 0)),
        _mod_spec(row_fn, gate_chunk, d),
    ]
    return pl.pallas_call(
        functools.partial(_mmres_kernel, n_parts=len(parts)),
        out_shape=jax.ShapeDtypeStruct((n, d), F32),
        grid=(n // tm,),
        in_specs=in_specs,
        out_specs=pl.BlockSpec((tm, d), lambda i: (i, 0)),
        compiler_params=_cparams("arbitrary"),
        name=name,
    )(*parts, w, x, mods)


def _gla_kernel(q_ref, k_ref, v_ref, g_ref, a_ref, wa_ref, ba_ref, ng_ref, s0f_ref, s0b_ref,
                y_ref, sf_ref, sb_ref, of_ref, ob_ref):
    t_len = q_ref.shape[0]
    n_chunks = t_len // GLA_CHUNK
    c = GLA_CHUNK
    sf_ref[...] = s0f_ref[...]
    sb_ref[...] = s0b_ref[...]

    row = lax.broadcasted_iota(jnp.int32, (c, c), 0)
    col = lax.broadcasted_iota(jnp.int32, (c, c), 1)
    row3 = lax.broadcasted_iota(jnp.int32, (c, 3 * c), 0)
    col3 = lax.broadcasted_iota(jnp.int32, (c, 3 * c), 1) % c
    lane = lax.broadcasted_iota(jnp.int32, (1, QK_PAD), 1)
    real_lane = ((lane % HEAD_PAD) < GLA_DK).astype(F32)

    def dir_step(ci, forward):
        r0 = pl.multiple_of(ci * c, c)
        rows = pl.ds(r0, c)
        off = 0 if forward else QK_PAD
        z = jnp.dot(a_ref[rows, :], wa_ref[:, off:off + QK_PAD], preferred_element_type=F32)
        z = z + ba_ref[:, off:off + QK_PAD]
        log_sig = jnp.minimum(z, 0.0) - jnp.log1p(jnp.exp(-jnp.abs(z)))
        la = log_sig * (1.0 / GLA_TEMP) * real_lane
        keep = (row >= col) if forward else (row <= col)
        keep3 = (row3 >= col3) if forward else (row3 <= col3)
        la_hi = la.astype(BF16)
        rem = la - la_hi.astype(F32)
        la_mid = rem.astype(BF16)
        la_lo = (rem - la_mid.astype(F32)).astype(BF16)
        bc = jnp.dot(keep3.astype(BF16), jnp.concatenate([la_hi, la_mid, la_lo], axis=0),
                     preferred_element_type=F32)
        b_last = bc[c - 1:c, :] if forward else bc[0:1, :]
        q = q_ref[rows, :].astype(F32)
        k = k_ref[rows, :].astype(F32)
        v = v_ref[rows, :]
        qt = (q * jnp.exp(bc)).astype(BF16)
        kt = (k * jnp.exp(-bc)).astype(BF16)
        kd = (k * jnp.exp(b_last - bc)).astype(BF16)
        decay = jnp.exp(b_last)
        s_ref = sf_ref if forward else sb_ref
        o_ref = of_ref if forward else ob_ref
        for h in range(GLA_HEADS):
            sl = slice(h * HEAD_PAD, (h + 1) * HEAD_PAD)
            att = lax.dot_general(qt[:, sl], kt[:, sl], (((1,), (1,)), ((), ())),
                                  preferred_element_type=F32)
            att = jnp.where(keep, att, 0.0).astype(BF16)
            st = s_ref[0, h]
            o = jnp.dot(att, v[:, sl], preferred_element_type=F32)
            o = o + lax.dot_general(qt[:, sl], st.astype(BF16), (((1,), (1,)), ((), ())),
                                    preferred_element_type=F32)
            kv_t = lax.dot_general(v[:, sl], kd[:, sl], (((0,), (0,)), ((), ())),
                                   preferred_element_type=F32)
            s_ref[0, h] = st * decay[:, sl] + kv_t
            o_ref[rows, sl] = o

    def step(i, carry):
        dir_step(i, True)
        dir_step(n_chunks - 1 - i, False)
        return carry

    lax.fori_loop(0, n_chunks, step, 0)

    blk = min(256, t_len)

    def post(i, carry):
        rows = pl.ds(pl.multiple_of(i * blk, blk), blk)
        o = of_ref[rows, :] + ob_ref[rows, :]
        g = g_ref[rows, :].astype(F32)
        gate = g * _sigmoid(g)
        for h in range(GLA_HEADS):
            sl = slice(h * GLA_DV, (h + 1) * GLA_DV)
            oh = o[:, sl]
            oh = oh * lax.rsqrt(jnp.mean(oh * oh, axis=-1, keepdims=True) + NORM_EPS)
            y_ref[rows, sl] = (oh * ng_ref[:, sl] * gate[:, sl]).astype(y_ref.dtype)
        return carry

    lax.fori_loop(0, t_len // blk, post, 0)


def _gla(p, t_len, wa, ba, norm_g, s0f, s0b):
    n = p.shape[0]
    b = n // t_len

    def col(width, start):
        return pl.BlockSpec((t_len, width), lambda i: (i, start // width))

    st_spec = pl.BlockSpec((1, GLA_HEADS, GLA_DV, HEAD_PAD), lambda i: (i, 0, 0, 0))
    st_shape = jax.ShapeDtypeStruct((b, GLA_HEADS, GLA_DV, HEAD_PAD), F32)
    return pl.pallas_call(
        _gla_kernel,
        out_shape=(jax.ShapeDtypeStruct((n, GLA_V), BF16), st_shape, st_shape),
        grid=(b,),
        in_specs=[
            col(QK_PAD, PK_Q), col(QK_PAD, PK_K), col(GLA_V, PK_V), col(GLA_V, PK_G), col(LANE, PK_LR),
            pl.BlockSpec(wa.shape, lambda i: (0, 0)),
            pl.BlockSpec(ba.shape, lambda i: (0, 0)),
            pl.BlockSpec((1, GLA_V), lambda i: (0, 0)),
            st_spec, st_spec,
        ],
        out_specs=(pl.BlockSpec((t_len, GLA_V), lambda i: (i, 0)), st_spec, st_spec),
        scratch_shapes=[pltpu.VMEM((t_len, GLA_V), F32), pltpu.VMEM((t_len, GLA_V), F32)],
        compiler_params=_cparams("arbitrary"),
        name="gla_scan",
    )(p, p, p, p, p, wa, ba, norm_g.reshape(1, GLA_V), s0f, s0b)


CONV_HALO = 16
CONV_ROWS = 64


def _conv_kernel(a_ref, gt_ref, w_ref, cb_ref, lg_ref, lb_ref, y_ref, u_ref):
    t_len = a_ref.shape[0]
    zeros = jnp.zeros((CONV_HALO, CONV_CH), F32)
    u_ref[0:CONV_HALO, :] = zeros
    u_ref[CONV_HALO + t_len:2 * CONV_HALO + t_len, :] = zeros
    blk = min(256, t_len)

    def glu(i, carry):
        r0 = pl.multiple_of(i * blk, blk)
        a = a_ref[pl.ds(r0, blk), :].astype(F32)
        gt = gt_ref[pl.ds(r0, blk), :].astype(F32)
        u_ref[pl.ds(r0 + CONV_HALO, blk), :] = a * _sigmoid(gt)
        return carry

    lax.fori_loop(0, t_len // blk, glu, 0)
    shift = CONV_HALO - CONV_WIDTH // 2
    win_rows = CONV_ROWS + 2 * CONV_HALO

    def tile(i, carry):
        r0 = pl.multiple_of(i * CONV_ROWS, CONV_ROWS)
        parts = []
        for lb in range(CONV_CH // LANE):
            ls = slice(lb * LANE, (lb + 1) * LANE)
            win = u_ref[pl.ds(r0, win_rows), ls]
            acc = jnp.zeros((CONV_ROWS, LANE), F32)
            for b in range(8):
                wb = win if b == 0 else pltpu.roll(win, win_rows - b, 0)
                for a in range(win_rows // 8):
                    j = 8 * a + b - shift
                    if 0 <= j < CONV_WIDTH:
                        acc = acc + wb[8 * a:8 * a + CONV_ROWS, :] * w_ref[j:j + 1, ls]
            parts.append(acc)
        y = jnp.concatenate(parts, axis=1) + cb_ref[...]
        mu = jnp.mean(y, axis=-1, keepdims=True)
        yc = y - mu
        var = jnp.mean(yc * yc, axis=-1, keepdims=True)
        yn = yc * lax.rsqrt(var + NORM_EPS) * lg_ref[...] + lb_ref[...]
        y_ref[pl.ds(r0, CONV_ROWS), :] = (yn * _sigmoid(yn)).astype(y_ref.dtype)
        return carry

    lax.fori_loop(0, t_len // CONV_ROWS, tile, 0)


def _conv(p, t_len, conv_w, conv_b, ln_g, ln_b):
    n = p.shape[0]
    w_pad = jnp.pad(conv_w, ((0, 32 - CONV_WIDTH), (0, 0)))
    vec = pl.BlockSpec((1, CONV_CH), lambda i: (0, 0))
    return pl.pallas_call(
        _conv_kernel,
        out_shape=jax.ShapeDtypeStruct((n, CONV_CH), BF16),
        grid=(n // t_len,),
        in_specs=[
            pl.BlockSpec((t_len, CONV_CH), lambda i: (i, PK_A // CONV_CH)),
            pl.BlockSpec((t_len, CONV_CH), lambda i: (i, PK_GT // CONV_CH)),
            pl.BlockSpec((32, CONV_CH), lambda i: (0, 0)),
            vec, vec, vec,
        ],
        out_specs=pl.BlockSpec((t_len, CONV_CH), lambda i: (i, 0)),
        scratch_shapes=[pltpu.VMEM((t_len + 2 * CONV_HALO, CONV_CH), F32)],
        compiler_params=_cparams("arbitrary"),
        name="conformer_conv",
    )(p, p, w_pad, conv_b.reshape(1, -1), ln_g.reshape(1, -1), ln_b.reshape(1, -1))


NA_QROWS = 4
NA_BAND = 12


def _na_geometry(rows):
    assert rows >= NA_BAND and rows % NA_QROWS == 0
    win_r = min(NA_WIN_R, rows)
    starts, classes, sigs = [], [], []
    for rb in range(rows // NA_QROWS):
        bs = int(np.clip(rb * NA_QROWS - NA_WIN_R // 2, 0, rows - NA_BAND))
        sig = tuple((rb * NA_QROWS + i - bs,
                     int(np.clip(rb * NA_QROWS + i - win_r // 2, 0, rows - win_r)) - bs)
                    for i in range(NA_QROWS))
        if sig not in sigs:
            sigs.append(sig)
        starts.append(bs)
        classes.append(sigs.index(sig))
    return starts, classes, sigs, win_r


def _na_bias(rpb, sigs, win_r):
    w = GRID_W
    cols = np.arange(w)
    c_start = np.clip(cols - NA_WIN_C // 2, 0, w - NA_WIN_C)
    col_ok = (cols[None, :] >= c_start[:, None]) & (cols[None, :] < c_start[:, None] + NA_WIN_C)
    dc = np.clip(cols[None, :] - cols[:, None] + NA_WIN_C - 1, 0, 2 * NA_WIN_C - 2)
    rpb_col = jnp.take(rpb.astype(F32), jnp.asarray(dc.reshape(-1)), axis=2)
    rpb_col = rpb_col.reshape(NA_HEADS, 2 * NA_WIN_R - 1, w, w)
    rpb_col = jnp.where(jnp.asarray(col_ok)[None, None], rpb_col, NEG_BIG)
    kj = np.arange(NA_BAND)
    out = []
    for sig in sigs:
        q_rel = np.array([s[0] for s in sig])
        r_rel = np.array([s[1] for s in sig])
        row_ok = (kj[None, :] >= r_rel[:, None]) & (kj[None, :] < r_rel[:, None] + win_r)
        dr = np.clip(kj[None, :] - q_rel[:, None] + NA_WIN_R - 1, 0, 2 * NA_WIN_R - 2)
        bias = jnp.take(rpb_col, jnp.asarray(dr.reshape(-1)), axis=1)
        bias = bias.reshape(NA_HEADS, NA_QROWS, NA_BAND, w, w)
        bias = jnp.where(jnp.asarray(row_ok)[None, :, :, None, None], bias, NEG_BIG)
        out.append(bias.transpose(0, 1, 3, 2, 4).reshape(NA_HEADS, NA_QROWS * w, NA_BAND * w))
    return jnp.stack(out, axis=0)


def _na_kernel(q_ref, k_ref, v_ref, kc_ref, vc_ref, bias_ref, o_ref, *, starts, classes):
    w = GRID_W
    scale = NA_HEAD_DIM ** -0.5
    nq = NA_QROWS * w
    kc = kc_ref[...]
    vc = vc_ref[...]
    nt = (((1,), (1,)), ((), ()))

    def head_lanes(rows, hh):
        lane = lax.broadcasted_iota(jnp.int32, (rows, LANE), 1)
        return (lane >= hh * NA_HEAD_DIM) & (lane < (hh + 1) * NA_HEAD_DIM)

    vc_h = [jnp.where(head_lanes(vc.shape[0], hh), vc, jnp.ones_like(vc)) for hh in range(2)]
    for rb, (bs, cls) in enumerate(zip(starts, classes)):
        q2 = q_ref[rb * nq:(rb + 1) * nq, :]
        kb = k_ref[bs * w:(bs + NA_BAND) * w, :]
        vb = v_ref[bs * w:(bs + NA_BAND) * w, :]
        acc = None
        for hh in range(2):
            mine = head_lanes(nq, hh)
            qh = jnp.where(mine, q2, jnp.zeros_like(q2)) * scale
            s_loc = lax.dot_general(qh, kb, nt, preferred_element_type=F32) + bias_ref[cls, hh]
            s_ctx = lax.dot_general(qh, kc, nt, preferred_element_type=F32)
            m = jnp.maximum(jnp.max(s_loc, axis=-1, keepdims=True), jnp.max(s_ctx, axis=-1, keepdims=True))
            p_loc = jnp.exp((s_loc - m).astype(BF16))
            p_ctx = jnp.exp((s_ctx - m).astype(BF16))
            vb_h = jnp.where(head_lanes(vb.shape[0], hh), vb, jnp.ones_like(vb))
            o = jnp.dot(p_loc, vb_h, preferred_element_type=F32)
            o = o + jnp.dot(p_ctx, vc_h[hh], preferred_element_type=F32)
            o = o / pltpu.roll(o, NA_HEAD_DIM, 1)
            acc = o if acc is None else jnp.where(mine, o, acc)
        o_ref[rb * nq:(rb + 1) * nq, :] = acc.astype(o_ref.dtype)


def _neighbourhood_attention(qkv, kv_ctx, rpb, t_len, l_ctx):
    n = qkv.shape[0]
    b = n // t_len
    rows = t_len // GRID_W
    starts, classes, sigs, win_r = _na_geometry(rows)
    bias = _na_bias(rpb, sigs, win_r)
    n_pairs = NA_HEADS // 2
    nq, nk = NA_QROWS * GRID_W, NA_BAND * GRID_W
    return pl.pallas_call(
        functools.partial(_na_kernel, starts=starts, classes=classes),
        out_shape=jax.ShapeDtypeStruct((n, NA_WIDTH), BF16),
        grid=(n_pairs, b),
        in_specs=[
            pl.BlockSpec((t_len, LANE), lambda j, i: (i, j)),
            pl.BlockSpec((t_len, LANE), lambda j, i: (i, n_pairs + j)),
            pl.BlockSpec((t_len, LANE), lambda j, i: (i, 2 * n_pairs + j)),
            pl.BlockSpec((l_ctx, LANE), lambda j, i: (i, j)),
            pl.BlockSpec((l_ctx, LANE), lambda j, i: (i, n_pairs + j)),
            pl.BlockSpec((len(sigs), 2, nq, nk), lambda j, i: (0, j, 0, 0)),
        ],
        out_specs=pl.BlockSpec((t_len, LANE), lambda j, i: (i, j)),
        compiler_params=_cparams("arbitrary", "arbitrary"),
        name="neighbourhood_attention",
    )(qkv, qkv, qkv, kv_ctx, kv_ctx, bias)


INFO_ID, INFO_W, INFO_RANK = 0, TOP_K, 2 * TOP_K
INFO_ROWS = 16


def _router_kernel(x_ref, g_ref, sc_ref, sh_ref, wrt_ref, br_ref, tri_ref, cnt0_ref, *rest):
    h_ref, info_ref, cnt_ref = rest[-3:]
    tm = x_ref.shape[0]

    @pl.when(pl.program_id(0) == 0)
    def _():
        cnt_ref[...] = cnt0_ref[...]

    h = _norm_mod(x_ref[...], g_ref[...], sc_ref[...], sh_ref[...])
    h_ref[...] = h
    logits = lax.dot_general(wrt_ref[...], h, (((1,), (1,)), ((), ())), precision=HIGHEST,
                             preferred_element_type=F32) + br_ref[...]
    sub = lax.broadcasted_iota(jnp.int32, (N_EXPERTS, tm), 0)
    cur = logits
    vals, ids = [], []
    for _ in range(TOP_K):
        m = jnp.max(cur, axis=0, keepdims=True)
        idx = jnp.min(jnp.where(cur == m, sub, N_EXPERTS), axis=0, keepdims=True)
        vals.append(m)
        ids.append(idx)
        cur = jnp.where(sub == idx, NEG_BIG, cur)
    ex = [jnp.exp(v - vals[0]) for v in vals]
    den = ex[0] + ex[1] + ex[2] + ex[3]
    onehot = jnp.zeros((N_EXPERTS, tm), F32)
    for idx in ids:
        onehot = onehot + (sub == idx).astype(F32)
    before = jnp.dot(onehot.astype(BF16), tri_ref[...], preferred_element_type=F32)
    running = cnt_ref[:, 0:1]
    base = running + before
    row = lax.broadcasted_iota(jnp.int32, (INFO_ROWS, tm), 0)
    info = jnp.zeros((INFO_ROWS, tm), F32)
    for k in range(TOP_K):
        rank = jnp.sum(jnp.where(sub == ids[k], base, 0.0), axis=0, keepdims=True)
        info = info + jnp.where(row == INFO_ID + k, ids[k].astype(F32), 0.0)
        info = info + jnp.where(row == INFO_W + k, ex[k] / den, 0.0)
        info = info + jnp.where(row == INFO_RANK + k, rank, 0.0)
    info_ref[...] = info
    cnt_ref[...] = jnp.broadcast_to(running + jnp.sum(onehot, axis=1, keepdims=True), cnt_ref.shape)


def _router(x, g, mods, row_fn, w_rt, b_r, tri, cnt0, h_prev, h_rows, h_start, *, name):
    n, d = x.shape
    tm = ROW_TILE
    ht0 = h_start // tm
    in_specs = [
        pl.BlockSpec((tm, d), lambda i: (i, 0)),
        pl.BlockSpec((1, d), lambda i: (0, 0)),
        _mod_spec(lambda i: row_fn(i * tm), 4, d),
        _mod_spec(lambda i: row_fn(i * tm), 3, d),
        pl.BlockSpec((N_EXPERTS, d), lambda i: (0, 0)),
        pl.BlockSpec((N_EXPERTS, 1), lambda i: (0, 0)),
        pl.BlockSpec((tm, tm), lambda i: (0, 0)),
        pl.BlockSpec((N_EXPERTS, LANE), lambda i: (0, 0)),
    ]
    args = [x, g.reshape(1, d), mods, mods, w_rt, b_r, tri, cnt0]
    aliases = {}
    if h_prev is not None:
        in_specs.append(pl.BlockSpec(memory_space=pl.ANY))
        args.append(h_prev)
        aliases = {len(args) - 1: 0}
    return pl.pallas_call(
        _router_kernel,
        out_shape=(jax.ShapeDtypeStruct((h_rows, d), F32),
                   jax.ShapeDtypeStruct((INFO_ROWS, n), F32),
                   jax.ShapeDtypeStruct((N_EXPERTS, LANE), F32)),
        grid=(n // tm,),
        in_specs=in_specs,
        out_specs=(pl.BlockSpec((tm, d), lambda i: (i + ht0, 0)),
                   pl.BlockSpec((INFO_ROWS, tm), lambda i: (0, i)),
                   pl.BlockSpec((N_EXPERTS, LANE), lambda i: (0, 0))),
        input_output_aliases=aliases,
        compiler_params=_cparams("arbitrary"),
        name=name,
    )(*args)


def _row_copy(src_ref, src_row, dst_ref, dst_row, sem):
    return pltpu.make_async_copy(src_ref.at[pl.ds(src_row, 1)], dst_ref.at[pl.ds(dst_row, 1)], sem)


def _dispatch_kernel(pos_ref, h_ref, xs_ref, sem):
    tm = h_ref.shape[0]

    def issue(rb, carry):
        base = pl.multiple_of(rb * ISSUE_UNROLL, ISSUE_UNROLL)
        for u in range(ISSUE_UNROLL):
            for k in range(TOP_K):
                _row_copy(h_ref, base + u, xs_ref, pos_ref[0, 0, (base + u) * TOP_K + k], sem).start()
        return carry

    lax.fori_loop(0, tm // ISSUE_UNROLL, issue, 0)
    for _ in range(TOP_K):
        pltpu.make_async_copy(h_ref, xs_ref.at[pl.ds(0, tm)], sem).wait()


def _dispatch(pos, h, n_slots):
    n, d = h.shape
    tm = COPY_TILE
    return pl.pallas_call(
        _dispatch_kernel,
        out_shape=jax.ShapeDtypeStruct((n_slots, d), F32),
        grid=(n // tm,),
        in_specs=[
            pl.BlockSpec((1, 1, tm * TOP_K), lambda i: (i, 0, 0), memory_space=pltpu.SMEM),
            pl.BlockSpec((tm, d), lambda i: (i, 0)),
        ],
        out_specs=pl.BlockSpec(memory_space=pl.ANY),
        scratch_shapes=[pltpu.SemaphoreType.DMA],
        compiler_params=_cparams("arbitrary"),
        name="moe_dispatch",
    )(pos.reshape(n // tm, 1, tm * TOP_K), h)


PAD_ROWS = 8


def _padfill_kernel(lo_ref, mid_ref, hi_ref, xs_in, xs_ref, zero_ref, sem):
    del xs_in
    zero_ref[...] = jnp.zeros(zero_ref.shape, F32)
    for phase in range(2):
        for e in range(N_EXPERTS):
            def row(r, carry):
                cp = _row_copy(zero_ref, 0, xs_ref, r, sem)
                cp.start() if phase == 0 else cp.wait()
                return carry

            def block(b, carry):
                r = pl.multiple_of(mid_ref[e] + b * PAD_ROWS, PAD_ROWS)
                cp = pltpu.make_async_copy(zero_ref, xs_ref.at[pl.ds(r, PAD_ROWS)], sem)
                cp.start() if phase == 0 else cp.wait()
                return carry

            lax.fori_loop(lo_ref[e], mid_ref[e], row, 0)
            lax.fori_loop(0, (hi_ref[e] - mid_ref[e]) // PAD_ROWS, block, 0)


def _padfill(xs, lo, hi):
    mid = jnp.minimum(((lo + PAD_ROWS - 1) // PAD_ROWS) * PAD_ROWS, hi)
    return pl.pallas_call(
        _padfill_kernel,
        out_shape=jax.ShapeDtypeStruct(xs.shape, F32),
        grid_spec=pltpu.PrefetchScalarGridSpec(
            num_scalar_prefetch=3,
            grid=(1,),
            in_specs=[pl.BlockSpec(memory_space=pl.ANY)],
            out_specs=pl.BlockSpec(memory_space=pl.ANY),
            scratch_shapes=[pltpu.VMEM((PAD_ROWS,) + xs.shape[1:], F32), pltpu.SemaphoreType.DMA],
        ),
        input_output_aliases={3: 0},
        compiler_params=_cparams("arbitrary"),
        name="moe_padfill",
    )(lo, mid, hi, xs)


def _expert_kernel(te_ref, na_ref, xs_ref, wgu_ref, bgu_ref, wd_ref, bd_ref, y_ref, wgu_bf, wd_bf):
    i = pl.program_id(0)
    active = i < na_ref[0]
    first_of_expert = (i == 0) | (te_ref[i] != te_ref[jnp.maximum(i - 1, 0)])

    @pl.when(active & first_of_expert)
    def _():
        rows = 256
        for r in range(0, wgu_ref.shape[0], rows):
            wgu_bf[r:r + rows, :] = wgu_ref[r:r + rows, :].astype(BF16)
        for r in range(0, wd_ref.shape[0], rows):
            wd_bf[r:r + rows, :] = wd_ref[r:r + rows, :].astype(BF16)

    @pl.when(active)
    def _():
        x = xs_ref[...].astype(BF16)
        f = wd_ref.shape[0]
        half = f // EXPERT_SPLIT
        y = None
        for lo in range(0, f, half):
            gt = jnp.dot(x, wgu_bf[:, lo:lo + half], preferred_element_type=F32) + bgu_ref[:, lo:lo + half]
            up = jnp.dot(x, wgu_bf[:, f + lo:f + lo + half], preferred_element_type=F32)
            up = up + bgu_ref[:, f + lo:f + lo + half]
            gt = jnp.minimum(gt, SWIGLU_LIMIT)
            up = jnp.clip(up, -SWIGLU_LIMIT, SWIGLU_LIMIT)
            act = (up + 1.0) * gt * _sigmoid(SWIGLU_ALPHA * gt)
            t = jnp.dot(act.astype(BF16), wd_bf[lo:lo + half, :], preferred_element_type=F32)
            y = t if y is None else y + t
        y_ref[...] = y + bd_ref[...]

    @pl.when(jnp.logical_not(active))
    def _():
        y_ref[...] = jnp.zeros(y_ref.shape, F32)


def _experts(xs, tile_expert, n_active, layer, w_gu, b_gu, w_down, b_down):
    n_slots, d = xs.shape
    tm = EXPERT_TILE
    depth, n_e, _, two_f = w_gu.shape
    return pl.pallas_call(
        _expert_kernel,
        out_shape=jax.ShapeDtypeStruct(xs.shape, F32),
        grid_spec=pltpu.PrefetchScalarGridSpec(
            num_scalar_prefetch=2,
            grid=(n_slots // tm,),
            in_specs=[
                pl.BlockSpec((tm, d), lambda i, te, na: (jnp.minimum(i, na[0] - 1), 0)),
                pl.BlockSpec((None, None, d, two_f), lambda i, te, na: (layer, te[i], 0, 0)),
                pl.BlockSpec((None, None, 1, two_f), lambda i, te, na: (layer, te[i], 0, 0)),
                pl.BlockSpec((None, None, two_f // 2, d), lambda i, te, na: (layer, te[i], 0, 0)),
                pl.BlockSpec((None, None, 1, d), lambda i, te, na: (layer, te[i], 0, 0)),
            ],
            out_specs=pl.BlockSpec((tm, d), lambda i, te, na: (i, 0)),
            scratch_shapes=[pltpu.VMEM((d, two_f), BF16), pltpu.VMEM((two_f // 2, d), BF16)],
        ),
        compiler_params=_cparams("arbitrary"),
        name="moe_experts",
    )(tile_expert, n_active, xs, w_gu, b_gu.reshape(depth, n_e, 1, two_f), w_down,
      b_down.reshape(depth, n_e, 1, d))


def _combine_kernel(pos_ref, x_ref, wts_ref, gate_ref, *rest, final):
    if final:
        fg_ref, y_hbm, o_ref, ybuf, sem = rest
    else:
        y_hbm, o_ref, ybuf, sem = rest
    tm = x_ref.shape[0]

    def issue(rb, carry):
        base = pl.multiple_of(rb * ISSUE_UNROLL, ISSUE_UNROLL)
        for u in range(ISSUE_UNROLL):
            for k in range(TOP_K):
                _row_copy(y_hbm, pos_ref[0, 0, (base + u) * TOP_K + k], ybuf.at[k], base + u, sem).start()
        return carry

    lax.fori_loop(0, tm // ISSUE_UNROLL, issue, 0)
    for k in range(TOP_K):
        pltpu.make_async_copy(y_hbm.at[pl.ds(0, tm)], ybuf.at[k], sem).wait()
    wts = wts_ref[...]
    f = wts[:, 0:1] * ybuf[0]
    for k in range(1, TOP_K):
        f = f + wts[:, k:k + 1] * ybuf[k]
    out = x_ref[...] + gate_ref[...] * f
    if final:
        out = out * lax.rsqrt(jnp.mean(out * out, axis=-1, keepdims=True) + NORM_EPS) * fg_ref[...]
    o_ref[...] = out


def _combine(x, g_start, pos, wts, mods, row_fn, y, final_g, *, name):
    n, d = x.shape
    tm = COPY_TILE
    gt0 = g_start // tm
    in_specs = [
        pl.BlockSpec((1, 1, tm * TOP_K), lambda i: (i + gt0, 0, 0), memory_space=pltpu.SMEM),
        pl.BlockSpec((tm, d), lambda i: (i, 0)),
        pl.BlockSpec((tm, TOP_K), lambda i: (i + gt0, 0)),
        _mod_spec(lambda i: row_fn(i * tm), 5, d),
    ]
    args = [pos.reshape(pos.shape[0] // tm, 1, tm * TOP_K), x, wts, mods]
    if final_g is not None:
        in_specs.append(pl.BlockSpec((1, d), lambda i: (0, 0)))
        args.append(final_g.reshape(1, d))
    in_specs.append(pl.BlockSpec(memory_space=pl.ANY))
    args.append(y)
    return pl.pallas_call(
        functools.partial(_combine_kernel, final=final_g is not None),
        out_shape=jax.ShapeDtypeStruct((n, d), F32),
        grid=(n // tm,),
        in_specs=in_specs,
        out_specs=pl.BlockSpec((tm, d), lambda i: (i, 0)),
        scratch_shapes=[pltpu.VMEM((TOP_K, tm) + y.shape[1:], F32), pltpu.SemaphoreType.DMA],
        compiler_params=_cparams("arbitrary"),
        name=name,
    )(*args)


def _moe(streams, layer, mods, norm_g, w_r, b_r, w_gu, b_gu, w_down, b_down, final_g):
    tm = EXPERT_TILE
    w_rt = w_r.T
    b_col = b_r.reshape(N_EXPERTS, 1)
    tok = np.arange(ROW_TILE)
    tri = jnp.asarray(tok[:, None] < tok[None, :], BF16)
    n_total = sum(x.shape[0] for x, _ in streams)
    cnt = jnp.zeros((N_EXPERTS, LANE), F32)
    h, infos, g_start = None, [], 0
    for s, (x, row_fn) in enumerate(streams):
        h, info, cnt = _router(x, norm_g, mods, row_fn, w_rt, b_col, tri, cnt, h, n_total, g_start,
                               name=f"moe_router_{s}")
        infos.append(info)
        g_start += x.shape[0]
    info = jnp.concatenate(infos, axis=1) if len(infos) > 1 else infos[0]
    counts = cnt[:, 0].astype(jnp.int32)
    padded = ((counts + tm - 1) // tm) * tm
    ends = jnp.cumsum(padded)
    offsets = ends - padded
    n_slots = n_total * TOP_K + N_EXPERTS * tm
    n_active = (ends[-1] // tm).astype(jnp.int32).reshape(1)
    tile_start = jnp.minimum(jnp.arange(n_slots // tm, dtype=jnp.int32), n_active[0] - 1) * tm
    tile_expert = jnp.sum((ends[None, :] <= tile_start[:, None]).astype(jnp.int32), axis=1)
    tile_expert = jnp.minimum(tile_expert, N_EXPERTS - 1)
    eid = info[INFO_ID:INFO_ID + TOP_K].astype(jnp.int32)
    rank = info[INFO_RANK:INFO_RANK + TOP_K].astype(jnp.int32)
    base = jnp.zeros_like(eid)
    for e in range(N_EXPERTS):
        base = jnp.where(eid == e, offsets[e], base)
    pos = (base + rank).T
    wts = info[INFO_W:INFO_W + TOP_K].T

    slots = _dispatch(pos, h, n_slots)
    slots = _padfill(slots, (offsets + counts).astype(jnp.int32), ends.astype(jnp.int32))
    y = _experts(slots, tile_expert, n_active, layer, w_gu, b_gu, w_down, b_down)
    outs, g_start = [], 0
    for s, (x, row_fn) in enumerate(streams):
        outs.append(_combine(x, g_start, pos, wts, mods, row_fn, y, final_g, name=f"moe_combine_{s}"))
        g_start += x.shape[0]
    return outs


def _pad_heads(w):
    lead = w.shape[:-1]
    w4 = w.reshape(*lead, GLA_HEADS, GLA_DK)
    w4 = jnp.pad(w4, [(0, 0)] * len(lead) + [(0, 0), (0, HEAD_PAD - GLA_DK)])
    return w4.reshape(*lead, QK_PAD)


def _pack_gla_in(w_in):
    d = w_in.shape[0]
    lr = jnp.pad(w_in[:, OFF_AF:OFF_AB + GLA_LOWRANK], ((0, 0), (0, LANE - 2 * GLA_LOWRANK)))
    return jnp.concatenate([
        _pad_heads(w_in[:, OFF_Q:OFF_Q + GLA_QK]),
        _pad_heads(w_in[:, OFF_K:OFF_K + GLA_QK]),
        w_in[:, OFF_V:OFF_V + GLA_V],
        w_in[:, OFF_G:OFF_G + GLA_V],
        w_in[:, OFF_GLU:OFF_GLU + 2 * CONV_CH],
        lr,
    ], axis=1).astype(BF16)


def _pack_decay(wa_f, ba_f, wa_b, ba_b):
    wa = jnp.zeros((LANE, 2 * QK_PAD), F32)
    wa = wa.at[0:GLA_LOWRANK, 0:QK_PAD].set(_pad_heads(wa_f))
    wa = wa.at[GLA_LOWRANK:2 * GLA_LOWRANK, QK_PAD:].set(_pad_heads(wa_b))
    ba = jnp.concatenate([_pad_heads(ba_f), _pad_heads(ba_b)]).reshape(1, 2 * QK_PAD)
    return wa.astype(BF16), ba


def _rope_tables(t_len):
    t = jnp.arange(t_len)
    row_pos = (t // GRID_W).astype(F32)
    col_pos = (t % GRID_W).astype(F32)
    half = GLA_DK // 4
    inv_freq = ROPE_BASE ** (-jnp.arange(half, dtype=F32) / half)
    dim = np.arange(HEAD_PAD)
    real = dim < GLA_DK
    use_col = (dim % GLA_DK) >= GLA_DK // 2
    first = (dim % (GLA_DK // 2)) < half
    pos = jnp.where(jnp.asarray(use_col)[None, :], col_pos[:, None], row_pos[:, None])
    ang = pos * inv_freq[dim % half][None, :]
    cos = jnp.where(jnp.asarray(real)[None, :], jnp.cos(ang), 0.0)
    sin = jnp.where(jnp.asarray(real)[None, :], jnp.sin(ang), 0.0)
    sin = jnp.where(jnp.asarray(first)[None, :], -sin, sin)
    return jnp.tile(cos, (1, GLA_HEADS)), jnp.tile(sin, (1, GLA_HEADS))


def kernel(x, c, ctx, c_ctx, ada_w, ada_b, norm1_g, norm2_g, gla_conv_w_in, gla_wa_fwd, gla_ba_fwd,
           gla_wa_bwd, gla_ba_bwd, gla_norm_g, conv_dw_w, conv_dw_b, conv_ln_g, conv_ln_b,
           gla_conv_w_out, na_w_qkv, na_rpb, na_w_out, router_w, router_b, expert_w_gu, expert_b_gu,
           expert_w_down, expert_b_down, final_norm_g):
    b, t_len, d = x.shape
    l_ctx = ctx.shape[1]
    depth = ada_w.shape[0]
    assert b + 1 <= MOD_ROWS and t_len % COPY_TILE == 0 and COPY_TILE % ROW_TILE == 0
    assert (b * l_ctx) % COPY_TILE == 0

    c_all = jnp.concatenate([c, c_ctx[None, :], jnp.zeros((MOD_ROWS - b - 1, d), F32)], axis=0)
    mods_all = _mods(c_all, ada_w, ada_b)

    x_lat = x.reshape(b * t_len, d)
    x_ctx = ctx.reshape(b * l_ctx, d)
    tm_lat = 512
    tm_ctx = min(512, l_ctx)

    def lat_row(i, tm=tm_lat):
        return (i * tm) // t_len

    def ctx_row(i):
        return b

    def lat_token_row(t):
        return t // t_len

    def ctx_token_row(t):
        return b

    for layer in range(depth):
        last = layer == depth - 1
        j = layer // 2
        mods = mods_all[layer]
        if layer % 2 == 0:
            w_pack = _pack_gla_in(gla_conv_w_in[j])
            wa, ba = _pack_decay(gla_wa_fwd[j], gla_ba_fwd[j], gla_wa_bwd[j], gla_ba_bwd[j])
            w_out = gla_conv_w_out[j].astype(BF16)
            zero_state = jnp.zeros((b, GLA_HEADS, GLA_DV, HEAD_PAD), F32)
            p_c = _norm_mod_matmul(x_ctx, norm1_g[layer], mods, ctx_row, (1, 0), w_pack, tm=tm_ctx,
                                   gla_layout=True, name="gla_in_ctx")
            y_gla_c, s_f, s_b = _gla(p_c, l_ctx, wa, ba, gla_norm_g[j], zero_state, zero_state)
            p_l = _norm_mod_matmul(x_lat, norm1_g[layer], mods, lat_row, (1, 0), w_pack, tm=tm_lat,
                                   gla_layout=True, rope=_rope_tables(t_len), name="gla_in_lat")
            y_gla_l, _, _ = _gla(p_l, t_len, wa, ba, gla_norm_g[j], s_f, s_b)
            y_conv_l = _conv(p_l, t_len, conv_dw_w[j], conv_dw_b[j], conv_ln_g[j], conv_ln_b[j])
            x_lat = _matmul_residual([y_gla_l, y_conv_l], w_out, x_lat, mods, lat_row, 2, tm=tm_lat,
                                     name="mix_out_lat")
            if not last:
                y_conv_c = _conv(p_c, l_ctx, conv_dw_w[j], conv_dw_b[j], conv_ln_g[j], conv_ln_b[j])
                x_ctx = _matmul_residual([y_gla_c, y_conv_c], w_out, x_ctx, mods, ctx_row, 2, tm=tm_ctx,
                                         name="mix_out_ctx")
        else:
            w_qkv = na_w_qkv[j].astype(BF16)
            kv_c = _norm_mod_matmul(x_ctx, norm1_g[layer], mods, ctx_row, (1, 0), w_qkv[:, NA_WIDTH:],
                                    tm=tm_ctx, name="na_kv_ctx")
            qkv = _norm_mod_matmul(x_lat, norm1_g[layer], mods, lat_row, (1, 0), w_qkv, tm=tm_lat,
                                   name="na_qkv_lat")
            o_l = _neighbourhood_attention(qkv, kv_c, na_rpb[j], t_len, l_ctx)
            x_lat = _matmul_residual([o_l], na_w_out[j].astype(BF16), x_lat, mods, lat_row, 2, tm=tm_lat,
                                     name="na_out_lat")
            if not last:
                raise NotImplementedError("context output of an attention layer is only needed mid-stack")
        streams = [(x_lat, lat_token_row)]
        if not last:
            streams.append((x_ctx, ctx_token_row))
        outs = _moe(streams, layer, mods, norm2_g[layer], router_w[layer], router_b[layer],
                    expert_w_gu, expert_b_gu, expert_w_down, expert_b_down, final_norm_g if last else None)
        x_lat = outs[0]
        if not last:
            x_ctx = outs[1]
    return x_lat.reshape(b, t_len, d)
```

```python
import functools

import numpy as np
import jax
import jax.numpy as jnp
from jax import lax
from jax.experimental import pallas as pl
from jax.experimental.pallas import tpu as pltpu

F32 = jnp.float32
BF16 = jnp.bfloat16
HIGHEST = lax.Precision.HIGHEST

NORM_EPS = 1e-6
ROPE_BASE = 10000.0
GRID_W = 64

GLA_HEADS = 4
GLA_DK = 64
GLA_DV = 128
GLA_LOWRANK = 16
GLA_TEMP = 16.0
GLA_CHUNK = 64
GLA_QK = GLA_HEADS * GLA_DK
GLA_V = GLA_HEADS * GLA_DV
CONV_CH = 512
CONV_WIDTH = 31
NA_HEADS = 16
NA_HEAD_DIM = 64
NA_WIDTH = NA_HEADS * NA_HEAD_DIM
NA_WIN_R = 8
NA_WIN_C = 16
N_EXPERTS = 32
TOP_K = 4
D_EXPERT = 1024
SWIGLU_LIMIT = 7.0
SWIGLU_ALPHA = 1.702

OFF_Q = 0
OFF_G = OFF_Q + GLA_QK
OFF_GLU = OFF_G + GLA_V
OFF_K = OFF_GLU + 2 * CONV_CH
OFF_V = OFF_K + GLA_QK
OFF_AF = OFF_V + GLA_V
OFF_AB = OFF_AF + GLA_LOWRANK

LANE = 128
VMEM_LIMIT = 56 * 1024 * 1024

HEAD_PAD = LANE
QK_PAD = GLA_HEADS * HEAD_PAD
PK_Q, PK_K, PK_V, PK_G, PK_A, PK_GT, PK_LR = 0, 512, 1024, 1536, 2048, 2560, 3072
PK_WIDTH = PK_LR + LANE

MOD_ROWS = 40
NEG_BIG = -1e30
ROW_TILE = 1024
COPY_TILE = 1024
EXPERT_TILE = 512
ISSUE_UNROLL = 8
EXPERT_SPLIT = 2


def _cparams(*sem):
    return pltpu.CompilerParams(dimension_semantics=sem, vmem_limit_bytes=VMEM_LIMIT)


def _norm_mod(x, g, scale, shift):
    y = x * lax.rsqrt(jnp.mean(x * x, axis=-1, keepdims=True) + NORM_EPS) * g
    return y * (1.0 + scale) + shift


def _sigmoid(x):
    return 1.0 / (1.0 + jnp.exp(-x))


def _mods_kernel(c_ref, w_ref, b_ref, o_ref):
    c = c_ref[...]
    s = c * _sigmoid(c)
    o_ref[...] = jnp.dot(s, w_ref[...], precision=HIGHEST, preferred_element_type=F32) + b_ref[...]


def _mods(c_all, ada_w, ada_b):
    depth, d, n6 = ada_w.shape
    nb = 512
    out = pl.pallas_call(
        _mods_kernel,
        out_shape=jax.ShapeDtypeStruct((depth, MOD_ROWS, n6), F32),
        grid=(depth, n6 // nb),
        in_specs=[
            pl.BlockSpec((MOD_ROWS, d), lambda l, j: (0, 0)),
            pl.BlockSpec((None, d, nb), lambda l, j: (l, 0, j)),
            pl.BlockSpec((None, 1, nb), lambda l, j: (l, 0, j)),
        ],
        out_specs=pl.BlockSpec((None, MOD_ROWS, nb), lambda l, j: (l, 0, j)),
        compiler_params=_cparams("arbitrary", "arbitrary"),
        name="adaln_mods",
    )(c_all, ada_w, ada_b.reshape(depth, 1, n6))
    return out.reshape(depth, MOD_ROWS * 6, 1, d)


def _mod_spec(row_fn, chunk, d):
    return pl.BlockSpec((None, 1, d), lambda i, *_: (row_fn(i) * 6 + chunk, 0, 0))


def _nmm_kernel(x_ref, g_ref, sc_ref, sh_ref, w_ref, *rest, gla_layout, rope):
    if rope:
        cos_ref, sin_ref, o_ref = rest
    else:
        (o_ref,) = rest
    h = _norm_mod(x_ref[...], g_ref[...], sc_ref[...], sh_ref[...]).astype(BF16)
    nout = o_ref.shape[1]
    chunk = 512
    for j0 in range(0, nout, chunk):
        j1 = min(j0 + chunk, nout)
        acc = jnp.dot(h, w_ref[:, j0:j1], preferred_element_type=F32)
        if gla_layout and j0 in (PK_Q, PK_K):
            if rope:
                lane = lax.broadcasted_iota(jnp.int32, acc.shape, 1)
                first = (lane % 32) < 16
                rot = jnp.where(first, pltpu.roll(acc, QK_PAD - 16, 1), pltpu.roll(acc, 16, 1))
                acc = acc * cos_ref[...] + rot * sin_ref[...]
            if j0 == PK_Q:
                acc = acc * (GLA_DK ** -0.5)
        o_ref[:, j0:j1] = acc.astype(o_ref.dtype)


def _norm_mod_matmul(x, g, mods, row_fn, chunks, w, *, tm, gla_layout=False, rope=None, name):
    n, d = x.shape
    nout = w.shape[1]
    in_specs = [
        pl.BlockSpec((tm, d), lambda i: (i, 0)),
        pl.BlockSpec((1, d), lambda i: (0, 0)),
        _mod_spec(row_fn, chunks[0], d),
        _mod_spec(row_fn, chunks[1], d),
        pl.BlockSpec((d, nout), lambda i: (0, 0)),
    ]
    args = [x, g.reshape(1, d), mods, mods, w]
    if rope is not None:
        cos_t, sin_t = rope
        t_tiles = cos_t.shape[0] // tm
        in_specs += [pl.BlockSpec((tm, QK_PAD), lambda i: (i % t_tiles, 0))] * 2
        args += [cos_t, sin_t]
    return pl.pallas_call(
        functools.partial(_nmm_kernel, gla_layout=gla_layout, rope=rope is not None),
        out_shape=jax.ShapeDtypeStruct((n, nout), BF16),
        grid=(n // tm,),
        in_specs=in_specs,
        out_specs=pl.BlockSpec((tm, nout), lambda i: (i, 0)),
        compiler_params=_cparams("arbitrary"),
        name=name,
    )(*args)


def _mmres_kernel(*refs, n_parts):
    a_refs = refs[:n_parts]
    w_ref, x_ref, gate_ref, o_ref = refs[n_parts:]
    acc = None
    k0 = 0
    for a_ref in a_refs:
        kk = a_ref.shape[1]
        t = jnp.dot(a_ref[...], w_ref[k0:k0 + kk, :], preferred_element_type=F32)
        acc = t if acc is None else acc + t
        k0 += kk
    o_ref[...] = x_ref[...] + gate_ref[...] * acc


def _matmul_residual(parts, w, x, mods, row_fn, gate_chunk, *, tm, name):
    n, d = x.shape
    in_specs = [pl.BlockSpec((tm, a.shape[1]), lambda i: (i, 0)) for a in parts]
    in_specs += [
        pl.BlockSpec(w.shape, lambda i: (0, 0)),
        pl.BlockSpec((tm, d), lambda i: (i, 0)),
        _mod_spec(row_fn, gate_chunk, d),
    ]
    return pl.pallas_call(
        functools.partial(_mmres_kernel, n_parts=len(parts)),
        out_shape=jax.ShapeDtypeStruct((n, d), F32),
        grid=(n // tm,),
        in_specs=in_specs,
        out_specs=pl.BlockSpec((tm, d), lambda i: (i, 0)),
        compiler_params=_cparams("arbitrary"),
        name=name,
    )(*parts, w, x, mods)


def _gla_kernel(q_ref, k_ref, v_ref, g_ref, a_ref, wa_ref, ba_ref, ng_ref, s0f_ref, s0b_ref,
                y_ref, sf_ref, sb_ref, of_ref, ob_ref):
    t_len = q_ref.shape[0]
    n_chunks = t_len // GLA_CHUNK
    c = GLA_CHUNK
    sf_ref[...] = s0f_ref[...]
    sb_ref[...] = s0b_ref[...]

    row = lax.broadcasted_iota(jnp.int32, (c, c), 0)
    col = lax.broadcasted_iota(jnp.int32, (c, c), 1)
    row3 = lax.broadcasted_iota(jnp.int32, (c, 3 * c), 0)
    col3 = lax.broadcasted_iota(jnp.int32, (c, 3 * c), 1) % c
    lane = lax.broadcasted_iota(jnp.int32, (1, QK_PAD), 1)
    real_lane = ((lane % HEAD_PAD) < GLA_DK).astype(F32)

    def dir_step(ci, forward):
        r0 = pl.multiple_of(ci * c, c)
        rows = pl.ds(r0, c)
        off = 0 if forward else QK_PAD
        z = jnp.dot(a_ref[rows, :], wa_ref[:, off:off + QK_PAD], preferred_element_type=F32)
        z = z + ba_ref[:, off:off + QK_PAD]
        log_sig = jnp.minimum(z, 0.0) - jnp.log1p(jnp.exp(-jnp.abs(z)))
        la = log_sig * (1.0 / GLA_TEMP) * real_lane
        keep = (row >= col) if forward else (row <= col)
        keep3 = (row3 >= col3) if forward else (row3 <= col3)
        la_hi = la.astype(BF16)
        rem = la - la_hi.astype(F32)
        la_mid = rem.astype(BF16)
        la_lo = (rem - la_mid.astype(F32)).astype(BF16)
        bc = jnp.dot(keep3.astype(BF16), jnp.concatenate([la_hi, la_mid, la_lo], axis=0),
                     preferred_element_type=F32)
        b_last = bc[c - 1:c, :] if forward else bc[0:1, :]
        q = q_ref[rows, :].astype(F32)
        k = k_ref[rows, :].astype(F32)
        v = v_ref[rows, :]
        qt = (q * jnp.exp(bc)).astype(BF16)
        kt = (k * jnp.exp(-bc)).astype(BF16)
        kd = (k * jnp.exp(b_last - bc)).astype(BF16)
        decay = jnp.exp(b_last)
        s_ref = sf_ref if forward else sb_ref
        o_ref = of_ref if forward else ob_ref
        for h in range(GLA_HEADS):
            sl = slice(h * HEAD_PAD, (h + 1) * HEAD_PAD)
            att = lax.dot_general(qt[:, sl], kt[:, sl], (((1,), (1,)), ((), ())),
                                  preferred_element_type=F32)
            att = jnp.where(keep, att, 0.0).astype(BF16)
            st = s_ref[0, h]
            o = jnp.dot(att, v[:, sl], preferred_element_type=F32)
            o = o + lax.dot_general(qt[:, sl], st.astype(BF16), (((1,), (1,)), ((), ())),
                                    preferred_element_type=F32)
            kv_t = lax.dot_general(v[:, sl], kd[:, sl], (((0,), (0,)), ((), ())),
                                   preferred_element_type=F32)
            s_ref[0, h] = st * decay[:, sl] + kv_t
            o_ref[rows, sl] = o

    def step(i, carry):
        dir_step(i, True)
        dir_step(n_chunks - 1 - i, False)
        return carry

    lax.fori_loop(0, n_chunks, step, 0)

    blk = min(256, t_len)

    def post(i, carry):
        rows = pl.ds(pl.multiple_of(i * blk, blk), blk)
        o = of_ref[rows, :] + ob_ref[rows, :]
        g = g_ref[rows, :].astype(F32)
        gate = g * _sigmoid(g)
        for h in range(GLA_HEADS):
            sl = slice(h * GLA_DV, (h + 1) * GLA_DV)
            oh = o[:, sl]
            oh = oh * lax.rsqrt(jnp.mean(oh * oh, axis=-1, keepdims=True) + NORM_EPS)
            y_ref[rows, sl] = (oh * ng_ref[:, sl] * gate[:, sl]).astype(y_ref.dtype)
        return carry

    lax.fori_loop(0, t_len // blk, post, 0)


def _gla(p, t_len, wa, ba, norm_g, s0f, s0b):
    n = p.shape[0]
    b = n // t_len

    def col(width, start):
        return pl.BlockSpec((t_len, width), lambda i: (i, start // width))

    st_spec = pl.BlockSpec((1, GLA_HEADS, GLA_DV, HEAD_PAD), lambda i: (i, 0, 0, 0))
    st_shape = jax.ShapeDtypeStruct((b, GLA_HEADS, GLA_DV, HEAD_PAD), F32)
    return pl.pallas_call(
        _gla_kernel,
        out_shape=(jax.ShapeDtypeStruct((n, GLA_V), BF16), st_shape, st_shape),
        grid=(b,),
        in_specs=[
            col(QK_PAD, PK_Q), col(QK_PAD, PK_K), col(GLA_V, PK_V), col(GLA_V, PK_G), col(LANE, PK_LR),
            pl.BlockSpec(wa.shape, lambda i: (0, 0)),
            pl.BlockSpec(ba.shape, lambda i: (0, 0)),
            pl.BlockSpec((1, GLA_V), lambda i: (0, 0)),
            st_spec, st_spec,
        ],
        out_specs=(pl.BlockSpec((t_len, GLA_V), lambda i: (i, 0)), st_spec, st_spec),
        scratch_shapes=[pltpu.VMEM((t_len, GLA_V), F32), pltpu.VMEM((t_len, GLA_V), F32)],
        compiler_params=_cparams("arbitrary"),
        name="gla_scan",
    )(p, p, p, p, p, wa, ba, norm_g.reshape(1, GLA_V), s0f, s0b)


CONV_HALO = 16
CONV_ROWS = 64


def _conv_kernel(a_ref, gt_ref, w_ref, cb_ref, lg_ref, lb_ref, y_ref, u_ref):
    t_len = a_ref.shape[0]
    zeros = jnp.zeros((CONV_HALO, CONV_CH), F32)
    u_ref[0:CONV_HALO, :] = zeros
    u_ref[CONV_HALO + t_len:2 * CONV_HALO + t_len, :] = zeros
    blk = min(256, t_len)

    def glu(i, carry):
        r0 = pl.multiple_of(i * blk, blk)
        a = a_ref[pl.ds(r0, blk), :].astype(F32)
        gt = gt_ref[pl.ds(r0, blk), :].astype(F32)
        u_ref[pl.ds(r0 + CONV_HALO, blk), :] = a * _sigmoid(gt)
        return carry

    lax.fori_loop(0, t_len // blk, glu, 0)
    shift = CONV_HALO - CONV_WIDTH // 2
    win_rows = CONV_ROWS + 2 * CONV_HALO

    def tile(i, carry):
        r0 = pl.multiple_of(i * CONV_ROWS, CONV_ROWS)
        parts = []
        for lb in range(CONV_CH // LANE):
            ls = slice(lb * LANE, (lb + 1) * LANE)
            win = u_ref[pl.ds(r0, win_rows), ls]
            acc = jnp.zeros((CONV_ROWS, LANE), F32)
            for b in range(8):
                wb = win if b == 0 else pltpu.roll(win, win_rows - b, 0)
                for a in range(win_rows // 8):
                    j = 8 * a + b - shift
                    if 0 <= j < CONV_WIDTH:
                        acc = acc + wb[8 * a:8 * a + CONV_ROWS, :] * w_ref[j:j + 1, ls]
            parts.append(acc)
        y = jnp.concatenate(parts, axis=1) + cb_ref[...]
        mu = jnp.mean(y, axis=-1, keepdims=True)
        yc = y - mu
        var = jnp.mean(yc * yc, axis=-1, keepdims=True)
        yn = yc * lax.rsqrt(var + NORM_EPS) * lg_ref[...] + lb_ref[...]
        y_ref[pl.ds(r0, CONV_ROWS), :] = (yn * _sigmoid(yn)).astype(y_ref.dtype)
        return carry

    lax.fori_loop(0, t_len // CONV_ROWS, tile, 0)


def _conv(p, t_len, conv_w, conv_b, ln_g, ln_b):
    n = p.shape[0]
    w_pad = jnp.pad(conv_w, ((0, 32 - CONV_WIDTH), (0, 0)))
    vec = pl.BlockSpec((1, CONV_CH), lambda i: (0, 0))
    return pl.pallas_call(
        _conv_kernel,
        out_shape=jax.ShapeDtypeStruct((n, CONV_CH), BF16),
        grid=(n // t_len,),
        in_specs=[
            pl.BlockSpec((t_len, CONV_CH), lambda i: (i, PK_A // CONV_CH)),
            pl.BlockSpec((t_len, CONV_CH), lambda i: (i, PK_GT // CONV_CH)),
            pl.BlockSpec((32, CONV_CH), lambda i: (0, 0)),
            vec, vec, vec,
        ],
        out_specs=pl.BlockSpec((t_len, CONV_CH), lambda i: (i, 0)),
        scratch_shapes=[pltpu.VMEM((t_len + 2 * CONV_HALO, CONV_CH), F32)],
        compiler_params=_cparams("arbitrary"),
        name="conformer_conv",
    )(p, p, w_pad, conv_b.reshape(1, -1), ln_g.reshape(1, -1), ln_b.reshape(1, -1))


NA_QROWS = 4
NA_BAND = 12


def _na_geometry(rows):
    assert rows >= NA_BAND and rows % NA_QROWS == 0
    win_r = min(NA_WIN_R, rows)
    starts, classes, sigs = [], [], []
    for rb in range(rows // NA_QROWS):
        bs = int(np.clip(rb * NA_QROWS - NA_WIN_R // 2, 0, rows - NA_BAND))
        sig = tuple((rb * NA_QROWS + i - bs,
                     int(np.clip(rb * NA_QROWS + i - win_r // 2, 0, rows - win_r)) - bs)
                    for i in range(NA_QROWS))
        if sig not in sigs:
            sigs.append(sig)
        starts.append(bs)
        classes.append(sigs.index(sig))
    return starts, classes, sigs, win_r


def _na_bias(rpb, sigs, win_r):
    w = GRID_W
    cols = np.arange(w)
    c_start = np.clip(cols - NA_WIN_C // 2, 0, w - NA_WIN_C)
    col_ok = (cols[None, :] >= c_start[:, None]) & (cols[None, :] < c_start[:, None] + NA_WIN_C)
    dc = np.clip(cols[None, :] - cols[:, None] + NA_WIN_C - 1, 0, 2 * NA_WIN_C - 2)
    rpb_col = jnp.take(rpb.astype(F32), jnp.asarray(dc.reshape(-1)), axis=2)
    rpb_col = rpb_col.reshape(NA_HEADS, 2 * NA_WIN_R - 1, w, w)
    rpb_col = jnp.where(jnp.asarray(col_ok)[None, None], rpb_col, NEG_BIG)
    kj = np.arange(NA_BAND)
    out = []
    for sig in sigs:
        q_rel = np.array([s[0] for s in sig])
        r_rel = np.array([s[1] for s in sig])
        row_ok = (kj[None, :] >= r_rel[:, None]) & (kj[None, :] < r_rel[:, None] + win_r)
        dr = np.clip(kj[None, :] - q_rel[:, None] + NA_WIN_R - 1, 0, 2 * NA_WIN_R - 2)
        bias = jnp.take(rpb_col, jnp.asarray(dr.reshape(-1)), axis=1)
        bias = bias.reshape(NA_HEADS, NA_QROWS, NA_BAND, w, w)
        bias = jnp.where(jnp.asarray(row_ok)[None, :, :, None, None], bias, NEG_BIG)
        out.append(bias.transpose(0, 1, 3, 2, 4).reshape(NA_HEADS, NA_QROWS * w, NA_BAND * w))
    return jnp.stack(out, axis=0)


def _na_kernel(q_ref, k_ref, v_ref, kc_ref, vc_ref, bias_ref, o_ref, *, starts, classes):
    w = GRID_W
    scale = NA_HEAD_DIM ** -0.5
    nq = NA_QROWS * w
    kc = kc_ref[...]
    vc = vc_ref[...]
    nt = (((1,), (1,)), ((), ()))

    def head_lanes(rows, hh):
        lane = lax.broadcasted_iota(jnp.int32, (rows, LANE), 1)
        return (lane >= hh * NA_HEAD_DIM) & (lane < (hh + 1) * NA_HEAD_DIM)

    vc_h = [jnp.where(head_lanes(vc.shape[0], hh), vc, jnp.ones_like(vc)) for hh in range(2)]
    for rb, (bs, cls) in enumerate(zip(starts, classes)):
        q2 = q_ref[rb * nq:(rb + 1) * nq, :]
        kb = k_ref[bs * w:(bs + NA_BAND) * w, :]
        vb = v_ref[bs * w:(bs + NA_BAND) * w, :]
        acc = None
        for hh in range(2):
            mine = head_lanes(nq, hh)
            qh = jnp.where(mine, q2, jnp.zeros_like(q2)) * scale
            s_loc = lax.dot_general(qh, kb, nt, preferred_element_type=F32) + bias_ref[cls, hh]
            s_ctx = lax.dot_general(qh, kc, nt, preferred_element_type=F32)
            m = jnp.maximum(jnp.max(s_loc, axis=-1, keepdims=True), jnp.max(s_ctx, axis=-1, keepdims=True))
            p_loc = jnp.exp((s_loc - m).astype(BF16))
            p_ctx = jnp.exp((s_ctx - m).astype(BF16))
            vb_h = jnp.where(head_lanes(vb.shape[0], hh), vb, jnp.ones_like(vb))
            o = jnp.dot(p_loc, vb_h, preferred_element_type=F32)
            o = o + jnp.dot(p_ctx, vc_h[hh], preferred_element_type=F32)
            o = o / pltpu.roll(o, NA_HEAD_DIM, 1)
            acc = o if acc is None else jnp.where(mine, o, acc)
        o_ref[rb * nq:(rb + 1) * nq, :] = acc.astype(o_ref.dtype)


def _neighbourhood_attention(qkv, kv_ctx, rpb, t_len, l_ctx):
    n = qkv.shape[0]
    b = n // t_len
    rows = t_len // GRID_W
    starts, classes, sigs, win_r = _na_geometry(rows)
    bias = _na_bias(rpb, sigs, win_r)
    n_pairs = NA_HEADS // 2
    nq, nk = NA_QROWS * GRID_W, NA_BAND * GRID_W
    return pl.pallas_call(
        functools.partial(_na_kernel, starts=starts, classes=classes),
        out_shape=jax.ShapeDtypeStruct((n, NA_WIDTH), BF16),
        grid=(n_pairs, b),
        in_specs=[
            pl.BlockSpec((t_len, LANE), lambda j, i: (i, j)),
            pl.BlockSpec((t_len, LANE), lambda j, i: (i, n_pairs + j)),
            pl.BlockSpec((t_len, LANE), lambda j, i: (i, 2 * n_pairs + j)),
            pl.BlockSpec((l_ctx, LANE), lambda j, i: (i, j)),
            pl.BlockSpec((l_ctx, LANE), lambda j, i: (i, n_pairs + j)),
            pl.BlockSpec((len(sigs), 2, nq, nk), lambda j, i: (0, j, 0, 0)),
        ],
        out_specs=pl.BlockSpec((t_len, LANE), lambda j, i: (i, j)),
        compiler_params=_cparams("arbitrary", "arbitrary"),
        name="neighbourhood_attention",
    )(qkv, qkv, qkv, kv_ctx, kv_ctx, bias)


INFO_ID, INFO_W, INFO_RANK = 0, TOP_K, 2 * TOP_K
INFO_ROWS = 16


def _router_kernel(x_ref, g_ref, sc_ref, sh_ref, wrt_ref, br_ref, tri_ref, cnt0_ref, *rest):
    h_ref, info_ref, cnt_ref = rest[-3:]
    tm = x_ref.shape[0]

    @pl.when(pl.program_id(0) == 0)
    def _():
        cnt_ref[...] = cnt0_ref[...]

    h = _norm_mod(x_ref[...], g_ref[...], sc_ref[...], sh_ref[...])
    h_ref[...] = h
    logits = lax.dot_general(wrt_ref[...], h, (((1,), (1,)), ((), ())), precision=HIGHEST,
                             preferred_element_type=F32) + br_ref[...]
    sub = lax.broadcasted_iota(jnp.int32, (N_EXPERTS, tm), 0)
    cur = logits
    vals, ids = [], []
    for _ in range(TOP_K):
        m = jnp.max(cur, axis=0, keepdims=True)
        idx = jnp.min(jnp.where(cur == m, sub, N_EXPERTS), axis=0, keepdims=True)
        vals.append(m)
        ids.append(idx)
        cur = jnp.where(sub == idx, NEG_BIG, cur)
    ex = [jnp.exp(v - vals[0]) for v in vals]
    den = ex[0] + ex[1] + ex[2] + ex[3]
    onehot = jnp.zeros((N_EXPERTS, tm), F32)
    for idx in ids:
        onehot = onehot + (sub == idx).astype(F32)
    before = jnp.dot(onehot.astype(BF16), tri_ref[...], preferred_element_type=F32)
    running = cnt_ref[:, 0:1]
    base = running + before
    row = lax.broadcasted_iota(jnp.int32, (INFO_ROWS, tm), 0)
    info = jnp.zeros((INFO_ROWS, tm), F32)
    for k in range(TOP_K):
        rank = jnp.sum(jnp.where(sub == ids[k], base, 0.0), axis=0, keepdims=True)
        info = info + jnp.where(row == INFO_ID + k, ids[k].astype(F32), 0.0)
        info = info + jnp.where(row == INFO_W + k, ex[k] / den, 0.0)
        info = info + jnp.where(row == INFO_RANK + k, rank, 0.0)
    info_ref[...] = info
    cnt_ref[...] = jnp.broadcast_to(running + jnp.sum(onehot, axis=1, keepdims=True), cnt_ref.shape)


def _router(x, g, mods, row_fn, w_rt, b_r, tri, cnt0, h_prev, h_rows, h_start, *, name):
    n, d = x.shape
    tm = ROW_TILE
    ht0 = h_start // tm
    in_specs = [
        pl.BlockSpec((tm, d), lambda i: (i, 0)),
        pl.BlockSpec((1, d), lambda i: (0, 0)),
        _mod_spec(lambda i: row_fn(i * tm), 4, d),
        _mod_spec(lambda i: row_fn(i * tm), 3, d),
        pl.BlockSpec((N_EXPERTS, d), lambda i: (0, 0)),
        pl.BlockSpec((N_EXPERTS, 1), lambda i: (0, 0)),
        pl.BlockSpec((tm, tm), lambda i: (0, 0)),
        pl.BlockSpec((N_EXPERTS, LANE), lambda i: (0, 0)),
    ]
    args = [x, g.reshape(1, d), mods, mods, w_rt, b_r, tri, cnt0]
    aliases = {}
    if h_prev is not None:
        in_specs.append(pl.BlockSpec(memory_space=pl.ANY))
        args.append(h_prev)
        aliases = {len(args) - 1: 0}
    return pl.pallas_call(
        _router_kernel,
        out_shape=(jax.ShapeDtypeStruct((h_rows, d), F32),
                   jax.ShapeDtypeStruct((INFO_ROWS, n), F32),
                   jax.ShapeDtypeStruct((N_EXPERTS, LANE), F32)),
        grid=(n // tm,),
        in_specs=in_specs,
        out_specs=(pl.BlockSpec((tm, d), lambda i: (i + ht0, 0)),
                   pl.BlockSpec((INFO_ROWS, tm), lambda i: (0, i)),
                   pl.BlockSpec((N_EXPERTS, LANE), lambda i: (0, 0))),
        input_output_aliases=aliases,
        compiler_params=_cparams("arbitrary"),
        name=name,
    )(*args)


def _row_copy(src_ref, src_row, dst_ref, dst_row, sem):
    return pltpu.make_async_copy(src_ref.at[pl.ds(src_row, 1)], dst_ref.at[pl.ds(dst_row, 1)], sem)


def _dispatch_kernel(pos_ref, h_ref, xs_ref, sem):
    tm = h_ref.shape[0]

    def issue(rb, carry):
        base = pl.multiple_of(rb * ISSUE_UNROLL, ISSUE_UNROLL)
        for u in range(ISSUE_UNROLL):
            for k in range(TOP_K):
                _row_copy(h_ref, base + u, xs_ref, pos_ref[0, 0, (base + u) * TOP_K + k], sem).start()
        return carry

    lax.fori_loop(0, tm // ISSUE_UNROLL, issue, 0)
    for _ in range(TOP_K):
        pltpu.make_async_copy(h_ref, xs_ref.at[pl.ds(0, tm)], sem).wait()


def _dispatch(pos, h, n_slots):
    n, d = h.shape
    tm = COPY_TILE
    return pl.pallas_call(
        _dispatch_kernel,
        out_shape=jax.ShapeDtypeStruct((n_slots, d), F32),
        grid=(n // tm,),
        in_specs=[
            pl.BlockSpec((1, 1, tm * TOP_K), lambda i: (i, 0, 0), memory_space=pltpu.SMEM),
            pl.BlockSpec((tm, d), lambda i: (i, 0)),
        ],
        out_specs=pl.BlockSpec(memory_space=pl.ANY),
        scratch_shapes=[pltpu.SemaphoreType.DMA],
        compiler_params=_cparams("arbitrary"),
        name="moe_dispatch",
    )(pos.reshape(n // tm, 1, tm * TOP_K), h)


PAD_ROWS = 8


def _padfill_kernel(lo_ref, mid_ref, hi_ref, xs_in, xs_ref, zero_ref, sem):
    del xs_in
    zero_ref[...] = jnp.zeros(zero_ref.shape, F32)
    for phase in range(2):
        for e in range(N_EXPERTS):
            def row(r, carry):
                cp = _row_copy(zero_ref, 0, xs_ref, r, sem)
                cp.start() if phase == 0 else cp.wait()
                return carry

            def block(b, carry):
                r = pl.multiple_of(mid_ref[e] + b * PAD_ROWS, PAD_ROWS)
                cp = pltpu.make_async_copy(zero_ref, xs_ref.at[pl.ds(r, PAD_ROWS)], sem)
                cp.start() if phase == 0 else cp.wait()
                return carry

            lax.fori_loop(lo_ref[e], mid_ref[e], row, 0)
            lax.fori_loop(0, (hi_ref[e] - mid_ref[e]) // PAD_ROWS, block, 0)


def _padfill(xs, lo, hi):
    mid = jnp.minimum(((lo + PAD_ROWS - 1) // PAD_ROWS) * PAD_ROWS, hi)
    return pl.pallas_call(
        _padfill_kernel,
        out_shape=jax.ShapeDtypeStruct(xs.shape, F32),
        grid_spec=pltpu.PrefetchScalarGridSpec(
            num_scalar_prefetch=3,
            grid=(1,),
            in_specs=[pl.BlockSpec(memory_space=pl.ANY)],
            out_specs=pl.BlockSpec(memory_space=pl.ANY),
            scratch_shapes=[pltpu.VMEM((PAD_ROWS,) + xs.shape[1:], F32), pltpu.SemaphoreType.DMA],
        ),
        input_output_aliases={3: 0},
        compiler_params=_cparams("arbitrary"),
        name="moe_padfill",
    )(lo, mid, hi, xs)


def _expert_kernel(te_ref, na_ref, xs_ref, wgu_ref, bgu_ref, wd_ref, bd_ref, y_ref, wgu_bf, wd_bf):
    i = pl.program_id(0)
    active = i < na_ref[0]
    first_of_expert = (i == 0) | (te_ref[i] != te_ref[jnp.maximum(i - 1, 0)])

    @pl.when(active & first_of_expert)
    def _():
        rows = 256
        for r in range(0, wgu_ref.shape[0], rows):
            wgu_bf[r:r + rows, :] = wgu_ref[r:r + rows, :].astype(BF16)
        for r in range(0, wd_ref.shape[0], rows):
            wd_bf[r:r + rows, :] = wd_ref[r:r + rows, :].astype(BF16)

    @pl.when(active)
    def _():
        x = xs_ref[...].astype(BF16)
        f = wd_ref.shape[0]
        half = f // EXPERT_SPLIT
        y = None
        for lo in range(0, f, half):
            gt = jnp.dot(x, wgu_bf[:, lo:lo + half], preferred_element_type=F32) + bgu_ref[:, lo:lo + half]
            up = jnp.dot(x, wgu_bf[:, f + lo:f + lo + half], preferred_element_type=F32)
            up = up + bgu_ref[:, f + lo:f + lo + half]
            gt = jnp.minimum(gt, SWIGLU_LIMIT)
            up = jnp.clip(up, -SWIGLU_LIMIT, SWIGLU_LIMIT)
            act = (up + 1.0) * gt * _sigmoid(SWIGLU_ALPHA * gt)
            t = jnp.dot(act.astype(BF16), wd_bf[lo:lo + half, :], preferred_element_type=F32)
            y = t if y is None else y + t
        y_ref[...] = y + bd_ref[...]

    @pl.when(jnp.logical_not(active))
    def _():
        y_ref[...] = jnp.zeros(y_ref.shape, F32)


def _experts(xs, tile_expert, n_active, layer, w_gu, b_gu, w_down, b_down):
    n_slots, d = xs.shape
    tm = EXPERT_TILE
    depth, n_e, _, two_f = w_gu.shape
    return pl.pallas_call(
        _expert_kernel,
        out_shape=jax.ShapeDtypeStruct(xs.shape, F32),
        grid_spec=pltpu.PrefetchScalarGridSpec(
            num_scalar_prefetch=2,
            grid=(n_slots // tm,),
            in_specs=[
                pl.BlockSpec((tm, d), lambda i, te, na: (jnp.minimum(i, na[0] - 1), 0)),
                pl.BlockSpec((None, None, d, two_f), lambda i, te, na: (layer, te[i], 0, 0)),
                pl.BlockSpec((None, None, 1, two_f), lambda i, te, na: (layer, te[i], 0, 0)),
                pl.BlockSpec((None, None, two_f // 2, d), lambda i, te, na: (layer, te[i], 0, 0)),
                pl.BlockSpec((None, None, 1, d), lambda i, te, na: (layer, te[i], 0, 0)),
            ],
            out_specs=pl.BlockSpec((tm, d), lambda i, te, na: (i, 0)),
            scratch_shapes=[pltpu.VMEM((d, two_f), BF16), pltpu.VMEM((two_f // 2, d), BF16)],
        ),
        compiler_params=_cparams("arbitrary"),
        name="moe_experts",
    )(tile_expert, n_active, xs, w_gu, b_gu.reshape(depth, n_e, 1, two_f), w_down,
      b_down.reshape(depth, n_e, 1, d))


def _combine_kernel(pos_ref, x_ref, wts_ref, gate_ref, *rest, final):
    if final:
        fg_ref, y_hbm, o_ref, ybuf, sem = rest
    else:
        y_hbm, o_ref, ybuf, sem = rest
    tm = x_ref.shape[0]

    def issue(rb, carry):
        base = pl.multiple_of(rb * ISSUE_UNROLL, ISSUE_UNROLL)
        for u in range(ISSUE_UNROLL):
            for k in range(TOP_K):
                _row_copy(y_hbm, pos_ref[0, 0, (base + u) * TOP_K + k], ybuf.at[k], base + u, sem).start()
        return carry

    lax.fori_loop(0, tm // ISSUE_UNROLL, issue, 0)
    for k in range(TOP_K):
        pltpu.make_async_copy(y_hbm.at[pl.ds(0, tm)], ybuf.at[k], sem).wait()
    wts = wts_ref[...]
    f = wts[:, 0:1] * ybuf[0]
    for k in range(1, TOP_K):
        f = f + wts[:, k:k + 1] * ybuf[k]
    out = x_ref[...] + gate_ref[...] * f
    if final:
        out = out * lax.rsqrt(jnp.mean(out * out, axis=-1, keepdims=True) + NORM_EPS) * fg_ref[...]
    o_ref[...] = out


def _combine(x, g_start, pos, wts, mods, row_fn, y, final_g, *, name):
    n, d = x.shape
    tm = COPY_TILE
    gt0 = g_start // tm
    in_specs = [
        pl.BlockSpec((1, 1, tm * TOP_K), lambda i: (i + gt0, 0, 0), memory_space=pltpu.SMEM),
        pl.BlockSpec((tm, d), lambda i: (i, 0)),
        pl.BlockSpec((tm, TOP_K), lambda i: (i + gt0, 0)),
        _mod_spec(lambda i: row_fn(i * tm), 5, d),
    ]
    args = [pos.reshape(pos.shape[0] // tm, 1, tm * TOP_K), x, wts, mods]
    if final_g is not None:
        in_specs.append(pl.BlockSpec((1, d), lambda i: (0, 0)))
        args.append(final_g.reshape(1, d))
    in_specs.append(pl.BlockSpec(memory_space=pl.ANY))
    args.append(y)
    return pl.pallas_call(
        functools.partial(_combine_kernel, final=final_g is not None),
        out_shape=jax.ShapeDtypeStruct((n, d), F32),
        grid=(n // tm,),
        in_specs=in_specs,
        out_specs=pl.BlockSpec((tm, d), lambda i: (i, 0)),
        scratch_shapes=[pltpu.VMEM((TOP_K, tm) + y.shape[1:], F32), pltpu.SemaphoreType.DMA],
        compiler_params=_cparams("arbitrary"),
        name=name,
    )(*args)


def _moe(streams, layer, mods, norm_g, w_r, b_r, w_gu, b_gu, w_down, b_down, final_g):
    tm = EXPERT_TILE
    w_rt = w_r.T
    b_col = b_r.reshape(N_EXPERTS, 1)
    tok = np.arange(ROW_TILE)
    tri = jnp.asarray(tok[:, None] < tok[None, :], BF16)
    n_total = sum(x.shape[0] for x, _ in streams)
    cnt = jnp.zeros((N_EXPERTS, LANE), F32)
    h, infos, g_start = None, [], 0
    for s, (x, row_fn) in enumerate(streams):
        h, info, cnt = _router(x, norm_g, mods, row_fn, w_rt, b_col, tri, cnt, h, n_total, g_start,
                               name=f"moe_router_{s}")
        infos.append(info)
        g_start += x.shape[0]
    info = jnp.concatenate(infos, axis=1) if len(infos) > 1 else infos[0]
    counts = cnt[:, 0].astype(jnp.int32)
    padded = ((counts + tm - 1) // tm) * tm
    ends = jnp.cumsum(padded)
    offsets = ends - padded
    n_slots = n_total * TOP_K + N_EXPERTS * tm
    n_active = (ends[-1] // tm).astype(jnp.int32).reshape(1)
    tile_start = jnp.minimum(jnp.arange(n_slots // tm, dtype=jnp.int32), n_active[0] - 1) * tm
    tile_expert = jnp.sum((ends[None, :] <= tile_start[:, None]).astype(jnp.int32), axis=1)
    tile_expert = jnp.minimum(tile_expert, N_EXPERTS - 1)
    eid = info[INFO_ID:INFO_ID + TOP_K].astype(jnp.int32)
    rank = info[INFO_RANK:INFO_RANK + TOP_K].astype(jnp.int32)
    base = jnp.zeros_like(eid)
    for e in range(N_EXPERTS):
        base = jnp.where(eid == e, offsets[e], base)
    pos = (base + rank).T
    wts = info[INFO_W:INFO_W + TOP_K].T

    slots = _dispatch(pos, h, n_slots)
    slots = _padfill(slots, (offsets + counts).astype(jnp.int32), ends.astype(jnp.int32))
    y = _experts(slots, tile_expert, n_active, layer, w_gu, b_gu, w_down, b_down)
    outs, g_start = [], 0
    for s, (x, row_fn) in enumerate(streams):
        outs.append(_combine(x, g_start, pos, wts, mods, row_fn, y, final_g, name=f"moe_combine_{s}"))
        g_start += x.shape[0]
    return outs


def _pad_heads(w):
    lead = w.shape[:-1]
    w4 = w.reshape(*lead, GLA_HEADS, GLA_DK)
    w4 = jnp.pad(w4, [(0, 0)] * len(lead) + [(0, 0), (0, HEAD_PAD - GLA_DK)])
    return w4.reshape(*lead, QK_PAD)


def _pack_gla_in(w_in):
    d = w_in.shape[0]
    lr = jnp.pad(w_in[:, OFF_AF:OFF_AB + GLA_LOWRANK], ((0, 0), (0, LANE - 2 * GLA_LOWRANK)))
    return jnp.concatenate([
        _pad_heads(w_in[:, OFF_Q:OFF_Q + GLA_QK]),
        _pad_heads(w_in[:, OFF_K:OFF_K + GLA_QK]),
        w_in[:, OFF_V:OFF_V + GLA_V],
        w_in[:, OFF_G:OFF_G + GLA_V],
        w_in[:, OFF_GLU:OFF_GLU + 2 * CONV_CH],
        lr,
    ], axis=1).astype(BF16)


def _pack_decay(wa_f, ba_f, wa_b, ba_b):
    wa = jnp.zeros((LANE, 2 * QK_PAD), F32)
    wa = wa.at[0:GLA_LOWRANK, 0:QK_PAD].set(_pad_heads(wa_f))
    wa = wa.at[GLA_LOWRANK:2 * GLA_LOWRANK, QK_PAD:].set(_pad_heads(wa_b))
    ba = jnp.concatenate([_pad_heads(ba_f), _pad_heads(ba_b)]).reshape(1, 2 * QK_PAD)
    return wa.astype(BF16), ba


def _rope_tables(t_len):
    t = jnp.arange(t_len)
    row_pos = (t // GRID_W).astype(F32)
    col_pos = (t % GRID_W).astype(F32)
    half = GLA_DK // 4
    inv_freq = ROPE_BASE ** (-jnp.arange(half, dtype=F32) / half)
    dim = np.arange(HEAD_PAD)
    real = dim < GLA_DK
    use_col = (dim % GLA_DK) >= GLA_DK // 2
    first = (dim % (GLA_DK // 2)) < half
    pos = jnp.where(jnp.asarray(use_col)[None, :], col_pos[:, None], row_pos[:, None])
    ang = pos * inv_freq[dim % half][None, :]
    cos = jnp.where(jnp.asarray(real)[None, :], jnp.cos(ang), 0.0)
    sin = jnp.where(jnp.asarray(real)[None, :], jnp.sin(ang), 0.0)
    sin = jnp.where(jnp.asarray(first)[None, :], -sin, sin)
    return jnp.tile(cos, (1, GLA_HEADS)), jnp.tile(sin, (1, GLA_HEADS))


def kernel(x, c, ctx, c_ctx, ada_w, ada_b, norm1_g, norm2_g, gla_conv_w_in, gla_wa_fwd, gla_ba_fwd,
           gla_wa_bwd, gla_ba_bwd, gla_norm_g, conv_dw_w, conv_dw_b, conv_ln_g, conv_ln_b,
           gla_conv_w_out, na_w_qkv, na_rpb, na_w_out, router_w, router_b, expert_w_gu, expert_b_gu,
           expert_w_down, expert_b_down, final_norm_g):
    b, t_len, d = x.shape
    l_ctx = ctx.shape[1]
    depth = ada_w.shape[0]
    assert b + 1 <= MOD_ROWS and t_len % COPY_TILE == 0 and COPY_TILE % ROW_TILE == 0
    assert (b * l_ctx) % COPY_TILE == 0

    c_all = jnp.concatenate([c, c_ctx[None, :], jnp.zeros((MOD_ROWS - b - 1, d), F32)], axis=0)
    mods_all = _mods(c_all, ada_w, ada_b)

    x_lat = x.reshape(b * t_len, d)
    x_ctx = ctx.reshape(b * l_ctx, d)
    tm_lat = min(1024, t_len)
    tm_ctx = min(512, l_ctx)

    def lat_row(i, tm=tm_lat):
        return (i * tm) // t_len

    def ctx_row(i):
        return b

    def lat_token_row(t):
        return t // t_len

    def ctx_token_row(t):
        return b

    for layer in range(depth):
        last = layer == depth - 1
        j = layer // 2
        mods = mods_all[layer]
        if layer % 2 == 0:
            w_pack = _pack_gla_in(gla_conv_w_in[j])
            wa, ba = _pack_decay(gla_wa_fwd[j], gla_ba_fwd[j], gla_wa_bwd[j], gla_ba_bwd[j])
            w_out = gla_conv_w_out[j].astype(BF16)
            zero_state = jnp.zeros((b, GLA_HEADS, GLA_DV, HEAD_PAD), F32)
            p_c = _norm_mod_matmul(x_ctx, norm1_g[layer], mods, ctx_row, (1, 0), w_pack, tm=tm_ctx,
                                   gla_layout=True, name="gla_in_ctx")
            y_gla_c, s_f, s_b = _gla(p_c, l_ctx, wa, ba, gla_norm_g[j], zero_state, zero_state)
            p_l = _norm_mod_matmul(x_lat, norm1_g[layer], mods, lat_row, (1, 0), w_pack, tm=tm_lat,
                                   gla_layout=True, rope=_rope_tables(t_len), name="gla_in_lat")
            y_gla_l, _, _ = _gla(p_l, t_len, wa, ba, gla_norm_g[j], s_f, s_b)
            y_conv_l = _conv(p_l, t_len, conv_dw_w[j], conv_dw_b[j], conv_ln_g[j], conv_ln_b[j])
            x_lat = _matmul_residual([y_gla_l, y_conv_l], w_out, x_lat, mods, lat_row, 2, tm=tm_lat,
                                     name="mix_out_lat")
            if not last:
                y_conv_c = _conv(p_c, l_ctx, conv_dw_w[j], conv_dw_b[j], conv_ln_g[j], conv_ln_b[j])
                x_ctx = _matmul_residual([y_gla_c, y_conv_c], w_out, x_ctx, mods, ctx_row, 2, tm=tm_ctx,
                                         name="mix_out_ctx")
        else:
            w_qkv = na_w_qkv[j].astype(BF16)
            kv_c = _norm_mod_matmul(x_ctx, norm1_g[layer], mods, ctx_row, (1, 0), w_qkv[:, NA_WIDTH:],
                                    tm=tm_ctx, name="na_kv_ctx")
            qkv = _norm_mod_matmul(x_lat, norm1_g[layer], mods, lat_row, (1, 0), w_qkv, tm=tm_lat,
                                   name="na_qkv_lat")
            o_l = _neighbourhood_attention(qkv, kv_c, na_rpb[j], t_len, l_ctx)
            x_lat = _matmul_residual([o_l], na_w_out[j].astype(BF16), x_lat, mods, lat_row, 2, tm=tm_lat,
                                     name="na_out_lat")
            if not last:
                raise NotImplementedError("context output of an attention layer is only needed mid-stack")
        streams = [(x_lat, lat_token_row)]
        if not last:
            streams.append((x_ctx, ctx_token_row))
        outs = _moe(streams, layer, mods, norm2_g[layer], router_w[layer], router_b[layer],
                    expert_w_gu, expert_b_gu, expert_w_down, expert_b_down, final_norm_g if last else None)
        x_lat = outs[0]
        if not last:
            x_ctx = outs[1]
    return x_lat.reshape(b, t_len, d)
```

```python
import functools

import numpy as np
import jax
import jax.numpy as jnp
from jax import lax
from jax.experimental import pallas as pl
from jax.experimental.pallas import tpu as pltpu

F32 = jnp.float32
BF16 = jnp.bfloat16
HIGHEST = lax.Precision.HIGHEST

NORM_EPS = 1e-6
ROPE_BASE = 10000.0
GRID_W = 64

GLA_HEADS = 4
GLA_DK = 64
GLA_DV = 128
GLA_LOWRANK = 16
GLA_TEMP = 16.0
GLA_CHUNK = 64
GLA_QK = GLA_HEADS * GLA_DK
GLA_V = GLA_HEADS * GLA_DV
CONV_CH = 512
CONV_WIDTH = 31
NA_HEADS = 16
NA_HEAD_DIM = 64
NA_WIDTH = NA_HEADS * NA_HEAD_DIM
NA_WIN_R = 8
NA_WIN_C = 16
N_EXPERTS = 32
TOP_K = 4
D_EXPERT = 1024
SWIGLU_LIMIT = 7.0
SWIGLU_ALPHA = 1.702

OFF_Q = 0
OFF_G = OFF_Q + GLA_QK
OFF_GLU = OFF_G + GLA_V
OFF_K = OFF_GLU + 2 * CONV_CH
OFF_V = OFF_K + GLA_QK
OFF_AF = OFF_V + GLA_V
OFF_AB = OFF_AF + GLA_LOWRANK

LANE = 128
VMEM_LIMIT = 56 * 1024 * 1024

HEAD_PAD = LANE
QK_PAD = GLA_HEADS * HEAD_PAD
PK_Q, PK_K, PK_V, PK_G, PK_A, PK_GT, PK_LR = 0, 512, 1024, 1536, 2048, 2560, 3072
PK_WIDTH = PK_LR + LANE

MOD_ROWS = 40
NEG_BIG = -1e30
ROW_TILE = 1024
COPY_TILE = 1024
EXPERT_TILE = 512
ISSUE_UNROLL = 8
EXPERT_SPLIT = 2


def _cparams(*sem):
    return pltpu.CompilerParams(dimension_semantics=sem, vmem_limit_bytes=VMEM_LIMIT)


def _norm_mod(x, g, scale, shift):
    y = x * lax.rsqrt(jnp.mean(x * x, axis=-1, keepdims=True) + NORM_EPS) * g
    return y * (1.0 + scale) + shift


def _sigmoid(x):
    return 1.0 / (1.0 + jnp.exp(-x))


def _mods_kernel(c_ref, w_ref, b_ref, o_ref):
    c = c_ref[...]
    s = c * _sigmoid(c)
    o_ref[...] = jnp.dot(s, w_ref[...], precision=HIGHEST, preferred_element_type=F32) + b_ref[...]


def _mods(c_all, ada_w, ada_b):
    depth, d, n6 = ada_w.shape
    nb = 512
    out = pl.pallas_call(
        _mods_kernel,
        out_shape=jax.ShapeDtypeStruct((depth, MOD_ROWS, n6), F32),
        grid=(depth, n6 // nb),
        in_specs=[
            pl.BlockSpec((MOD_ROWS, d), lambda l, j: (0, 0)),
            pl.BlockSpec((None, d, nb), lambda l, j: (l, 0, j)),
            pl.BlockSpec((None, 1, nb), lambda l, j: (l, 0, j)),
        ],
        out_specs=pl.BlockSpec((None, MOD_ROWS, nb), lambda l, j: (l, 0, j)),
        compiler_params=_cparams("arbitrary", "arbitrary"),
        name="adaln_mods",
    )(c_all, ada_w, ada_b.reshape(depth, 1, n6))
    return out.reshape(depth, MOD_ROWS * 6, 1, d)


def _mod_spec(row_fn, chunk, d):
    return pl.BlockSpec((None, 1, d), lambda i, *_: (row_fn(i) * 6 + chunk, 0, 0))


def _nmm_kernel(x_ref, g_ref, sc_ref, sh_ref, w_ref, *rest, gla_layout, rope):
    if rope:
        cos_ref, sin_ref, o_ref = rest
    else:
        (o_ref,) = rest
    h = _norm_mod(x_ref[...], g_ref[...], sc_ref[...], sh_ref[...]).astype(BF16)
    nout = o_ref.shape[1]
    chunk = 512
    for j0 in range(0, nout, chunk):
        j1 = min(j0 + chunk, nout)
        acc = jnp.dot(h, w_ref[:, j0:j1], preferred_element_type=F32)
        if gla_layout and j0 in (PK_Q, PK_K):
            if rope:
                lane = lax.broadcasted_iota(jnp.int32, acc.shape, 1)
                first = (lane % 32) < 16
                rot = jnp.where(first, pltpu.roll(acc, QK_PAD - 16, 1), pltpu.roll(acc, 16, 1))
                acc = acc * cos_ref[...] + rot * sin_ref[...]
            if j0 == PK_Q:
                acc = acc * (GLA_DK ** -0.5)
        o_ref[:, j0:j1] = acc.astype(o_ref.dtype)


def _norm_mod_matmul(x, g, mods, row_fn, chunks, w, *, tm, gla_layout=False, rope=None, name):
    n, d = x.shape
    nout = w.shape[1]
    in_specs = [
        pl.BlockSpec((tm, d), lambda i: (i, 0)),
        pl.BlockSpec((1, d), lambda i: (0, 0)),
        _mod_spec(row_fn, chunks[0], d),
        _mod_spec(row_fn, chunks[1], d),
        pl.BlockSpec((d, nout), lambda i: (0, 0)),
    ]
    args = [x, g.reshape(1, d), mods, mods, w]
    if rope is not None:
        cos_t, sin_t = rope
        t_tiles = cos_t.shape[0] // tm
        in_specs += [pl.BlockSpec((tm, QK_PAD), lambda i: (i % t_tiles, 0))] * 2
        args += [cos_t, sin_t]
    return pl.pallas_call(
        functools.partial(_nmm_kernel, gla_layout=gla_layout, rope=rope is not None),
        out_shape=jax.ShapeDtypeStruct((n, nout), BF16),
        grid=(n // tm,),
        in_specs=in_specs,
        out_specs=pl.BlockSpec((tm, nout), lambda i: (i, 0)),
        compiler_params=_cparams("arbitrary"),
        name=name,
    )(*args)


def _mmres_kernel(*refs, n_parts):
    a_refs = refs[:n_parts]
    w_ref, x_ref, gate_ref, o_ref = refs[n_parts:]
    acc = None
    k0 = 0
    for a_ref in a_refs:
        kk = a_ref.shape[1]
        t = jnp.dot(a_ref[...], w_ref[k0:k0 + kk, :], preferred_element_type=F32)
        acc = t if acc is None else acc + t
        k0 += kk
    o_ref[...] = x_ref[...] + gate_ref[...] * acc


def _matmul_residual(parts, w, x, mods, row_fn, gate_chunk, *, tm, name):
    n, d = x.shape
    in_specs = [pl.BlockSpec((tm, a.shape[1]), lambda i: (i, 0)) for a in parts]
    in_specs += [
        pl.BlockSpec(w.shape, lambda i: (0, 0)),
        pl.BlockSpec((tm, d), lambda i: (i, 0)),
        _mod_spec(row_fn, gate_chunk, d),
    ]
    return pl.pallas_call(
        functools.partial(_mmres_kernel, n_parts=len(parts)),
        out_shape=jax.ShapeDtypeStruct((n, d), F32),
        grid=(n // tm,),
        in_specs=in_specs,
        out_specs=pl.BlockSpec((tm, d), lambda i: (i, 0)),
        compiler_params=_cparams("arbitrary"),
        name=name,
    )(*parts, w, x, mods)


def _gla_kernel(q_ref, k_ref, v_ref, g_ref, a_ref, wa_ref, ba_ref, ng_ref, s0f_ref, s0b_ref,
                y_ref, sf_ref, sb_ref, of_ref, ob_ref):
    t_len = q_ref.shape[0]
    n_chunks = t_len // GLA_CHUNK
    c = GLA_CHUNK
    sf_ref[...] = s0f_ref[...]
    sb_ref[...] = s0b_ref[...]

    row = lax.broadcasted_iota(jnp.int32, (c, c), 0)
    col = lax.broadcasted_iota(jnp.int32, (c, c), 1)
    row3 = lax.broadcasted_iota(jnp.int32, (c, 3 * c), 0)
    col3 = lax.broadcasted_iota(jnp.int32, (c, 3 * c), 1) % c
    lane = lax.broadcasted_iota(jnp.int32, (1, QK_PAD), 1)
    real_lane = ((lane % HEAD_PAD) < GLA_DK).astype(F32)

    def dir_step(ci, forward):
        r0 = pl.multiple_of(ci * c, c)
        rows = pl.ds(r0, c)
        off = 0 if forward else QK_PAD
        z = jnp.dot(a_ref[rows, :], wa_ref[:, off:off + QK_PAD], preferred_element_type=F32)
        z = z + ba_ref[:, off:off + QK_PAD]
        log_sig = jnp.minimum(z, 0.0) - jnp.log1p(jnp.exp(-jnp.abs(z)))
        la = log_sig * (1.0 / GLA_TEMP) * real_lane
        keep = (row >= col) if forward else (row <= col)
        keep3 = (row3 >= col3) if forward else (row3 <= col3)
        la_hi = la.astype(BF16)
        rem = la - la_hi.astype(F32)
        la_mid = rem.astype(BF16)
        la_lo = (rem - la_mid.astype(F32)).astype(BF16)
        bc = jnp.dot(keep3.astype(BF16), jnp.concatenate([la_hi, la_mid, la_lo], axis=0),
                     preferred_element_type=F32)
        b_last = bc[c - 1:c, :] if forward else bc[0:1, :]
        q = q_ref[rows, :].astype(F32)
        k = k_ref[rows, :].astype(F32)
        v = v_ref[rows, :]
        qt = (q * jnp.exp(bc)).astype(BF16)
        kt = (k * jnp.exp(-bc)).astype(BF16)
        kd = (k * jnp.exp(b_last - bc)).astype(BF16)
        decay = jnp.exp(b_last)
        s_ref = sf_ref if forward else sb_ref
        o_ref = of_ref if forward else ob_ref
        for h in range(GLA_HEADS):
            sl = slice(h * HEAD_PAD, (h + 1) * HEAD_PAD)
            att = lax.dot_general(qt[:, sl], kt[:, sl], (((1,), (1,)), ((), ())),
                                  preferred_element_type=F32)
            att = jnp.where(keep, att, 0.0).astype(BF16)
            st = s_ref[0, h]
            o = jnp.dot(att, v[:, sl], preferred_element_type=F32)
            o = o + lax.dot_general(qt[:, sl], st.astype(BF16), (((1,), (1,)), ((), ())),
                                    preferred_element_type=F32)
            kv_t = lax.dot_general(v[:, sl], kd[:, sl], (((0,), (0,)), ((), ())),
                                   preferred_element_type=F32)
            s_ref[0, h] = st * decay[:, sl] + kv_t
            o_ref[rows, sl] = o

    def step(i, carry):
        dir_step(i, True)
        dir_step(n_chunks - 1 - i, False)
        return carry

    lax.fori_loop(0, n_chunks, step, 0)

    blk = min(256, t_len)

    def post(i, carry):
        rows = pl.ds(pl.multiple_of(i * blk, blk), blk)
        o = of_ref[rows, :] + ob_ref[rows, :]
        g = g_ref[rows, :].astype(F32)
        gate = g * _sigmoid(g)
        for h in range(GLA_HEADS):
            sl = slice(h * GLA_DV, (h + 1) * GLA_DV)
            oh = o[:, sl]
            oh = oh * lax.rsqrt(jnp.mean(oh * oh, axis=-1, keepdims=True) + NORM_EPS)
            y_ref[rows, sl] = (oh * ng_ref[:, sl] * gate[:, sl]).astype(y_ref.dtype)
        return carry

    lax.fori_loop(0, t_len // blk, post, 0)


def _gla(p, t_len, wa, ba, norm_g, s0f, s0b):
    n = p.shape[0]
    b = n // t_len

    def col(width, start):
        return pl.BlockSpec((t_len, width), lambda i: (i, start // width))

    st_spec = pl.BlockSpec((1, GLA_HEADS, GLA_DV, HEAD_PAD), lambda i: (i, 0, 0, 0))
    st_shape = jax.ShapeDtypeStruct((b, GLA_HEADS, GLA_DV, HEAD_PAD), F32)
    return pl.pallas_call(
        _gla_kernel,
        out_shape=(jax.ShapeDtypeStruct((n, GLA_V), BF16), st_shape, st_shape),
        grid=(b,),
        in_specs=[
            col(QK_PAD, PK_Q), col(QK_PAD, PK_K), col(GLA_V, PK_V), col(GLA_V, PK_G), col(LANE, PK_LR),
            pl.BlockSpec(wa.shape, lambda i: (0, 0)),
            pl.BlockSpec(ba.shape, lambda i: (0, 0)),
            pl.BlockSpec((1, GLA_V), lambda i: (0, 0)),
            st_spec, st_spec,
        ],
        out_specs=(pl.BlockSpec((t_len, GLA_V), lambda i: (i, 0)), st_spec, st_spec),
        scratch_shapes=[pltpu.VMEM((t_len, GLA_V), F32), pltpu.VMEM((t_len, GLA_V), F32)],
        compiler_params=_cparams("arbitrary"),
        name="gla_scan",
    )(p, p, p, p, p, wa, ba, norm_g.reshape(1, GLA_V), s0f, s0b)


CONV_HALO = 16
CONV_ROWS = 64


def _conv_kernel(a_ref, gt_ref, w_ref, cb_ref, lg_ref, lb_ref, y_ref, u_ref):
    t_len = a_ref.shape[0]
    zeros = jnp.zeros((CONV_HALO, CONV_CH), F32)
    u_ref[0:CONV_HALO, :] = zeros
    u_ref[CONV_HALO + t_len:2 * CONV_HALO + t_len, :] = zeros
    blk = min(256, t_len)

    def glu(i, carry):
        r0 = pl.multiple_of(i * blk, blk)
        a = a_ref[pl.ds(r0, blk), :].astype(F32)
        gt = gt_ref[pl.ds(r0, blk), :].astype(F32)
        u_ref[pl.ds(r0 + CONV_HALO, blk), :] = a * _sigmoid(gt)
        return carry

    lax.fori_loop(0, t_len // blk, glu, 0)
    shift = CONV_HALO - CONV_WIDTH // 2
    win_rows = CONV_ROWS + 2 * CONV_HALO

    def tile(i, carry):
        r0 = pl.multiple_of(i * CONV_ROWS, CONV_ROWS)
        parts = []
        for lb in range(CONV_CH // LANE):
            ls = slice(lb * LANE, (lb + 1) * LANE)
            win = u_ref[pl.ds(r0, win_rows), ls]
            acc = jnp.zeros((CONV_ROWS, LANE), F32)
            for b in range(8):
                wb = win if b == 0 else pltpu.roll(win, win_rows - b, 0)
                for a in range(win_rows // 8):
                    j = 8 * a + b - shift
                    if 0 <= j < CONV_WIDTH:
                        acc = acc + wb[8 * a:8 * a + CONV_ROWS, :] * w_ref[j:j + 1, ls]
            parts.append(acc)
        y = jnp.concatenate(parts, axis=1) + cb_ref[...]
        mu = jnp.mean(y, axis=-1, keepdims=True)
        yc = y - mu
        var = jnp.mean(yc * yc, axis=-1, keepdims=True)
        yn = yc * lax.rsqrt(var + NORM_EPS) * lg_ref[...] + lb_ref[...]
        y_ref[pl.ds(r0, CONV_ROWS), :] = (yn * _sigmoid(yn)).astype(y_ref.dtype)
        return carry

    lax.fori_loop(0, t_len // CONV_ROWS, tile, 0)


def _conv(p, t_len, conv_w, conv_b, ln_g, ln_b):
    n = p.shape[0]
    w_pad = jnp.pad(conv_w, ((0, 32 - CONV_WIDTH), (0, 0)))
    vec = pl.BlockSpec((1, CONV_CH), lambda i: (0, 0))
    return pl.pallas_call(
        _conv_kernel,
        out_shape=jax.ShapeDtypeStruct((n, CONV_CH), BF16),
        grid=(n // t_len,),
        in_specs=[
            pl.BlockSpec((t_len, CONV_CH), lambda i: (i, PK_A // CONV_CH)),
            pl.BlockSpec((t_len, CONV_CH), lambda i: (i, PK_GT // CONV_CH)),
            pl.BlockSpec((32, CONV_CH), lambda i: (0, 0)),
            vec, vec, vec,
        ],
        out_specs=pl.BlockSpec((t_len, CONV_CH), lambda i: (i, 0)),
        scratch_shapes=[pltpu.VMEM((t_len + 2 * CONV_HALO, CONV_CH), F32)],
        compiler_params=_cparams("arbitrary"),
        name="conformer_conv",
    )(p, p, w_pad, conv_b.reshape(1, -1), ln_g.reshape(1, -1), ln_b.reshape(1, -1))


NA_QROWS = 4
NA_BAND = 12


def _na_geometry(rows):
    assert rows >= NA_BAND and rows % NA_QROWS == 0
    win_r = min(NA_WIN_R, rows)
    starts, classes, sigs = [], [], []
    for rb in range(rows // NA_QROWS):
        bs = int(np.clip(rb * NA_QROWS - NA_WIN_R // 2, 0, rows - NA_BAND))
        sig = tuple((rb * NA_QROWS + i - bs,
                     int(np.clip(rb * NA_QROWS + i - win_r // 2, 0, rows - win_r)) - bs)
                    for i in range(NA_QROWS))
        if sig not in sigs:
            sigs.append(sig)
        starts.append(bs)
        classes.append(sigs.index(sig))
    return starts, classes, sigs, win_r


def _na_bias(rpb, sigs, win_r):
    w = GRID_W
    cols = np.arange(w)
    c_start = np.clip(cols - NA_WIN_C // 2, 0, w - NA_WIN_C)
    col_ok = (cols[None, :] >= c_start[:, None]) & (cols[None, :] < c_start[:, None] + NA_WIN_C)
    dc = np.clip(cols[None, :] - cols[:, None] + NA_WIN_C - 1, 0, 2 * NA_WIN_C - 2)
    rpb_col = jnp.take(rpb.astype(F32), jnp.asarray(dc.reshape(-1)), axis=2)
    rpb_col = rpb_col.reshape(NA_HEADS, 2 * NA_WIN_R - 1, w, w)
    rpb_col = jnp.where(jnp.asarray(col_ok)[None, None], rpb_col, NEG_BIG)
    kj = np.arange(NA_BAND)
    out = []
    for sig in sigs:
        q_rel = np.array([s[0] for s in sig])
        r_rel = np.array([s[1] for s in sig])
        row_ok = (kj[None, :] >= r_rel[:, None]) & (kj[None, :] < r_rel[:, None] + win_r)
        dr = np.clip(kj[None, :] - q_rel[:, None] + NA_WIN_R - 1, 0, 2 * NA_WIN_R - 2)
        bias = jnp.take(rpb_col, jnp.asarray(dr.reshape(-1)), axis=1)
        bias = bias.reshape(NA_HEADS, NA_QROWS, NA_BAND, w, w)
        bias = jnp.where(jnp.asarray(row_ok)[None, :, :, None, None], bias, NEG_BIG)
        out.append(bias.transpose(0, 1, 3, 2, 4).reshape(NA_HEADS, NA_QROWS * w, NA_BAND * w))
    return jnp.stack(out, axis=0)


def _na_kernel(q_ref, k_ref, v_ref, kc_ref, vc_ref, bias_ref, o_ref, *, starts, classes):
    w = GRID_W
    scale = NA_HEAD_DIM ** -0.5
    nq = NA_QROWS * w
    kc = kc_ref[...]
    vc = vc_ref[...]
    nt = (((1,), (1,)), ((), ()))

    def head_lanes(rows, hh):
        lane = lax.broadcasted_iota(jnp.int32, (rows, LANE), 1)
        return (lane >= hh * NA_HEAD_DIM) & (lane < (hh + 1) * NA_HEAD_DIM)

    vc_h = [jnp.where(head_lanes(vc.shape[0], hh), vc, jnp.ones_like(vc)) for hh in range(2)]
    for rb, (bs, cls) in enumerate(zip(starts, classes)):
        q2 = q_ref[rb * nq:(rb + 1) * nq, :]
        kb = k_ref[bs * w:(bs + NA_BAND) * w, :]
        vb = v_ref[bs * w:(bs + NA_BAND) * w, :]
        acc = None
        for hh in range(2):
            mine = head_lanes(nq, hh)
            qh = jnp.where(mine, q2, jnp.zeros_like(q2)) * scale
            s_loc = lax.dot_general(qh, kb, nt, preferred_element_type=F32) + bias_ref[cls, hh]
            s_ctx = lax.dot_general(qh, kc, nt, preferred_element_type=F32)
            m = jnp.maximum(jnp.max(s_loc, axis=-1, keepdims=True), jnp.max(s_ctx, axis=-1, keepdims=True))
            p_loc = jnp.exp((s_loc - m).astype(BF16))
            p_ctx = jnp.exp((s_ctx - m).astype(BF16))
            vb_h = jnp.where(head_lanes(vb.shape[0], hh), vb, jnp.ones_like(vb))
            o = jnp.dot(p_loc, vb_h, preferred_element_type=F32)
            o = o + jnp.dot(p_ctx, vc_h[hh], preferred_element_type=F32)
            o = o / pltpu.roll(o, NA_HEAD_DIM, 1)
            acc = o if acc is None else jnp.where(mine, o, acc)
        o_ref[rb * nq:(rb + 1) * nq, :] = acc.astype(o_ref.dtype)


def _neighbourhood_attention(qkv, kv_ctx, rpb, t_len, l_ctx):
    n = qkv.shape[0]
    b = n // t_len
    rows = t_len // GRID_W
    starts, classes, sigs, win_r = _na_geometry(rows)
    bias = _na_bias(rpb, sigs, win_r)
    n_pairs = NA_HEADS // 2
    nq, nk = NA_QROWS * GRID_W, NA_BAND * GRID_W
    return pl.pallas_call(
        functools.partial(_na_kernel, starts=starts, classes=classes),
        out_shape=jax.ShapeDtypeStruct((n, NA_WIDTH), BF16),
        grid=(n_pairs, b),
        in_specs=[
            pl.BlockSpec((t_len, LANE), lambda j, i: (i, j)),
            pl.BlockSpec((t_len, LANE), lambda j, i: (i, n_pairs + j)),
            pl.BlockSpec((t_len, LANE), lambda j, i: (i, 2 * n_pairs + j)),
            pl.BlockSpec((l_ctx, LANE), lambda j, i: (i, j)),
            pl.BlockSpec((l_ctx, LANE), lambda j, i: (i, n_pairs + j)),
            pl.BlockSpec((len(sigs), 2, nq, nk), lambda j, i: (0, j, 0, 0)),
        ],
        out_specs=pl.BlockSpec((t_len, LANE), lambda j, i: (i, j)),
        compiler_params=_cparams("arbitrary", "arbitrary"),
        name="neighbourhood_attention",
    )(qkv, qkv, qkv, kv_ctx, kv_ctx, bias)


INFO_ID, INFO_W, INFO_RANK = 0, TOP_K, 2 * TOP_K
INFO_ROWS = 16


def _router_kernel(x_ref, g_ref, sc_ref, sh_ref, wrt_ref, br_ref, tri_ref, cnt0_ref, *rest):
    h_ref, info_ref, cnt_ref = rest[-3:]
    tm = x_ref.shape[0]

    @pl.when(pl.program_id(0) == 0)
    def _():
        cnt_ref[...] = cnt0_ref[...]

    h = _norm_mod(x_ref[...], g_ref[...], sc_ref[...], sh_ref[...])
    h_ref[...] = h
    logits = lax.dot_general(wrt_ref[...], h, (((1,), (1,)), ((), ())), precision=HIGHEST,
                             preferred_element_type=F32) + br_ref[...]
    sub = lax.broadcasted_iota(jnp.int32, (N_EXPERTS, tm), 0)
    cur = logits
    vals, ids = [], []
    for _ in range(TOP_K):
        m = jnp.max(cur, axis=0, keepdims=True)
        idx = jnp.min(jnp.where(cur == m, sub, N_EXPERTS), axis=0, keepdims=True)
        vals.append(m)
        ids.append(idx)
        cur = jnp.where(sub == idx, NEG_BIG, cur)
    ex = [jnp.exp(v - vals[0]) for v in vals]
    den = ex[0] + ex[1] + ex[2] + ex[3]
    onehot = jnp.zeros((N_EXPERTS, tm), F32)
    for idx in ids:
        onehot = onehot + (sub == idx).astype(F32)
    before = jnp.dot(onehot.astype(BF16), tri_ref[...], preferred_element_type=F32)
    running = cnt_ref[:, 0:1]
    base = running + before
    row = lax.broadcasted_iota(jnp.int32, (INFO_ROWS, tm), 0)
    info = jnp.zeros((INFO_ROWS, tm), F32)
    for k in range(TOP_K):
        rank = jnp.sum(jnp.where(sub == ids[k], base, 0.0), axis=0, keepdims=True)
        info = info + jnp.where(row == INFO_ID + k, ids[k].astype(F32), 0.0)
        info = info + jnp.where(row == INFO_W + k, ex[k] / den, 0.0)
        info = info + jnp.where(row == INFO_RANK + k, rank, 0.0)
    info_ref[...] = info
    cnt_ref[...] = jnp.broadcast_to(running + jnp.sum(onehot, axis=1, keepdims=True), cnt_ref.shape)


def _router(x, g, mods, row_fn, w_rt, b_r, tri, cnt0, h_prev, h_rows, h_start, *, name):
    n, d = x.shape
    tm = ROW_TILE
    ht0 = h_start // tm
    in_specs = [
        pl.BlockSpec((tm, d), lambda i: (i, 0)),
        pl.BlockSpec((1, d), lambda i: (0, 0)),
        _mod_spec(lambda i: row_fn(i * tm), 4, d),
        _mod_spec(lambda i: row_fn(i * tm), 3, d),
        pl.BlockSpec((N_EXPERTS, d), lambda i: (0, 0)),
        pl.BlockSpec((N_EXPERTS, 1), lambda i: (0, 0)),
        pl.BlockSpec((tm, tm), lambda i: (0, 0)),
        pl.BlockSpec((N_EXPERTS, LANE), lambda i: (0, 0)),
    ]
    args = [x, g.reshape(1, d), mods, mods, w_rt, b_r, tri, cnt0]
    aliases = {}
    if h_prev is not None:
        in_specs.append(pl.BlockSpec(memory_space=pl.ANY))
        args.append(h_prev)
        aliases = {len(args) - 1: 0}
    return pl.pallas_call(
        _router_kernel,
        out_shape=(jax.ShapeDtypeStruct((h_rows, d), F32),
                   jax.ShapeDtypeStruct((INFO_ROWS, n), F32),
                   jax.ShapeDtypeStruct((N_EXPERTS, LANE), F32)),
        grid=(n // tm,),
        in_specs=in_specs,
        out_specs=(pl.BlockSpec((tm, d), lambda i: (i + ht0, 0)),
                   pl.BlockSpec((INFO_ROWS, tm), lambda i: (0, i)),
                   pl.BlockSpec((N_EXPERTS, LANE), lambda i: (0, 0))),
        input_output_aliases=aliases,
        compiler_params=_cparams("arbitrary"),
        name=name,
    )(*args)


def _row_copy(src_ref, src_row, dst_ref, dst_row, sem):
    return pltpu.make_async_copy(src_ref.at[pl.ds(src_row, 1)], dst_ref.at[pl.ds(dst_row, 1)], sem)


def _dispatch_kernel(pos_ref, h_ref, xs_ref, sem):
    tm = h_ref.shape[0]

    def issue(rb, carry):
        base = pl.multiple_of(rb * ISSUE_UNROLL, ISSUE_UNROLL)
        for u in range(ISSUE_UNROLL):
            for k in range(TOP_K):
                _row_copy(h_ref, base + u, xs_ref, pos_ref[0, 0, (base + u) * TOP_K + k], sem).start(
                    priority=k % 2)
        return carry

    lax.fori_loop(0, tm // ISSUE_UNROLL, issue, 0)
    for _ in range(TOP_K):
        pltpu.make_async_copy(h_ref, xs_ref.at[pl.ds(0, tm)], sem).wait()


def _dispatch(pos, h, n_slots):
    n, d = h.shape
    tm = COPY_TILE
    return pl.pallas_call(
        _dispatch_kernel,
        out_shape=jax.ShapeDtypeStruct((n_slots, d), F32),
        grid=(n // tm,),
        in_specs=[
            pl.BlockSpec((1, 1, tm * TOP_K), lambda i: (i, 0, 0), memory_space=pltpu.SMEM),
            pl.BlockSpec((tm, d), lambda i: (i, 0)),
        ],
        out_specs=pl.BlockSpec(memory_space=pl.ANY),
        scratch_shapes=[pltpu.SemaphoreType.DMA],
        compiler_params=_cparams("arbitrary"),
        name="moe_dispatch",
    )(pos.reshape(n // tm, 1, tm * TOP_K), h)


PAD_ROWS = 8


def _padfill_kernel(lo_ref, mid_ref, hi_ref, xs_in, xs_ref, zero_ref, sem):
    del xs_in
    zero_ref[...] = jnp.zeros(zero_ref.shape, F32)
    for phase in range(2):
        for e in range(N_EXPERTS):
            def row(r, carry):
                cp = _row_copy(zero_ref, 0, xs_ref, r, sem)
                cp.start() if phase == 0 else cp.wait()
                return carry

            def block(b, carry):
                r = pl.multiple_of(mid_ref[e] + b * PAD_ROWS, PAD_ROWS)
                cp = pltpu.make_async_copy(zero_ref, xs_ref.at[pl.ds(r, PAD_ROWS)], sem)
                cp.start() if phase == 0 else cp.wait()
                return carry

            lax.fori_loop(lo_ref[e], mid_ref[e], row, 0)
            lax.fori_loop(0, (hi_ref[e] - mid_ref[e]) // PAD_ROWS, block, 0)


def _padfill(xs, lo, hi):
    mid = jnp.minimum(((lo + PAD_ROWS - 1) // PAD_ROWS) * PAD_ROWS, hi)
    return pl.pallas_call(
        _padfill_kernel,
        out_shape=jax.ShapeDtypeStruct(xs.shape, F32),
        grid_spec=pltpu.PrefetchScalarGridSpec(
            num_scalar_prefetch=3,
            grid=(1,),
            in_specs=[pl.BlockSpec(memory_space=pl.ANY)],
            out_specs=pl.BlockSpec(memory_space=pl.ANY),
            scratch_shapes=[pltpu.VMEM((PAD_ROWS,) + xs.shape[1:], F32), pltpu.SemaphoreType.DMA],
        ),
        input_output_aliases={3: 0},
        compiler_params=_cparams("arbitrary"),
        name="moe_padfill",
    )(lo, mid, hi, xs)


def _expert_kernel(te_ref, na_ref, xs_ref, wgu_ref, bgu_ref, wd_ref, bd_ref, y_ref, wgu_bf, wd_bf):
    i = pl.program_id(0)
    active = i < na_ref[0]
    first_of_expert = (i == 0) | (te_ref[i] != te_ref[jnp.maximum(i - 1, 0)])

    @pl.when(active & first_of_expert)
    def _():
        rows = 256
        for r in range(0, wgu_ref.shape[0], rows):
            wgu_bf[r:r + rows, :] = wgu_ref[r:r + rows, :].astype(BF16)
        for r in range(0, wd_ref.shape[0], rows):
            wd_bf[r:r + rows, :] = wd_ref[r:r + rows, :].astype(BF16)

    @pl.when(active)
    def _():
        x = xs_ref[...].astype(BF16)
        f = wd_ref.shape[0]
        half = f // EXPERT_SPLIT
        y = None
        for lo in range(0, f, half):
            gt = jnp.dot(x, wgu_bf[:, lo:lo + half], preferred_element_type=F32) + bgu_ref[:, lo:lo + half]
            up = jnp.dot(x, wgu_bf[:, f + lo:f + lo + half], preferred_element_type=F32)
            up = up + bgu_ref[:, f + lo:f + lo + half]
            gt = jnp.minimum(gt, SWIGLU_LIMIT)
            up = jnp.clip(up, -SWIGLU_LIMIT, SWIGLU_LIMIT)
            act = (up + 1.0) * gt * _sigmoid(SWIGLU_ALPHA * gt)
            t = jnp.dot(act.astype(BF16), wd_bf[lo:lo + half, :], preferred_element_type=F32)
            y = t if y is None else y + t
        y_ref[...] = y + bd_ref[...]

    @pl.when(jnp.logical_not(active))
    def _():
        y_ref[...] = jnp.zeros(y_ref.shape, F32)


def _experts(xs, tile_expert, n_active, layer, w_gu, b_gu, w_down, b_down):
    n_slots, d = xs.shape
    tm = EXPERT_TILE
    depth, n_e, _, two_f = w_gu.shape
    return pl.pallas_call(
        _expert_kernel,
        out_shape=jax.ShapeDtypeStruct(xs.shape, F32),
        grid_spec=pltpu.PrefetchScalarGridSpec(
            num_scalar_prefetch=2,
            grid=(n_slots // tm,),
            in_specs=[
                pl.BlockSpec((tm, d), lambda i, te, na: (jnp.minimum(i, na[0] - 1), 0)),
                pl.BlockSpec((None, None, d, two_f), lambda i, te, na: (layer, te[i], 0, 0)),
                pl.BlockSpec((None, None, 1, two_f), lambda i, te, na: (layer, te[i], 0, 0)),
                pl.BlockSpec((None, None, two_f // 2, d), lambda i, te, na: (layer, te[i], 0, 0)),
                pl.BlockSpec((None, None, 1, d), lambda i, te, na: (layer, te[i], 0, 0)),
            ],
            out_specs=pl.BlockSpec((tm, d), lambda i, te, na: (i, 0)),
            scratch_shapes=[pltpu.VMEM((d, two_f), BF16), pltpu.VMEM((two_f // 2, d), BF16)],
        ),
        compiler_params=_cparams("arbitrary"),
        name="moe_experts",
    )(tile_expert, n_active, xs, w_gu, b_gu.reshape(depth, n_e, 1, two_f), w_down,
      b_down.reshape(depth, n_e, 1, d))


def _combine_kernel(pos_ref, x_ref, wts_ref, gate_ref, *rest, final):
    if final:
        fg_ref, y_hbm, o_ref, ybuf, sem = rest
    else:
        y_hbm, o_ref, ybuf, sem = rest
    tm = x_ref.shape[0]

    def issue(rb, carry):
        base = pl.multiple_of(rb * ISSUE_UNROLL, ISSUE_UNROLL)
        for u in range(ISSUE_UNROLL):
            for k in range(TOP_K):
                _row_copy(y_hbm, pos_ref[0, 0, (base + u) * TOP_K + k], ybuf.at[k], base + u, sem).start(
                    priority=k % 2)
        return carry

    lax.fori_loop(0, tm // ISSUE_UNROLL, issue, 0)
    for k in range(TOP_K):
        pltpu.make_async_copy(y_hbm.at[pl.ds(0, tm)], ybuf.at[k], sem).wait()
    wts = wts_ref[...]
    f = wts[:, 0:1] * ybuf[0]
    for k in range(1, TOP_K):
        f = f + wts[:, k:k + 1] * ybuf[k]
    out = x_ref[...] + gate_ref[...] * f
    if final:
        out = out * lax.rsqrt(jnp.mean(out * out, axis=-1, keepdims=True) + NORM_EPS) * fg_ref[...]
    o_ref[...] = out


def _combine(x, g_start, pos, wts, mods, row_fn, y, final_g, *, name):
    n, d = x.shape
    tm = COPY_TILE
    gt0 = g_start // tm
    in_specs = [
        pl.BlockSpec((1, 1, tm * TOP_K), lambda i: (i + gt0, 0, 0), memory_space=pltpu.SMEM),
        pl.BlockSpec((tm, d), lambda i: (i, 0)),
        pl.BlockSpec((tm, TOP_K), lambda i: (i + gt0, 0)),
        _mod_spec(lambda i: row_fn(i * tm), 5, d),
    ]
    args = [pos.reshape(pos.shape[0] // tm, 1, tm * TOP_K), x, wts, mods]
    if final_g is not None:
        in_specs.append(pl.BlockSpec((1, d), lambda i: (0, 0)))
        args.append(final_g.reshape(1, d))
    in_specs.append(pl.BlockSpec(memory_space=pl.ANY))
    args.append(y)
    return pl.pallas_call(
        functools.partial(_combine_kernel, final=final_g is not None),
        out_shape=jax.ShapeDtypeStruct((n, d), F32),
        grid=(n // tm,),
        in_specs=in_specs,
        out_specs=pl.BlockSpec((tm, d), lambda i: (i, 0)),
        scratch_shapes=[pltpu.VMEM((TOP_K, tm) + y.shape[1:], F32), pltpu.SemaphoreType.DMA],
        compiler_params=_cparams("arbitrary"),
        name=name,
    )(*args)


def _moe(streams, layer, mods, norm_g, w_r, b_r, w_gu, b_gu, w_down, b_down, final_g):
    tm = EXPERT_TILE
    w_rt = w_r.T
    b_col = b_r.reshape(N_EXPERTS, 1)
    tok = np.arange(ROW_TILE)
    tri = jnp.asarray(tok[:, None] < tok[None, :], BF16)
    n_total = sum(x.shape[0] for x, _ in streams)
    cnt = jnp.zeros((N_EXPERTS, LANE), F32)
    h, infos, g_start = None, [], 0
    for s, (x, row_fn) in enumerate(streams):
        h, info, cnt = _router(x, norm_g, mods, row_fn, w_rt, b_col, tri, cnt, h, n_total, g_start,
                               name=f"moe_router_{s}")
        infos.append(info)
        g_start += x.shape[0]
    info = jnp.concatenate(infos, axis=1) if len(infos) > 1 else infos[0]
    counts = cnt[:, 0].astype(jnp.int32)
    padded = ((counts + tm - 1) // tm) * tm
    ends = jnp.cumsum(padded)
    offsets = ends - padded
    n_slots = n_total * TOP_K + N_EXPERTS * tm
    n_active = (ends[-1] // tm).astype(jnp.int32).reshape(1)
    tile_start = jnp.minimum(jnp.arange(n_slots // tm, dtype=jnp.int32), n_active[0] - 1) * tm
    tile_expert = jnp.sum((ends[None, :] <= tile_start[:, None]).astype(jnp.int32), axis=1)
    tile_expert = jnp.minimum(tile_expert, N_EXPERTS - 1)
    eid = info[INFO_ID:INFO_ID + TOP_K].astype(jnp.int32)
    rank = info[INFO_RANK:INFO_RANK + TOP_K].astype(jnp.int32)
    base = jnp.zeros_like(eid)
    for e in range(N_EXPERTS):
        base = jnp.where(eid == e, offsets[e], base)
    pos = (base + rank).T
    wts = info[INFO_W:INFO_W + TOP_K].T

    slots = _dispatch(pos, h, n_slots)
    slots = _padfill(slots, (offsets + counts).astype(jnp.int32), ends.astype(jnp.int32))
    y = _experts(slots, tile_expert, n_active, layer, w_gu, b_gu, w_down, b_down)
    outs, g_start = [], 0
    for s, (x, row_fn) in enumerate(streams):
        outs.append(_combine(x, g_start, pos, wts, mods, row_fn, y, final_g, name=f"moe_combine_{s}"))
        g_start += x.shape[0]
    return outs


def _pad_heads(w):
    lead = w.shape[:-1]
    w4 = w.reshape(*lead, GLA_HEADS, GLA_DK)
    w4 = jnp.pad(w4, [(0, 0)] * len(lead) + [(0, 0), (0, HEAD_PAD - GLA_DK)])
    return w4.reshape(*lead, QK_PAD)


def _pack_gla_in(w_in):
    d = w_in.shape[0]
    lr = jnp.pad(w_in[:, OFF_AF:OFF_AB + GLA_LOWRANK], ((0, 0), (0, LANE - 2 * GLA_LOWRANK)))
    return jnp.concatenate([
        _pad_heads(w_in[:, OFF_Q:OFF_Q + GLA_QK]),
        _pad_heads(w_in[:, OFF_K:OFF_K + GLA_QK]),
        w_in[:, OFF_V:OFF_V + GLA_V],
        w_in[:, OFF_G:OFF_G + GLA_V],
        w_in[:, OFF_GLU:OFF_GLU + 2 * CONV_CH],
        lr,
    ], axis=1).astype(BF16)


def _pack_decay(wa_f, ba_f, wa_b, ba_b):
    wa = jnp.zeros((LANE, 2 * QK_PAD), F32)
    wa = wa.at[0:GLA_LOWRANK, 0:QK_PAD].set(_pad_heads(wa_f))
    wa = wa.at[GLA_LOWRANK:2 * GLA_LOWRANK, QK_PAD:].set(_pad_heads(wa_b))
    ba = jnp.concatenate([_pad_heads(ba_f), _pad_heads(ba_b)]).reshape(1, 2 * QK_PAD)
    return wa.astype(BF16), ba


def _rope_tables(t_len):
    t = jnp.arange(t_len)
    row_pos = (t // GRID_W).astype(F32)
    col_pos = (t % GRID_W).astype(F32)
    half = GLA_DK // 4
    inv_freq = ROPE_BASE ** (-jnp.arange(half, dtype=F32) / half)
    dim = np.arange(HEAD_PAD)
    real = dim < GLA_DK
    use_col = (dim % GLA_DK) >= GLA_DK // 2
    first = (dim % (GLA_DK // 2)) < half
    pos = jnp.where(jnp.asarray(use_col)[None, :], col_pos[:, None], row_pos[:, None])
    ang = pos * inv_freq[dim % half][None, :]
    cos = jnp.where(jnp.asarray(real)[None, :], jnp.cos(ang), 0.0)
    sin = jnp.where(jnp.asarray(real)[None, :], jnp.sin(ang), 0.0)
    sin = jnp.where(jnp.asarray(first)[None, :], -sin, sin)
    return jnp.tile(cos, (1, GLA_HEADS)), jnp.tile(sin, (1, GLA_HEADS))


def kernel(x, c, ctx, c_ctx, ada_w, ada_b, norm1_g, norm2_g, gla_conv_w_in, gla_wa_fwd, gla_ba_fwd,
           gla_wa_bwd, gla_ba_bwd, gla_norm_g, conv_dw_w, conv_dw_b, conv_ln_g, conv_ln_b,
           gla_conv_w_out, na_w_qkv, na_rpb, na_w_out, router_w, router_b, expert_w_gu, expert_b_gu,
           expert_w_down, expert_b_down, final_norm_g):
    b, t_len, d = x.shape
    l_ctx = ctx.shape[1]
    depth = ada_w.shape[0]
    assert b + 1 <= MOD_ROWS and t_len % COPY_TILE == 0 and COPY_TILE % ROW_TILE == 0
    assert (b * l_ctx) % COPY_TILE == 0

    c_all = jnp.concatenate([c, c_ctx[None, :], jnp.zeros((MOD_ROWS - b - 1, d), F32)], axis=0)
    mods_all = _mods(c_all, ada_w, ada_b)

    x_lat = x.reshape(b * t_len, d)
    x_ctx = ctx.reshape(b * l_ctx, d)
    tm_lat = min(1024, t_len)
    tm_ctx = min(512, l_ctx)

    def lat_row(i, tm=tm_lat):
        return (i * tm) // t_len

    def ctx_row(i):
        return b

    def lat_token_row(t):
        return t // t_len

    def ctx_token_row(t):
        return b

    for layer in range(depth):
        last = layer == depth - 1
        j = layer // 2
        mods = mods_all[layer]
        if layer % 2 == 0:
            w_pack = _pack_gla_in(gla_conv_w_in[j])
            wa, ba = _pack_decay(gla_wa_fwd[j], gla_ba_fwd[j], gla_wa_bwd[j], gla_ba_bwd[j])
            w_out = gla_conv_w_out[j].astype(BF16)
            zero_state = jnp.zeros((b, GLA_HEADS, GLA_DV, HEAD_PAD), F32)
            p_c = _norm_mod_matmul(x_ctx, norm1_g[layer], mods, ctx_row, (1, 0), w_pack, tm=tm_ctx,
                                   gla_layout=True, name="gla_in_ctx")
            y_gla_c, s_f, s_b = _gla(p_c, l_ctx, wa, ba, gla_norm_g[j], zero_state, zero_state)
            p_l = _norm_mod_matmul(x_lat, norm1_g[layer], mods, lat_row, (1, 0), w_pack, tm=tm_lat,
                                   gla_layout=True, rope=_rope_tables(t_len), name="gla_in_lat")
            y_gla_l, _, _ = _gla(p_l, t_len, wa, ba, gla_norm_g[j], s_f, s_b)
            y_conv_l = _conv(p_l, t_len, conv_dw_w[j], conv_dw_b[j], conv_ln_g[j], conv_ln_b[j])
            x_lat = _matmul_residual([y_gla_l, y_conv_l], w_out, x_lat, mods, lat_row, 2, tm=tm_lat,
                                     name="mix_out_lat")
            if not last:
                y_conv_c = _conv(p_c, l_ctx, conv_dw_w[j], conv_dw_b[j], conv_ln_g[j], conv_ln_b[j])
                x_ctx = _matmul_residual([y_gla_c, y_conv_c], w_out, x_ctx, mods, ctx_row, 2, tm=tm_ctx,
                                         name="mix_out_ctx")
        else:
            w_qkv = na_w_qkv[j].astype(BF16)
            kv_c = _norm_mod_matmul(x_ctx, norm1_g[layer], mods, ctx_row, (1, 0), w_qkv[:, NA_WIDTH:],
                                    tm=tm_ctx, name="na_kv_ctx")
            qkv = _norm_mod_matmul(x_lat, norm1_g[layer], mods, lat_row, (1, 0), w_qkv, tm=tm_lat,
                                   name="na_qkv_lat")
            o_l = _neighbourhood_attention(qkv, kv_c, na_rpb[j], t_len, l_ctx)
            x_lat = _matmul_residual([o_l], na_w_out[j].astype(BF16), x_lat, mods, lat_row, 2, tm=tm_lat,
                                     name="na_out_lat")
            if not last:
                raise NotImplementedError("context output of an attention layer is only needed mid-stack")
        streams = [(x_lat, lat_token_row)]
        if not last:
            streams.append((x_ctx, ctx_token_row))
        outs = _moe(streams, layer, mods, norm2_g[layer], router_w[layer], router_b[layer],
                    expert_w_gu, expert_b_gu, expert_w_down, expert_b_down, final_norm_g if last else None)
        x_lat = outs[0]
        if not last:
            x_ctx = outs[1]
    return x_lat.reshape(b, t_len, d)
```
